```python
import jax
import jax.numpy as jnp
from jax import lax
import numpy as np

D_MODEL = 2048
BATCH = 4
SEQ = 2048
DEPTH = 2
DEC_BATCH = 32
DEC_SEQ = 1
PAST_LEN = 8192
PAGE_SIZE = 128

N_MIXERS = 4
GROUP_WIDTH = D_MODEL // N_MIXERS
HEAD_DIM = 64
GROUP_HEADS = GROUP_WIDTH // HEAD_DIM
TM_LORA = 32
A_PROJ = 3 * GROUP_WIDTH + 3 * TM_LORA
B_PROJ = 3 * GROUP_WIDTH
C_PROJ = 2 * GROUP_WIDTH
D_PROJ = 2 * GROUP_WIDTH
IN_WIDTH = A_PROJ + B_PROJ + C_PROJ + D_PROJ
MOBA_BLOCK = 256
MOBA_TOPK = 3
MOBA_Q_CHUNK = 64
SG_CHUNK = 128
CONV_WIDTH = 31
FFN_HIDDEN = -(-(8 * D_MODEL) // (3 * 256)) * 256
RMS_EPS = 1e-6
LN_EPS = 1e-5
GN_EPS = 64e-5

kernel_name = 'hybrid_rwkv7_moba_sgu_conformer_decode_step'


def rms_norm(x, g):
    xf = x.astype(jnp.float32)
    return xf * lax.rsqrt(jnp.mean(xf * xf, axis=-1, keepdims=True) + RMS_EPS) * g


def layer_norm(x, g, b, eps=LN_EPS):
    xf = x.astype(jnp.float32)
    mu = jnp.mean(xf, axis=-1, keepdims=True)
    var = jnp.mean(jnp.square(xf - mu), axis=-1, keepdims=True)
    return (xf - mu) * lax.rsqrt(var + eps) * g + b


def alibi_slopes(n_heads):
    return 2.0 ** (-8.0 * jnp.arange(1, n_heads + 1, dtype=jnp.float32) / n_heads)


def rwkv7_time_mix(P, l, p, shift_prev, wkv_prev):
    bn, t, _ = p.shape
    p = p.astype(jnp.float32)
    prev = jnp.concatenate([shift_prev[:, None, :].astype(jnp.float32), p[:, :-1]], axis=1)
    xm = p + (prev - p) * P['tm_mu'][l]
    gw, lr = GROUP_WIDTH, TM_LORA
    r, k, v, w_low, a_low, g_low = jnp.split(
        xm, [gw, 2 * gw, 3 * gw, 3 * gw + lr, 3 * gw + 2 * lr], axis=-1)
    w_log = -jax.nn.softplus(-(P['tm_w0'][l] + jnp.tanh(w_low) @ P['tm_w_up'][l])) - 0.5
    decay = jnp.exp(-jnp.exp(w_log))
    a = jax.nn.sigmoid(P['tm_a0'][l] + a_low @ P['tm_a_up'][l])
    g = jax.nn.sigmoid(g_low) @ P['tm_g_up'][l]
    heads = lambda z: z.reshape(bn, t, GROUP_HEADS, HEAD_DIM)
    kk = heads(k * P['tm_k_k'][l])
    kk = kk / jnp.maximum(jnp.linalg.norm(kk, axis=-1, keepdims=True), 1e-12)
    k = heads(k * (1.0 + (a - 1.0) * P['tm_k_a'][l]))
    r, v, decay, a = heads(r), heads(v), heads(decay), heads(a)
    tmaj = lambda z: jnp.moveaxis(z, 1, 0)

    def step(S, inp):
        r_t, w_t, k_t, v_t, kk_t, a_t = inp
        sa = jnp.einsum('bhvk,bhk->bhv', S, -kk_t)
        S = (S * w_t[:, :, None, :] + sa[..., None] * (kk_t * a_t)[:, :, None, :]
             + v_t[..., None] * k_t[:, :, None, :])
        return S, jnp.einsum('bhvk,bhk->bhv', S, r_t)

    wkv_last, o = lax.scan(step, wkv_prev.astype(jnp.float32),
                           (tmaj(r), tmaj(decay), tmaj(k), tmaj(v), tmaj(kk), tmaj(a)))
    o = jnp.moveaxis(o, 0, 1)
    mu = jnp.mean(o, axis=-1, keepdims=True)
    var = jnp.mean(jnp.square(o - mu), axis=-1, keepdims=True)
    o = ((o - mu) * lax.rsqrt(var + GN_EPS)).reshape(bn, t, gw) * P['tm_ln_g'][l] + P['tm_ln_b'][l]
    bonus = jnp.sum(r * k * P['tm_r_k'][l], axis=-1, keepdims=True) * v
    out = (o + bonus.reshape(bn, t, gw)) * g
    return out, p[:, -1], wkv_last


def moba_attention(q, k, v, q_pos0):
    bn, nh, tq, hd = q.shape
    tk = k.shape[2]
    nb = -(-tk // MOBA_BLOCK)
    pad_k = ((0, 0), (0, 0), (0, nb * MOBA_BLOCK - tk), (0, 0))
    kb = jnp.pad(k, pad_k).reshape(bn, nh, nb, MOBA_BLOCK, hd)
    vb = jnp.pad(v, pad_k).reshape(bn, nh, nb, MOBA_BLOCK, hd)
    k_mean = jnp.mean(kb.astype(jnp.float32), axis=3)
    qc = min(MOBA_Q_CHUNK, tq)
    nq = -(-tq // qc)
    qs = jnp.pad(q, ((0, 0), (0, 0), (0, nq * qc - tq), (0, 0))).reshape(bn, nh, nq, qc, hd)
    qs = jnp.moveaxis(qs, 2, 0)
    n_sel = min(MOBA_TOPK, nb - 1)
    slopes = alibi_slopes(nh)
    scale = hd ** -0.5
    b_ix = jnp.arange(bn)[:, None, None, None]
    h_ix = jnp.arange(nh)[None, :, None, None]
    blk_pos = jnp.arange(MOBA_BLOCK)

    def one_chunk(args):
        qb, ci = args
        qf = qb.astype(jnp.float32) * scale
        start = q_pos0 + ci * qc
        own = start // MOBA_BLOCK
        q_pos = start + jnp.arange(qc)
        k_own = lax.dynamic_index_in_dim(kb, own, axis=2, keepdims=False).astype(jnp.float32)
        v_own = lax.dynamic_index_in_dim(vb, own, axis=2, keepdims=False).astype(jnp.float32)
        dist_own = (q_pos[:, None] - (own * MOBA_BLOCK + blk_pos)[None, :]).astype(jnp.float32)
        s_own = jnp.einsum('bhqd,bhkd->bhqk', qf, k_own) - slopes[:, None, None] * dist_own
        s_own = jnp.where(dist_own >= 0, s_own, -jnp.inf)
        if n_sel == 0:
            p = jax.nn.softmax(s_own, axis=-1)
            return jnp.einsum('bhqk,bhkd->bhqd', p, v_own)
        gate = jnp.einsum('bhqd,bhnd->bhqn', qf, k_mean)
        gate = jnp.where(jnp.arange(nb) < own, gate, -jnp.inf)
        _, sel = lax.top_k(gate, n_sel)
        k_sel = kb[b_ix, h_ix, sel].astype(jnp.float32)
        v_sel = vb[b_ix, h_ix, sel].astype(jnp.float32)
        dist_sel = (q_pos[:, None, None] - (sel[..., None] * MOBA_BLOCK + blk_pos)).astype(jnp.float32)
        s_sel = jnp.einsum('bhqd,bhqnkd->bhqnk', qf, k_sel) - slopes[:, None, None, None] * dist_sel
        s_sel = jnp.where((sel < own)[..., None], s_sel, -jnp.inf)
        logits = jnp.concatenate([s_own, s_sel.reshape(bn, nh, qc, n_sel * MOBA_BLOCK)], axis=-1)
        p = jax.nn.softmax(logits, axis=-1)
        p_sel = p[..., MOBA_BLOCK:].reshape(bn, nh, qc, n_sel, MOBA_BLOCK)
        return (jnp.einsum('bhqk,bhkd->bhqd', p[..., :MOBA_BLOCK], v_own)
                + jnp.einsum('bhqnk,bhqnkd->bhqd', p_sel, v_sel))

    out = lax.map(one_chunk, (qs, jnp.arange(nq)))
    out = jnp.moveaxis(out, 0, 2).reshape(bn, nh, nq * qc, hd)
    return out[:, :, :tq]


def spatial_gating(P, l, pc):
    u, v = jnp.split(jax.nn.gelu(pc.astype(jnp.float32)), 2, axis=-1)
    v = layer_norm(v, P['sg_ln_g'][l], P['sg_ln_b'][l])
    bn, t, _ = v.shape
    cl = min(SG_CHUNK, t)
    n_chunks = -(-t // cl)
    vp = jnp.pad(v, ((0, 0), (0, n_chunks * cl - t), (0, 0))).reshape(
        bn, n_chunks, cl, GROUP_HEADS, HEAD_DIM)
    w = P['sg_ws'][l][:, :cl, :cl] * jnp.tril(jnp.ones((cl, cl), jnp.float32))
    bias = P['sg_bs'][l][:, :cl].T
    mixed = jnp.einsum('hij,bnjhd->bnihd', w, vp) + bias[None, None, :, :, None]
    mixed = mixed.reshape(bn, n_chunks * cl, GROUP_WIDTH)[:, :t]
    return u * mixed, v


def conformer_conv(P, l, pd, conv_prev):
    a, gte = jnp.split(pd.astype(jnp.float32), 2, axis=-1)
    z = a * jax.nn.sigmoid(gte)
    zp = jnp.concatenate([conv_prev.astype(jnp.float32), z], axis=1)
    w = P['cv_w'][l].astype(jnp.float32)[:, None, :]
    y = lax.conv_general_dilated(zp, w, window_strides=(1,), padding='VALID',
                                 dimension_numbers=('NWC', 'WIO', 'NWC'),
                                 feature_group_count=GROUP_WIDTH) + P['cv_b'][l]
    y = jax.nn.silu(layer_norm(y, P['cv_ln_g'][l], P['cv_ln_b'][l]))
    return y, zp[:, zp.shape[1] - (CONV_WIDTH - 1):]


def decoder_layer(P, l, x, c, shift_prev, wkv_prev, k_past, v_past, conv_prev):
    bn, t, _ = x.shape
    mod = jax.nn.silu(c.astype(jnp.float32)) @ P['w_ada'][l] + P['b_ada'][l]
    sh_m, sc_m, gt_m, sh_f, sc_f, gt_f = jnp.split(mod[:, None, :], 6, axis=-1)
    h = rms_norm(x, P['g_norm_mix'][l]) * (1.0 + sc_m) + sh_m
    proj = h @ P['w_in'][l]
    pa, pb, pc, pd = jnp.split(proj, [A_PROJ, A_PROJ + B_PROJ, A_PROJ + B_PROJ + C_PROJ], axis=-1)
    ya, shift_new, wkv_new = rwkv7_time_mix(P, l, pa, shift_prev, wkv_prev)
    q, k, v = (z.reshape(bn, t, GROUP_HEADS, HEAD_DIM) for z in jnp.split(pb, 3, axis=-1))
    k_all = jnp.concatenate([k_past, k.transpose(0, 2, 1, 3)], axis=2)
    v_all = jnp.concatenate([v_past, v.transpose(0, 2, 1, 3)], axis=2)
    yb = moba_attention(q.transpose(0, 2, 1, 3), k_all, v_all, k_past.shape[2])
    yb = yb.transpose(0, 2, 1, 3).reshape(bn, t, GROUP_WIDTH)
    yc, v_sg = spatial_gating(P, l, pc)
    yd, conv_new = conformer_conv(P, l, pd, conv_prev)
    y_mix = jnp.concatenate([ya, yb, yc, yd], axis=-1) @ P['w_out'][l]
    x = x + gt_m * y_mix
    h2 = rms_norm(x, P['g_norm_ffn'][l]) * (1.0 + sc_f) + sh_f
    gate, up = jnp.split(h2 @ P['w_ffn_in'][l], 2, axis=-1)
    x = x + gt_f * ((jax.nn.silu(gate) * up) @ P['w_ffn_out'][l])
    return x, (shift_new, wkv_new, k, v, v_sg, conv_new)


def setup_inputs(seed: int = 0) -> dict:
    key = jax.random.key(seed)
    ks = jax.random.split(key, 64)
    counter = iter(range(64))

    def nk():
        return ks[next(counter)]

    def nrm(shape, scale):
        return jax.random.normal(nk(), shape, jnp.float32) * scale

    def gain(shape):
        return 1.0 + nrm(shape, 0.02)

    n_pages = PAST_LEN // PAGE_SIZE
    n_used = DEC_BATCH * n_pages
    n_pool = n_used + max(1, n_used // 4)
    perm = jax.random.permutation(nk(), n_pool)
    page_table = perm[:n_used].reshape(DEC_BATCH, n_pages).astype(jnp.int32)
    gw, gh, hd, lr = GROUP_WIDTH, GROUP_HEADS, HEAD_DIM, TM_LORA
    return {
        'x_prompt': nrm((BATCH, SEQ, D_MODEL), 1.0),
        'x_sample': nrm((DEC_BATCH, DEC_SEQ, D_MODEL), 1.0),
        'c_prompt': nrm((BATCH, D_MODEL), 1.0),
        'c_sample': nrm((DEC_BATCH, D_MODEL), 1.0),
        'cache_k_b': nrm((DEPTH, n_pool, PAGE_SIZE, gh, hd), 1.0),
        'cache_v_b': nrm((DEPTH, n_pool, PAGE_SIZE, gh, hd), 1.0),
        'state_a_wkv': nrm((DEPTH, DEC_BATCH, gh, hd, hd), 1.0),
        'state_a_shift': nrm((DEPTH, DEC_BATCH, A_PROJ), 1.0),
        'state_d_conv': nrm((DEPTH, DEC_BATCH, CONV_WIDTH - 1, gw), 0.5),
        'page_table': page_table,
        'w_ada': nrm((DEPTH, D_MODEL, 6 * D_MODEL), 0.5 * D_MODEL ** -0.5),
        'b_ada': nrm((DEPTH, 6 * D_MODEL), 0.02),
        'g_norm_mix': gain((DEPTH, D_MODEL)),
        'w_in': nrm((DEPTH, D_MODEL, IN_WIDTH), D_MODEL ** -0.5),
        'w_out': nrm((DEPTH, D_MODEL, D_MODEL), D_MODEL ** -0.5),
        'tm_mu': jax.random.uniform(nk(), (DEPTH, A_PROJ), jnp.float32, 0.0, 1.0),
        'tm_w0': jax.random.uniform(nk(), (DEPTH, gw), jnp.float32, -6.0, -1.0),
        'tm_w_up': nrm((DEPTH, lr, gw), 0.1),
        'tm_a0': nrm((DEPTH, gw), 0.1),
        'tm_a_up': nrm((DEPTH, lr, gw), 0.1),
        'tm_g_up': nrm((DEPTH, lr, gw), lr ** -0.5),
        'tm_k_k': 0.85 + nrm((DEPTH, gw), 0.02),
        'tm_k_a': gain((DEPTH, gw)),
        'tm_r_k': nrm((DEPTH, gh, hd), 0.1),
        'tm_ln_g': gain((DEPTH, gw)),
        'tm_ln_b': nrm((DEPTH, gw), 0.02),
        'sg_ws': nrm((DEPTH, gh, SG_CHUNK, SG_CHUNK), SG_CHUNK ** -0.5),
        'sg_bs': gain((DEPTH, gh, SG_CHUNK)),
        'sg_ln_g': gain((DEPTH, gw)),
        'sg_ln_b': nrm((DEPTH, gw), 0.02),
        'cv_w': nrm((DEPTH, CONV_WIDTH, gw), CONV_WIDTH ** -0.5),
        'cv_b': nrm((DEPTH, gw), 0.02),
        'cv_ln_g': gain((DEPTH, gw)),
        'cv_ln_b': nrm((DEPTH, gw), 0.02),
        'g_norm_ffn': gain((DEPTH, D_MODEL)),
        'w_ffn_in': nrm((DEPTH, D_MODEL, 2 * FFN_HIDDEN), D_MODEL ** -0.5),
        'w_ffn_out': nrm((DEPTH, FFN_HIDDEN, D_MODEL), FFN_HIDDEN ** -0.5),
        'g_norm_final': gain((D_MODEL,)),
    }


def reference(x_prompt, x_sample, c_prompt, c_sample, cache_k_b, cache_v_b, state_a_wkv,
              state_a_shift, state_d_conv, page_table, w_ada, b_ada, g_norm_mix, w_in, w_out,
              tm_mu, tm_w0, tm_w_up, tm_a0, tm_a_up, tm_g_up, tm_k_k, tm_k_a, tm_r_k, tm_ln_g,
              tm_ln_b, sg_ws, sg_bs, sg_ln_g, sg_ln_b, cv_w, cv_b, cv_ln_g, cv_ln_b, g_norm_ffn,
              w_ffn_in, w_ffn_out, g_norm_final):
    P = dict(w_ada=w_ada, b_ada=b_ada, g_norm_mix=g_norm_mix, w_in=w_in, w_out=w_out,
             tm_mu=tm_mu, tm_w0=tm_w0, tm_w_up=tm_w_up, tm_a0=tm_a0, tm_a_up=tm_a_up,
             tm_g_up=tm_g_up, tm_k_k=tm_k_k, tm_k_a=tm_k_a, tm_r_k=tm_r_k, tm_ln_g=tm_ln_g,
             tm_ln_b=tm_ln_b, sg_ws=sg_ws, sg_bs=sg_bs, sg_ln_g=sg_ln_g, sg_ln_b=sg_ln_b,
             cv_w=cv_w, cv_b=cv_b, cv_ln_g=cv_ln_g, cv_ln_b=cv_ln_b, g_norm_ffn=g_norm_ffn,
             w_ffn_in=w_ffn_in, w_ffn_out=w_ffn_out)
    gh, hd = GROUP_HEADS, HEAD_DIM
    n_seq, n_pg = page_table.shape
    bp = x_prompt.shape[0]
    xp, xs = x_prompt, x_sample
    st_p, st_s = [], []
    for l in range(DEPTH):
        xp, sp = decoder_layer(
            P, l, xp, c_prompt,
            jnp.zeros((bp, A_PROJ), jnp.float32),
            jnp.zeros((bp, gh, hd, hd), jnp.float32),
            jnp.zeros((bp, gh, 0, hd), jnp.float32),
            jnp.zeros((bp, gh, 0, hd), jnp.float32),
            jnp.zeros((bp, CONV_WIDTH - 1, GROUP_WIDTH), jnp.float32))
        k_past = cache_k_b[l][page_table].reshape(n_seq, n_pg * PAGE_SIZE, gh, hd).transpose(0, 2, 1, 3)
        v_past = cache_v_b[l][page_table].reshape(n_seq, n_pg * PAGE_SIZE, gh, hd).transpose(0, 2, 1, 3)
        xs, ss = decoder_layer(P, l, xs, c_sample, state_a_shift[l], state_a_wkv[l],
                               k_past, v_past, state_d_conv[l])
        st_p.append(sp)
        st_s.append(ss)
    y_prompt = rms_norm(xp, g_norm_final)
    y_sample = rms_norm(xs, g_norm_final)
    shift_prompt = jnp.stack([s[0] for s in st_p])
    shift_sample = jnp.stack([s[0] for s in st_s])
    wkv_prompt = jnp.stack([s[1] for s in st_p])
    wkv_sample = jnp.stack([s[1] for s in st_s])
    new_k_prompt = jnp.stack([s[2] for s in st_p])
    new_k_sample = jnp.stack([s[2] for s in st_s])
    new_v_prompt = jnp.stack([s[3] for s in st_p])
    new_v_sample = jnp.stack([s[3] for s in st_s])
    sg_v_sample = jnp.stack([s[4] for s in st_s])
    conv_prompt = jnp.stack([s[5] for s in st_p])
    conv_sample = jnp.stack([s[5] for s in st_s])
    return (y_prompt, y_sample, new_k_prompt, new_v_prompt, new_k_sample, new_v_sample,
            wkv_prompt, wkv_sample, shift_prompt, shift_sample, conv_prompt, conv_sample,
            sg_v_sample)
```

```python
import functools

import jax
import jax.numpy as jnp
from jax import lax
from jax.experimental import pallas as pl
from jax.experimental.pallas import tpu as pltpu

F32, BF16 = jnp.float32, jnp.bfloat16

D_MODEL = 2048
GROUP_WIDTH = 512
N_HEADS = 8
HEAD_DIM = 64
TM_LORA = 32
A_PROJ = 3 * GROUP_WIDTH + 3 * TM_LORA
IN_WIDTH = A_PROJ + 7 * GROUP_WIDTH
IN_PAD = 11 * GROUP_WIDTH
MOBA_BLOCK = 256
MOBA_TOPK = 3
SG_CHUNK = 128
CONV_WIDTH = 31
CONV_HALO = 32
RMS_EPS = 1e-6
LN_EPS = 1e-5
GN_EPS = 64e-5
RWKV_CHUNK = 64
INV_BLOCK = 16
LANES = 128

CB_Q, CB_K, CB_V, CB_SU, CB_SV, CB_CA, CB_CG, CB_R, CB_RK, CB_RV, CB_L = range(11)
(V_MU_R, V_MU_K, V_MU_V, V_W0, V_A0, V_KK, V_KA, V_RK, V_LNG, V_LNB, V_MU_L) = range(11)

VMEM_LIMIT = 56 * 1024 * 1024


def _cp(*sem):
    return pltpu.CompilerParams(dimension_semantics=sem, vmem_limit_bytes=VMEM_LIMIT)


def _dot(a, b):
    return jnp.dot(a, b, preferred_element_type=F32)


def _split2(x):
    hi = x.astype(BF16)
    return hi, (x - hi.astype(F32)).astype(BF16)


def _split3(x):
    h1 = x.astype(BF16)
    r1 = x - h1.astype(F32)
    h2 = r1.astype(BF16)
    return h1, h2, (r1 - h2.astype(F32)).astype(BF16)


def _dot_exact_rhs(x, e):
    hi, lo = _split2(x)
    return _dot(hi, e) + _dot(lo, e)


def _head_ones():
    r = lax.broadcasted_iota(jnp.int32, (GROUP_WIDTH, GROUP_WIDTH), 0) // HEAD_DIM
    c = lax.broadcasted_iota(jnp.int32, (GROUP_WIDTH, GROUP_WIDTH), 1) // HEAD_DIM
    return (r == c).astype(BF16)


def _layer_norm(x, g, b):
    mu = jnp.mean(x, axis=-1, keepdims=True)
    d = x - mu
    var = jnp.mean(d * d, axis=-1, keepdims=True)
    return d * lax.rsqrt(var + LN_EPS) * g + b


def _softplus(x):
    return jnp.maximum(x, 0.0) + jnp.log1p(jnp.exp(-jnp.abs(x)))


def _ada_kernel(c_ref, w_ref, b_ref, o_ref):
    c = c_ref[...]
    s = (c * jax.nn.sigmoid(c)).astype(BF16)
    o_ref[0] = _dot(s, w_ref[0].astype(BF16)) + b_ref[0]


def _ada(c_all, w_ada, b_ada):
    depth, d, n = w_ada.shape
    r = c_all.shape[0]
    tn = 1024
    return pl.pallas_call(
        _ada_kernel,
        grid=(depth, n // tn),
        in_specs=[pl.BlockSpec((r, d), lambda l, j: (0, 0)),
                  pl.BlockSpec((1, d, tn), lambda l, j: (l, 0, j)),
                  pl.BlockSpec((1, 1, tn), lambda l, j: (l, 0, j))],
        out_specs=pl.BlockSpec((1, r, tn), lambda l, j: (l, 0, j)),
        out_shape=jax.ShapeDtypeStruct((depth, r, n), F32),
        compiler_params=_cp("arbitrary", "arbitrary"),
        name="ada",
    )(c_all, w_ada, b_ada.reshape(depth, 1, n))


def _norm_mod_kernel(x_ref, g_ref, sc_ref, sh_ref, o_ref):
    x = x_ref[0]
    h = x * lax.rsqrt(jnp.mean(x * x, axis=-1, keepdims=True) + RMS_EPS) * g_ref[...]
    o_ref[0] = (h * (1.0 + sc_ref[0]) + sh_ref[0]).astype(o_ref.dtype)


def _norm_mod(x, g, mod, sh_col, sc_col, tt):
    ng, t, d = x.shape
    r = mod.shape[1]
    return pl.pallas_call(
        _norm_mod_kernel,
        grid=(ng, t // tt),
        in_specs=[pl.BlockSpec((1, tt, d), lambda b, i: (b, i, 0)),
                  pl.BlockSpec((1, d), lambda b, i: (0, 0)),
                  pl.BlockSpec((1, r, d), lambda b, i: (b, 0, sc_col)),
                  pl.BlockSpec((1, r, d), lambda b, i: (b, 0, sh_col))],
        out_specs=pl.BlockSpec((1, tt, d), lambda b, i: (b, i, 0)),
        out_shape=jax.ShapeDtypeStruct((ng, t, d), BF16),
        compiler_params=_cp("arbitrary", "arbitrary"),
        name="norm_mod",
    )(x, g.reshape(1, d), mod, mod)


def _norm_kernel(x_ref, g_ref, o_ref):
    x = x_ref[0]
    o_ref[0] = x * lax.rsqrt(jnp.mean(x * x, axis=-1, keepdims=True) + RMS_EPS) * g_ref[...]


def _norm(x, g, tt):
    ng, t, d = x.shape
    return pl.pallas_call(
        _norm_kernel,
        grid=(ng, t // tt),
        in_specs=[pl.BlockSpec((1, tt, d), lambda b, i: (b, i, 0)),
                  pl.BlockSpec((1, d), lambda b, i: (0, 0))],
        out_specs=pl.BlockSpec((1, tt, d), lambda b, i: (b, i, 0)),
        out_shape=jax.ShapeDtypeStruct((ng, t, d), F32),
        compiler_params=_cp("arbitrary", "arbitrary"),
        name="norm_final",
    )(x, g.reshape(1, d))


def _mm_kernel(*refs, n_x, gated):
    xs, w_ref = refs[:n_x], refs[n_x]
    o_ref = refs[-1]
    acc = None
    k0 = 0
    for x_ref in xs:
        k = x_ref.shape[1]
        part = _dot(x_ref[...], w_ref[k0:k0 + k, :])
        acc = part if acc is None else acc + part
        k0 += k
    if gated:
        res_ref, gt_ref = refs[n_x + 1], refs[n_x + 2]
        o_ref[...] = res_ref[...] + gt_ref[0] * acc
    else:
        o_ref[...] = acc


def _mm(xs, w, tm, tn, res=None, gate=None, gate_col=0, rows_per_group=None):
    m = xs[0].shape[0]
    n = w.shape[1]
    gated = res is not None
    in_specs = [pl.BlockSpec((tm, x.shape[1]), lambda i, j: (i, 0)) for x in xs]
    in_specs.append(pl.BlockSpec((w.shape[0], tn), lambda i, j: (0, j)))
    args = list(xs) + [w]
    if gated:
        r = gate.shape[1]
        tiles_per_group = rows_per_group // tm
        nj = n // tn
        in_specs.append(pl.BlockSpec((tm, tn), lambda i, j: (i, j)))
        in_specs.append(pl.BlockSpec((1, r, tn), lambda i, j: (i // tiles_per_group, 0, gate_col * nj + j)))
        args += [res, gate]
    return pl.pallas_call(
        functools.partial(_mm_kernel, n_x=len(xs), gated=gated),
        grid=(m // tm, n // tn),
        in_specs=in_specs,
        out_specs=pl.BlockSpec((tm, tn), lambda i, j: (i, j)),
        out_shape=jax.ShapeDtypeStruct((m, n), F32),
        compiler_params=_cp("arbitrary", "arbitrary"),
        name="mm_gated" if gated else "mm",
    )(*args)


def _swiglu_kernel(x_ref, wg_ref, wu_ref, o_ref):
    x = x_ref[...]
    g = _dot(x, wg_ref[...])
    u = _dot(x, wu_ref[...])
    o_ref[...] = (g * jax.nn.sigmoid(g) * u).astype(o_ref.dtype)


def _mm_swiglu(x, w, tm, tn):
    m, k = x.shape
    nh = w.shape[1] // 2
    nj = nh // tn
    return pl.pallas_call(
        _swiglu_kernel,
        grid=(m // tm, nj),
        in_specs=[pl.BlockSpec((tm, k), lambda i, j: (i, 0)),
                  pl.BlockSpec((k, tn), lambda i, j: (0, j)),
                  pl.BlockSpec((k, tn), lambda i, j: (0, nj + j))],
        out_specs=pl.BlockSpec((tm, tn), lambda i, j: (i, j)),
        out_shape=jax.ShapeDtypeStruct((m, nh), BF16),
        compiler_params=_cp("arbitrary", "arbitrary"),
        name="mm_swiglu",
    )(x, w, w)


def _rwkv_elem(pr, pk, pv, plo, qr, qk, qv, qlo, vec, wup_ref, e):
    row = lambda i: vec[i:i + 1, :]
    xr = pr + (qr - pr) * row(V_MU_R)
    xk = pk + (qk - pk) * row(V_MU_K)
    xv = pv + (qv - pv) * row(V_MU_V)
    xl = plo + (qlo - plo) * vec[V_MU_L:V_MU_L + 1, 0:LANES]
    lw = _dot(jnp.tanh(xl).astype(BF16), wup_ref[0])
    la = _dot(xl.astype(BF16), wup_ref[1])
    g = _dot(jax.nn.sigmoid(xl).astype(BF16), wup_ref[2])
    w_log = -_softplus(-(row(V_W0) + lw)) - 0.5
    log_decay = -jnp.exp(w_log)
    a = jax.nn.sigmoid(row(V_A0) + la)
    kk = xk * row(V_KK)
    kk = kk / jnp.maximum(jnp.sqrt(_dot_exact_rhs(kk * kk, e)), 1e-12)
    kmod = xk * (1.0 + (a - 1.0) * row(V_KA))
    bonus = _dot_exact_rhs(xr * kmod * row(V_RK), e) * xv
    return xr, log_decay, kmod, xv, kk, kk * a, g, bonus


def _rwkv_prep_kernel(pr_ref, pk_ref, pv_ref, pl_ref, sp_ref, vec_ref, wup_ref, e_ref,
                      rt_ref, at_ref, bt_ref, kt_ref, v_ref, bh_ref, kh_ref, gt_ref, g_ref, bonus_ref,
                      carry):
    ti = pl.program_id(1)

    @pl.when(ti == 0)
    def _():
        carry[...] = sp_ref[0]

    tt = pr_ref.shape[0]
    first = lax.broadcasted_iota(jnp.int32, (tt, 1), 0) == 0
    gw = GROUP_WIDTH
    cur = (pr_ref[...], pk_ref[...], pv_ref[...], pl_ref[...])
    offs = (0, gw, 2 * gw, 3 * gw)
    prev = []
    for p, o in zip(cur, offs):
        w = p.shape[1]
        prev.append(jnp.where(first, carry[:, o:o + w], pltpu.roll(p, 1, axis=0)))
        carry[:, o:o + w] = p[tt - 1:tt, :]
    vec = vec_ref[...]
    e = e_ref[...]
    r, lw, kmod, v, kk, b, g, bonus = _rwkv_elem(*cur, *prev, vec, wup_ref, e)

    ri = lax.broadcasted_iota(jnp.int32, (tt, tt), 0)
    ci = lax.broadcasted_iota(jnp.int32, (tt, tt), 1)
    same = (ri // RWKV_CHUNK) == (ci // RWKV_CHUNK)
    tri = (same & (ci <= ri)).astype(BF16)
    blk = same.astype(BF16)
    parts = _split3(lw)
    cum = sum(_dot(tri, p) for p in parts)
    tot = sum(_dot(blk, p) for p in parts)
    e_out = jnp.exp(-cum)
    to_end = jnp.exp(tot - cum)
    outs = ((rt_ref, r * jnp.exp(cum)), (at_ref, -kk * jnp.exp(cum - lw)), (bt_ref, b * e_out),
            (kt_ref, kmod * e_out), (v_ref, v), (bh_ref, b * to_end), (kh_ref, kmod * to_end),
            (gt_ref, jnp.exp(tot)))
    for ref, val in outs:
        for h in range(N_HEADS):
            ref[0, h] = val[:, h * HEAD_DIM:(h + 1) * HEAD_DIM]
    g_ref[0] = g
    bonus_ref[0] = bonus


def _bmm(a, b, ca, cb, passes):
    dn = (((ca,), (cb,)), ((0,), (0,)))
    d = lambda x, y: lax.dot_general(x, y, dn, preferred_element_type=F32)
    if passes == 1:
        return d(a.astype(BF16), b.astype(BF16))
    ah, al = _split2(a)
    bh, bl = _split2(b)
    return d(ah, bh) + d(ah, bl) + d(al, bh)


def _rwkv_scan_kernel(rt_ref, at_ref, bt_ref, kt_ref, v_ref, bh_ref, kh_ref, gt_ref, s0_ref,
                      o_ref, st_ref, s_scr, *, passes):
    c = pl.program_id(1)

    @pl.when(c == 0)
    def _():
        s_scr[...] = s0_ref[0]

    n = RWKV_CHUNK
    rt, at, bt, kt, v, bh, kh = (r[0] for r in (rt_ref, at_ref, bt_ref, kt_ref, v_ref, bh_ref, kh_ref))
    g_end = gt_ref[0][:, 0:1, :]
    shp = (N_HEADS, n, n)
    ti = lax.broadcasted_iota(jnp.int32, shp, 1)
    tj = lax.broadcasted_iota(jnp.int32, shp, 2)
    strict, incl = tj < ti, tj <= ti
    same = (ti // INV_BLOCK) == (tj // INV_BLOCK)
    nn = lambda a, b, p: _bmm(a, b, 2, 1, p)
    nt = lambda a, b, p: _bmm(a, b, 2, 2, p)
    tn = lambda a, b, p: _bmm(a, b, 1, 1, p)
    p_lo, p_hi = passes

    big = nt(jnp.concatenate([at, rt], axis=1), jnp.concatenate([bt, kt], axis=1), p_hi)
    a_ab = jnp.where(strict, big[:, :n, :n], 0.0)
    a_ak = jnp.where(strict, big[:, :n, n:], 0.0)
    a_rb = jnp.where(incl, big[:, n:, :n], 0.0)
    a_rk = jnp.where(incl, big[:, n:, n:], 0.0)
    av = nn(jnp.concatenate([a_ak, a_rk], axis=1), v, p_lo)

    nd = jnp.where(same, a_ab, 0.0)
    no = jnp.where(same, 0.0, a_ab)
    nd2 = nn(nd, nd, p_hi)
    nd4 = nn(nd2, nd2, p_hi)
    nd8 = nn(nd4, nd4, p_hi)
    y = jnp.concatenate([no, at, av[:, :n]], axis=2)
    for pw in (nd8, nd4, nd2, nd):
        y = y + nn(pw, y, p_hi)
    m = y[:, :, :n]
    z = y[:, :, n:]
    m2 = nn(m, m, p_hi)
    z = z + nn(m2, z, p_hi)
    z = z + nn(m, z, p_hi)
    qo = jnp.concatenate([rt, av[:, n:]], axis=2) + nn(a_rb, z, p_lo)
    pg = tn(z, bh, p_hi)
    gm = pg[:, n:] + tn(v, kh, p_lo)

    s = s_scr[...]
    o_ref[0] = nt(qo[:, :, :HEAD_DIM], s, p_hi) + qo[:, :, HEAD_DIM:]
    s_new = s * g_end + nn(s, pg[:, :n], p_hi) + gm
    s_scr[...] = s_new

    @pl.when(c == pl.num_programs(1) - 1)
    def _():
        st_ref[0] = s_new


def _rwkv_post_kernel(o_ref, g_ref, bonus_ref, vec_ref, e_ref, ya_ref):
    o = jnp.concatenate([o_ref[0, h] for h in range(N_HEADS)], axis=-1)
    e = e_ref[...]
    vec = vec_ref[...]
    mu = _dot_exact_rhs(o, e) * (1.0 / HEAD_DIM)
    d = o - mu
    var = _dot_exact_rhs(d * d, e) * (1.0 / HEAD_DIM)
    on = d * lax.rsqrt(var + GN_EPS) * vec[V_LNG:V_LNG + 1, :] + vec[V_LNB:V_LNB + 1, :]
    ya_ref[0] = ((on + bonus_ref[0]) * g_ref[0]).astype(ya_ref.dtype)


def _rwkv_prompt(proj, nb, t, shift_prev, s0, vec, wup, e, passes):
    tt = 256
    nt_ = t // tt
    gw, nh, hd = GROUP_WIDTH, N_HEADS, HEAD_DIM
    pspec = lambda cb: pl.BlockSpec((tt, gw), lambda b, i: (b * nt_ + i, cb))
    hm_spec = pl.BlockSpec((1, nh, tt, hd), lambda b, i: (b, 0, i, 0))
    ld_spec = pl.BlockSpec((1, tt, gw), lambda b, i: (b, i, 0))
    const = lambda shape: pl.BlockSpec(shape, lambda b, i: (0,) * len(shape))
    hm = jax.ShapeDtypeStruct((nb, nh, t, hd), F32)
    ld = jax.ShapeDtypeStruct((nb, t, gw), F32)
    outs = pl.pallas_call(
        _rwkv_prep_kernel,
        grid=(nb, nt_),
        in_specs=[pspec(CB_R), pspec(CB_RK), pspec(CB_RV),
                  pl.BlockSpec((tt, LANES), lambda b, i: (b * nt_ + i, CB_L * (gw // LANES))),
                  pl.BlockSpec((1, 1, 3 * gw + LANES), lambda b, i: (b, 0, 0)),
                  const(vec.shape), const(wup.shape), const(e.shape)],
        out_specs=[hm_spec] * 8 + [ld_spec] * 2,
        out_shape=[hm] * 8 + [ld] * 2,
        scratch_shapes=[pltpu.VMEM((1, 3 * gw + LANES), F32)],
        compiler_params=_cp("arbitrary", "arbitrary"),
        name="rwkv_prep",
    )(proj, proj, proj, proj, shift_prev, vec, wup, e)
    chunk_in, g, bonus = outs[:8], outs[8], outs[9]

    n = RWKV_CHUNK
    cspec = pl.BlockSpec((1, nh, n, hd), lambda b, c: (b, 0, c, 0))
    sspec = pl.BlockSpec((1, nh, hd, hd), lambda b, c: (b, 0, 0, 0))
    o, s_t = pl.pallas_call(
        functools.partial(_rwkv_scan_kernel, passes=passes),
        grid=(nb, t // n),
        in_specs=[cspec] * 8 + [sspec],
        out_specs=[cspec, sspec],
        out_shape=[hm, jax.ShapeDtypeStruct((nb, nh, hd, hd), F32)],
        scratch_shapes=[pltpu.VMEM((nh, hd, hd), F32)],
        compiler_params=_cp("arbitrary", "arbitrary"),
        name="rwkv_scan",
    )(*chunk_in, s0)

    ya = pl.pallas_call(
        _rwkv_post_kernel,
        grid=(nb, nt_),
        in_specs=[hm_spec, ld_spec, ld_spec, const(vec.shape), const(e.shape)],
        out_specs=ld_spec,
        out_shape=jax.ShapeDtypeStruct((nb, t, gw), BF16),
        compiler_params=_cp("arbitrary", "arbitrary"),
        name="rwkv_post",
    )(o, g, bonus, vec, e)
    return ya, s_t


def _alibi_slope(h):
    return 2.0 ** (-8.0 * (h + 1) / N_HEADS)


def _moba_prompt_kernel(q_ref, k_ref, v_ref, eh_ref, ex_ref, o_ref, kh_scr, vh_scr, km_scr, *, n_blocks):
    qi = pl.program_id(1)
    blk, nh, hd = MOBA_BLOCK, N_HEADS, HEAD_DIM
    t = n_blocks * blk

    @pl.when(qi == 0)
    def _():
        k = k_ref[...]
        v = v_ref[...]
        for h in range(nh):
            kh_scr[h] = k[:, h * hd:(h + 1) * hd].astype(BF16)
            vh_scr[h] = v[:, h * hd:(h + 1) * hd].astype(BF16)
        for n in range(n_blocks):
            km_scr[n:n + 1, :] = jnp.mean(k[n * blk:(n + 1) * blk, :], axis=0, keepdims=True)

    own = qi
    qs = q_ref[...] * (hd ** -0.5)
    eh = eh_ref[...]
    gates = []
    for n in range(n_blocks):
        x = qs * km_scr[n:n + 1, :]
        gates.append(sum(_dot(p, eh) for p in _split3(x)))
    n_sel = min(MOBA_TOPK, n_blocks - 1)
    sels = []
    for n in range(n_blocks):
        rank = jnp.zeros_like(gates[n])
        for m in range(n_blocks):
            if m == n:
                continue
            beats = (gates[m] >= gates[n]) if m < n else (gates[m] > gates[n])
            rank = rank + jnp.where(beats & (m < own), 1.0, 0.0)
        sel = jnp.where((rank < n_sel) & (n < own), 1.0, 0.0)
        sels.append(sel[:, 0:nh])
    selcat = jnp.concatenate(sels, axis=1).astype(BF16)

    row = lax.broadcasted_iota(jnp.int32, (blk, t), 0)
    col = lax.broadcasted_iota(jnp.int32, (blk, t), 1)
    dist_i = own * blk + row - col
    dist = dist_i.astype(F32)
    cblk = col // blk
    own_ok = (cblk == own) & (dist_i >= 0)
    past = cblk < own
    outs = []
    for h in range(nh):
        qh = qs[:, h * hd:(h + 1) * hd].astype(BF16)
        s = lax.dot_general(qh, kh_scr[h], (((1,), (1,)), ((), ())), preferred_element_type=F32)
        picked = _dot(selcat, ex_ref[h]) > 0.5
        ok = own_ok | (past & picked)
        logit = jnp.where(ok, s - _alibi_slope(h) * dist, -jnp.inf)
        p = jnp.exp(logit - jnp.max(logit, axis=-1, keepdims=True))
        l = jnp.sum(p, axis=-1, keepdims=True)
        outs.append(_dot(p.astype(BF16), vh_scr[h]) / l)
    o_ref[...] = jnp.concatenate(outs, axis=-1).astype(o_ref.dtype)


def _moba_prompt(proj, nb, t):
    blk, gw, nh, hd = MOBA_BLOCK, GROUP_WIDTH, N_HEADS, HEAD_DIM
    n_blocks = t // blk
    nsel_cols = n_blocks * nh
    eh = (jnp.arange(gw)[:, None] // hd == jnp.arange(LANES)[None, :]).astype(BF16)
    lane = jnp.arange(nsel_cols)
    ex = ((lane[None, :, None] % nh == jnp.arange(nh)[:, None, None])
          & (lane[None, :, None] // nh == (jnp.arange(t) // blk)[None, None, :])).astype(BF16)
    return pl.pallas_call(
        functools.partial(_moba_prompt_kernel, n_blocks=n_blocks),
        grid=(nb, n_blocks),
        in_specs=[pl.BlockSpec((blk, gw), lambda b, i: (b * n_blocks + i, CB_Q)),
                  pl.BlockSpec((t, gw), lambda b, i: (b, CB_K)),
                  pl.BlockSpec((t, gw), lambda b, i: (b, CB_V)),
                  pl.BlockSpec((gw, LANES), lambda b, i: (0, 0)),
                  pl.BlockSpec((nh, nsel_cols, t), lambda b, i: (0, 0, 0))],
        out_specs=pl.BlockSpec((blk, gw), lambda b, i: (b * n_blocks + i, 0)),
        out_shape=jax.ShapeDtypeStruct((nb * t, gw), BF16),
        scratch_shapes=[pltpu.VMEM((nh, t, hd), BF16), pltpu.VMEM((nh, t, hd), BF16),
                        pltpu.VMEM((n_blocks, gw), F32)],
        compiler_params=_cp("arbitrary", "arbitrary"),
        name="moba_prompt",
    )(proj, proj, proj, eh, ex)


def _sgu_prompt_kernel(pu_ref, pv_ref, w_ref, bias_ref, ln_ref, o_ref):
    u = jax.nn.gelu(pu_ref[...])
    v = _layer_norm(jax.nn.gelu(pv_ref[...]), ln_ref[0:1, :], ln_ref[1:2, :]).astype(BF16)
    cl, hd = SG_CHUNK, HEAD_DIM
    ri = lax.broadcasted_iota(jnp.int32, (cl, cl), 0)
    ci = lax.broadcasted_iota(jnp.int32, (cl, cl), 1)
    ws = [jnp.where(ci <= ri, w_ref[h], 0.0).astype(BF16) for h in range(N_HEADS)]
    bias = bias_ref[...]
    for c in range(u.shape[0] // cl):
        vc = v[c * cl:(c + 1) * cl, :]
        mixed = jnp.concatenate([_dot(ws[h], vc[:, h * hd:(h + 1) * hd]) for h in range(N_HEADS)], axis=-1)
        o_ref[c * cl:(c + 1) * cl, :] = (u[c * cl:(c + 1) * cl, :] * (mixed + bias)).astype(o_ref.dtype)


def _sgu_prompt(proj, nrows, w, bias_exp, ln):
    tt, gw = 256, GROUP_WIDTH
    return pl.pallas_call(
        _sgu_prompt_kernel,
        grid=(nrows // tt,),
        in_specs=[pl.BlockSpec((tt, gw), lambda i: (i, CB_SU)),
                  pl.BlockSpec((tt, gw), lambda i: (i, CB_SV)),
                  pl.BlockSpec(w.shape, lambda i: (0, 0, 0)),
                  pl.BlockSpec(bias_exp.shape, lambda i: (0, 0)),
                  pl.BlockSpec(ln.shape, lambda i: (0, 0))],
        out_specs=pl.BlockSpec((tt, gw), lambda i: (i, 0)),
        out_shape=jax.ShapeDtypeStruct((nrows, gw), BF16),
        compiler_params=_cp("arbitrary"),
        name="sgu_prompt",
    )(proj, proj, w, bias_exp, ln)


def _conv_prompt_kernel(pa_ref, pg_ref, prev_ref, w_ref, vec_ref, o_ref, hist_ref, zf):
    ti = pl.program_id(1)
    tt = pa_ref.shape[0]
    halo = CONV_HALO

    @pl.when(ti == 0)
    def _():
        zf[0:halo, :] = prev_ref[0]

    z = pa_ref[...] * jax.nn.sigmoid(pg_ref[...])
    zf[halo:halo + tt, :] = z
    lead = halo - (CONV_WIDTH - 1)
    acc = jnp.zeros_like(z)
    for i in range(CONV_WIDTH):
        acc = acc + w_ref[i:i + 1, :] * zf[pl.ds(lead + i, tt), :]
    y = _layer_norm(acc + vec_ref[0:1, :], vec_ref[1:2, :], vec_ref[2:3, :])
    o_ref[...] = (y * jax.nn.sigmoid(y)).astype(o_ref.dtype)
    tail = zf[pl.ds(tt, halo), :]
    hist_ref[0] = tail
    zf[0:halo, :] = tail


def _conv_prompt(proj, nb, t, prev, w, vec):
    tt, gw = 256, GROUP_WIDTH
    nt_ = t // tt
    return pl.pallas_call(
        _conv_prompt_kernel,
        grid=(nb, nt_),
        in_specs=[pl.BlockSpec((tt, gw), lambda b, i: (b * nt_ + i, CB_CA)),
                  pl.BlockSpec((tt, gw), lambda b, i: (b * nt_ + i, CB_CG)),
                  pl.BlockSpec((1, CONV_HALO, gw), lambda b, i: (b, 0, 0)),
                  pl.BlockSpec(w.shape, lambda b, i: (0, 0)),
                  pl.BlockSpec(vec.shape, lambda b, i: (0, 0))],
        out_specs=[pl.BlockSpec((tt, gw), lambda b, i: (b * nt_ + i, 0)),
                   pl.BlockSpec((1, CONV_HALO, gw), lambda b, i: (b, 0, 0))],
        out_shape=[jax.ShapeDtypeStruct((nb * t, gw), BF16),
                   jax.ShapeDtypeStruct((nb, CONV_HALO, gw), F32)],
        scratch_shapes=[pltpu.VMEM((CONV_HALO + tt, gw), F32)],
        compiler_params=_cp("arbitrary", "arbitrary"),
        name="conv_prompt",
    )(proj, proj, prev, w, vec)


def _sample_elem_kernel(pr_ref, pk_ref, pv_ref, pl_ref, sp_ref, vec_ref, wup_ref, e_ref,
                        su_ref, sv_ref, sgp_ref, ca_ref, cg_ref, cprev_ref, cw_ref, cvec_ref,
                        r_ref, w_ref, k_ref, v_ref, kk_ref, b_ref, g_ref, bonus_ref,
                        yc_ref, sgv_ref, yd_ref, cnew_ref):
    gw = GROUP_WIDTH
    sp = sp_ref[...]
    r, lw, kmod, v, kk, b, g, bonus = _rwkv_elem(
        pr_ref[...], pk_ref[...], pv_ref[...], pl_ref[...],
        sp[:, 0:gw], sp[:, gw:2 * gw], sp[:, 2 * gw:3 * gw], sp[:, 3 * gw:3 * gw + LANES],
        vec_ref[...], wup_ref, e_ref[...])
    for ref, val in ((r_ref, r), (w_ref, jnp.exp(lw)), (k_ref, kmod), (v_ref, v), (kk_ref, kk),
                     (b_ref, b), (g_ref, g), (bonus_ref, bonus)):
        ref[...] = val
    u = jax.nn.gelu(su_ref[...])
    sv = _layer_norm(jax.nn.gelu(sv_ref[...]), sgp_ref[0:1, :], sgp_ref[1:2, :])
    sgv_ref[...] = sv
    yc_ref[...] = (u * (sgp_ref[2:3, :] * sv + sgp_ref[3:4, :])).astype(yc_ref.dtype)
    z = ca_ref[...] * jax.nn.sigmoid(cg_ref[...])
    nprev = CONV_WIDTH - 1
    acc = cw_ref[nprev:nprev + 1, :] * z
    for i in range(nprev):
        acc = acc + cw_ref[i:i + 1, :] * cprev_ref[i]
        if i > 0:
            cnew_ref[i - 1] = cprev_ref[i]
    cnew_ref[nprev - 1] = z
    y = _layer_norm(acc + cvec_ref[0:1, :], cvec_ref[1:2, :], cvec_ref[2:3, :])
    yd_ref[...] = (y * jax.nn.sigmoid(y)).astype(yd_ref.dtype)


def _sample_wkv_kernel(r_ref, w_ref, k_ref, v_ref, kk_ref, b_ref, g_ref, bonus_ref, ln_ref, s_ref,
                       ya_ref, st_ref):
    hd = HEAD_DIM
    r, w, k, v, kk, b, g, bonus = (x[0] for x in (r_ref, w_ref, k_ref, v_ref, kk_ref, b_ref, g_ref, bonus_ref))
    s = s_ref[0]
    eye = (lax.broadcasted_iota(jnp.int32, (hd, hd), 0) == lax.broadcasted_iota(jnp.int32, (hd, hd), 1))
    eye = eye.astype(F32)[None]
    v_col = jnp.sum(eye * v, axis=2, keepdims=True)
    sa = jnp.sum(s * (-kk), axis=2, keepdims=True)
    s_new = s * w + sa * b + v_col * k
    st_ref[0] = s_new
    o_col = jnp.sum(s_new * r, axis=2, keepdims=True)
    o = jnp.sum(o_col * eye, axis=1, keepdims=True)
    mu = jnp.mean(o, axis=-1, keepdims=True)
    d = o - mu
    var = jnp.mean(d * d, axis=-1, keepdims=True)
    on = d * lax.rsqrt(var + GN_EPS) * ln_ref[0] + ln_ref[1]
    ya_ref[0] = (on + bonus) * g


def _sample_mixers(proj, shift_prev, s0, conv_prev_t, vec, wup, e, sgp, cw, cvec):
    nb = proj.shape[0]
    gw, nh, hd = GROUP_WIDTH, N_HEADS, HEAD_DIM
    col = lambda cb: pl.BlockSpec((nb, gw), lambda i: (0, cb))
    full = lambda x: pl.BlockSpec(x.shape, lambda i: (0,) * x.ndim)
    row_f32 = jax.ShapeDtypeStruct((nb, gw), F32)
    outs = pl.pallas_call(
        _sample_elem_kernel,
        grid=(1,),
        in_specs=[col(CB_R), col(CB_RK), col(CB_RV),
                  pl.BlockSpec((nb, LANES), lambda i: (0, CB_L * (gw // LANES))),
                  full(shift_prev), full(vec), full(wup), full(e),
                  col(CB_SU), col(CB_SV), full(sgp), col(CB_CA), col(CB_CG),
                  full(conv_prev_t), full(cw), full(cvec)],
        out_specs=[pl.BlockSpec((nb, gw), lambda i: (0, 0))] * 11 + [full(conv_prev_t)],
        out_shape=[row_f32] * 8 + [jax.ShapeDtypeStruct((nb, gw), BF16), row_f32,
                                   jax.ShapeDtypeStruct((nb, gw), BF16),
                                   jax.ShapeDtypeStruct(conv_prev_t.shape, F32)],
        compiler_params=_cp("arbitrary"),
        name="sample_elem",
    )(proj, proj, proj, proj, shift_prev, vec, wup, e, proj, proj, sgp, proj, proj, conv_prev_t, cw, cvec)
    heads = [x.reshape(nb, nh, 1, hd) for x in outs[:8]]
    yc, sgv, yd, conv_new_t = outs[8:]
    ln = jnp.stack([vec[V_LNG].reshape(nh, 1, hd), vec[V_LNB].reshape(nh, 1, hd)])
    hspec = pl.BlockSpec((1, nh, 1, hd), lambda b: (b, 0, 0, 0))
    sspec = pl.BlockSpec((1, nh, hd, hd), lambda b: (b, 0, 0, 0))
    ya, s_new = pl.pallas_call(
        _sample_wkv_kernel,
        grid=(nb,),
        in_specs=[hspec] * 8 + [pl.BlockSpec(ln.shape, lambda b: (0, 0, 0, 0)), sspec],
        out_specs=[hspec, sspec],
        out_shape=[jax.ShapeDtypeStruct((nb, nh, 1, hd), F32), jax.ShapeDtypeStruct(s0.shape, F32)],
        compiler_params=_cp("arbitrary"),
        name="sample_wkv",
    )(*heads, ln, s0)
    return ya.reshape(nb, gw).astype(BF16), s_new, yc, sgv, yd, conv_new_t


PAGES_PER_STEP = 16


def _moba_scores_kernel(pt_ref, q_ref, eht_ref, *refs):
    pages, o_ref = refs[:-1], refs[-1]
    ps = pages[0].shape[2]
    q = q_ref[0] * (HEAD_DIM ** -0.5)
    eht = eht_ref[...]
    dn = (((1,), (1,)), ((), ()))
    for i, p_ref in enumerate(pages):
        hi, lo = _split2(p_ref[0, 0] * q)
        st = (lax.dot_general(eht, hi, dn, preferred_element_type=F32)
              + lax.dot_general(eht, lo, dn, preferred_element_type=F32))
        o_ref[0, :, i * ps:(i + 1) * ps] = st


def _moba_scores(q3, cache, layer, page_table):
    nb, n_pages = page_table.shape
    ps, gw, nh = cache.shape[2], GROUP_WIDTH, N_HEADS
    pp = min(PAGES_PER_STEP, n_pages)
    eht = (jnp.arange(nh)[:, None] == jnp.arange(gw)[None, :] // HEAD_DIM).astype(BF16)

    def page_spec(i):
        return pl.BlockSpec((1, 1, ps, gw), lambda b, j, pt: (layer, pt[b, j * pp + i], 0, 0))

    return pl.pallas_call(
        _moba_scores_kernel,
        grid_spec=pltpu.PrefetchScalarGridSpec(
            num_scalar_prefetch=1,
            grid=(nb, n_pages // pp),
            in_specs=[pl.BlockSpec((1, 1, gw), lambda b, j, pt: (b, 0, 0)),
                      pl.BlockSpec((nh, gw), lambda b, j, pt: (0, 0))]
                     + [page_spec(i) for i in range(pp)],
            out_specs=pl.BlockSpec((1, nh, pp * ps), lambda b, j, pt: (b, 0, j)),
        ),
        out_shape=jax.ShapeDtypeStruct((nb, nh, n_pages * ps), F32),
        compiler_params=_cp("arbitrary", "arbitrary"),
        name="moba_scores",
    )(page_table, q3, eht, *([cache] * pp))


def _moba_select_kernel(s_ref, q_ref, kn_ref, eht_ref, bi_ref, slope_ref, p_ref, p0_ref, *, n_past_blocks):
    s = s_ref[0]
    nh, length = s.shape
    eht = eht_ref[...].astype(F32)
    s0 = jnp.sum(eht * (q_ref[0] * (HEAD_DIM ** -0.5) * kn_ref[0]), axis=1, keepdims=True)
    bi = bi_ref[...]
    gate = sum(_dot(p, bi) for p in _split3(s)) * (1.0 / MOBA_BLOCK)
    lane = lax.broadcasted_iota(jnp.int32, gate.shape, 1)
    gate = jnp.where(lane < n_past_blocks, gate, -jnp.inf)
    sel = jnp.zeros(gate.shape, jnp.bool_)
    for _ in range(min(MOBA_TOPK, n_past_blocks)):
        m = jnp.max(gate, axis=1, keepdims=True)
        idx = jnp.min(jnp.where(gate == m, lane, LANES), axis=1, keepdims=True)
        pick = lane == idx
        sel = sel | pick
        gate = jnp.where(pick, -jnp.inf, gate)
    picked = lax.dot_general(sel.astype(BF16), bi, (((1,), (1,)), ((), ())),
                             preferred_element_type=F32) > 0.5
    pos = lax.broadcasted_iota(jnp.int32, s.shape, 1)
    dist = (length - pos).astype(F32)
    logit = jnp.where(picked, s - slope_ref[:, 0:1] * dist, -jnp.inf)
    m = jnp.maximum(jnp.max(logit, axis=1, keepdims=True), s0)
    p = jnp.exp(logit - m)
    p0 = jnp.exp(s0 - m)
    inv = 1.0 / (jnp.sum(p, axis=1, keepdims=True) + p0)
    p_ref[0] = p * inv
    p0_ref[0] = jnp.broadcast_to(p0 * inv, (nh, LANES))


def _moba_pv_kernel(pt_ref, p_ref, p0_ref, vn_ref, eht_ref, *refs, n_steps):
    pages, o_ref, acc = refs[:-2], refs[-2], refs[-1]
    j = pl.program_id(1)
    ps = pages[0].shape[2]

    @pl.when(j == 0)
    def _():
        acc[...] = jnp.zeros_like(acc)

    tot = acc[...]
    for i, v_ref in enumerate(pages):
        tot = tot + _dot(p_ref[0, :, i * ps:(i + 1) * ps].astype(BF16), v_ref[0, 0].astype(BF16))
    acc[...] = tot

    @pl.when(j == n_steps - 1)
    def _():
        eht = eht_ref[...].astype(F32)
        own = p0_ref[0][:, 0:1] * vn_ref[0]
        o_ref[0] = jnp.sum(eht * (tot + own), axis=0, keepdims=True)


def _moba_sample(proj3, cache_k, cache_v, layer, page_table):
    nb, n_pages = page_table.shape
    ps, gw, nh = cache_k.shape[2], GROUP_WIDTH, N_HEADS
    length = n_pages * ps
    n_past_blocks = length // MOBA_BLOCK
    pp = min(PAGES_PER_STEP, n_pages)
    n_steps = n_pages // pp
    eht = (jnp.arange(nh)[:, None] == jnp.arange(gw)[None, :] // HEAD_DIM).astype(BF16)
    q3, kn3, vn3 = (proj3[:, :, cb * gw:(cb + 1) * gw] for cb in (CB_Q, CB_K, CB_V))
    scores = _moba_scores(q3, cache_k, layer, page_table)
    bi = (jnp.arange(length)[:, None] // MOBA_BLOCK == jnp.arange(LANES)[None, :]).astype(BF16)
    slopes = jnp.broadcast_to(
        jnp.array([_alibi_slope(h) for h in range(nh)], F32)[:, None], (nh, LANES))
    row3 = pl.BlockSpec((1, 1, gw), lambda b: (b, 0, 0))
    p, p0 = pl.pallas_call(
        functools.partial(_moba_select_kernel, n_past_blocks=n_past_blocks),
        grid=(nb,),
        in_specs=[pl.BlockSpec((1, nh, length), lambda b: (b, 0, 0)), row3, row3,
                  pl.BlockSpec((nh, gw), lambda b: (0, 0)),
                  pl.BlockSpec((length, LANES), lambda b: (0, 0)),
                  pl.BlockSpec((nh, LANES), lambda b: (0, 0))],
        out_specs=[pl.BlockSpec((1, nh, length), lambda b: (b, 0, 0)),
                   pl.BlockSpec((1, nh, LANES), lambda b: (b, 0, 0))],
        out_shape=[jax.ShapeDtypeStruct((nb, nh, length), F32),
                   jax.ShapeDtypeStruct((nb, nh, LANES), F32)],
        compiler_params=_cp("arbitrary"),
        name="moba_select",
    )(scores, q3, kn3, eht, bi, slopes)

    def page_spec(i):
        return pl.BlockSpec((1, 1, ps, gw), lambda b, j, pt: (layer, pt[b, j * pp + i], 0, 0))

    out = pl.pallas_call(
        functools.partial(_moba_pv_kernel, n_steps=n_steps),
        grid_spec=pltpu.PrefetchScalarGridSpec(
            num_scalar_prefetch=1,
            grid=(nb, n_steps),
            in_specs=[pl.BlockSpec((1, nh, pp * ps), lambda b, j, pt: (b, 0, j)),
                      pl.BlockSpec((1, nh, LANES), lambda b, j, pt: (b, 0, 0)),
                      pl.BlockSpec((1, 1, gw), lambda b, j, pt: (b, 0, 0)),
                      pl.BlockSpec((nh, gw), lambda b, j, pt: (0, 0))]
                     + [page_spec(i) for i in range(pp)],
            out_specs=pl.BlockSpec((1, 1, gw), lambda b, j, pt: (b, 0, 0)),
            scratch_shapes=[pltpu.VMEM((nh, gw), F32)],
        ),
        out_shape=jax.ShapeDtypeStruct((nb, 1, gw), F32),
        compiler_params=_cp("arbitrary", "arbitrary"),
        name="moba_pv",
    )(page_table, p, p0, vn3, eht, *([cache_v] * pp))
    return out.reshape(nb, gw)


RWKV_PASSES = (3, 3)


def _shift_layout(shift):
    n = shift.shape[0]
    pad = jnp.zeros((n, LANES - 3 * TM_LORA), F32)
    return jnp.concatenate([shift, pad], axis=1)


def _layer_params(l, w_in, w_out, w_ffn_in, w_ffn_out, tm_mu, tm_w0, tm_w_up, tm_a0, tm_a_up, tm_g_up,
                  tm_k_k, tm_k_a, tm_r_k, tm_ln_g, tm_ln_b, sg_ws, sg_bs, sg_ln_g, sg_ln_b,
                  cv_w, cv_b, cv_ln_g, cv_ln_b):
    gw, lr, hd = GROUP_WIDTH, TM_LORA, HEAD_DIM
    wi = w_in[l]
    w_in_p = jnp.concatenate(
        [wi[:, A_PROJ:], wi[:, :A_PROJ], jnp.zeros((wi.shape[0], IN_PAD - IN_WIDTH), F32)], axis=1).astype(BF16)
    mu = tm_mu[l]
    mu_l = jnp.concatenate([mu[3 * gw:], jnp.zeros((gw - 3 * lr,), F32)])
    vec = jnp.stack([mu[0:gw], mu[gw:2 * gw], mu[2 * gw:3 * gw], tm_w0[l], tm_a0[l], tm_k_k[l], tm_k_a[l],
                     tm_r_k[l].reshape(gw), tm_ln_g[l], tm_ln_b[l], mu_l]
                    + [jnp.zeros((gw,), F32)] * 5)
    wup = jnp.zeros((3, LANES, gw), F32)
    wup = wup.at[0, 0:lr].set(tm_w_up[l]).at[1, lr:2 * lr].set(tm_a_up[l]).at[2, 2 * lr:3 * lr].set(tm_g_up[l])
    sg_bias = jnp.repeat(sg_bs[l].T, hd, axis=1)
    sg_ln = jnp.stack([sg_ln_g[l], sg_ln_b[l]] + [jnp.zeros((gw,), F32)] * 6)
    sgp = jnp.stack([sg_ln_g[l], sg_ln_b[l], jnp.repeat(sg_ws[l][:, 0, 0], hd), jnp.repeat(sg_bs[l][:, 0], hd)]
                    + [jnp.zeros((gw,), F32)] * 4)
    cw = jnp.concatenate([cv_w[l], jnp.zeros((1, gw), F32)], axis=0)
    cvec = jnp.stack([cv_b[l], cv_ln_g[l], cv_ln_b[l]] + [jnp.zeros((gw,), F32)] * 5)
    return dict(w_in=w_in_p, w_out=w_out[l].astype(BF16), w_ffn_in=w_ffn_in[l].astype(BF16),
                w_ffn_out=w_ffn_out[l].astype(BF16), vec=vec, wup=wup.astype(BF16), sg_w=sg_ws[l],
                sg_bias=sg_bias, sg_ln=sg_ln, sgp=sgp, cw=cw, cvec=cvec)


def _dense_tail(x, ys, mod, p, g_ffn, tm, rows_per_group):
    ng, t, d = x.shape
    m = ng * t
    x1 = _mm(ys, p['w_out'], tm, min(1024, d), res=x.reshape(m, d), gate=mod, gate_col=2,
             rows_per_group=rows_per_group)
    h2 = _norm_mod(x1.reshape(ng, t, d), g_ffn, mod, 3, 4, min(512, t)).reshape(m, d)
    hid = _mm_swiglu(h2, p['w_ffn_in'], tm, 512)
    x2 = _mm([hid], p['w_ffn_out'], min(tm, 512), 512, res=x1, gate=mod, gate_col=5,
             rows_per_group=rows_per_group)
    return x2.reshape(ng, t, d)


def _prompt_layer(x, mod, p, g_mix, g_ffn, e):
    nb, t, d = x.shape
    m = nb * t
    gw, nh, hd = GROUP_WIDTH, N_HEADS, HEAD_DIM
    h = _norm_mod(x, g_mix, mod, 0, 1, 512).reshape(m, d)
    proj = _mm([h], p['w_in'], 1024, 512)
    shift0 = jnp.zeros((nb, 1, 3 * gw + LANES), F32)
    s0 = jnp.zeros((nb, nh, hd, hd), F32)
    ya, wkv = _rwkv_prompt(proj, nb, t, shift0, s0, p['vec'], p['wup'], e, RWKV_PASSES)
    yb = _moba_prompt(proj, nb, t)
    yc = _sgu_prompt(proj, m, p['sg_w'], p['sg_bias'], p['sg_ln'])
    yd, hist = _conv_prompt(proj, nb, t, jnp.zeros((nb, CONV_HALO, gw), F32), p['cw'], p['cvec'])
    x2 = _dense_tail(x, [ya.reshape(m, gw), yb, yc, yd], mod, p, g_ffn, 1024, t)
    proj3 = proj.reshape(nb, t, IN_PAD)
    last = proj3[:, t - 1]
    shift = jnp.concatenate([last[:, CB_R * gw:CB_L * gw], last[:, CB_L * gw:CB_L * gw + 3 * TM_LORA]], axis=1)
    new_k = proj3[:, :, CB_K * gw:(CB_K + 1) * gw].reshape(nb, t, nh, hd)
    new_v = proj3[:, :, CB_V * gw:(CB_V + 1) * gw].reshape(nb, t, nh, hd)
    return x2, (shift, wkv, new_k, new_v, hist[:, CONV_HALO - (CONV_WIDTH - 1):])


def _sample_layer(x, mod, p, g_mix, g_ffn, e, layer, shift_prev, wkv_prev, conv_prev, cache_k, cache_v,
                  page_table):
    nb, t, d = x.shape
    gw, nh, hd = GROUP_WIDTH, N_HEADS, HEAD_DIM
    xg = x.reshape(1, nb, d)
    h = _norm_mod(xg, g_mix, mod, 0, 1, nb).reshape(nb, d)
    proj = _mm([h], p['w_in'], nb, 512)
    ya, wkv, yc, sgv, yd, conv_new_t = _sample_mixers(
        proj, _shift_layout(shift_prev), wkv_prev, conv_prev.transpose(1, 0, 2),
        p['vec'], p['wup'], e, p['sgp'], p['cw'], p['cvec'])
    yb = _moba_sample(proj.reshape(nb, 1, IN_PAD), cache_k, cache_v, layer, page_table).astype(BF16)
    x2 = _dense_tail(xg, [ya, yb, yc, yd], mod, p, g_ffn, nb, nb)
    shift = jnp.concatenate([proj[:, CB_R * gw:CB_L * gw], proj[:, CB_L * gw:CB_L * gw + 3 * TM_LORA]], axis=1)
    new_k = proj[:, CB_K * gw:(CB_K + 1) * gw].reshape(nb, 1, nh, hd)
    new_v = proj[:, CB_V * gw:(CB_V + 1) * gw].reshape(nb, 1, nh, hd)
    return x2.reshape(nb, t, d), (shift, wkv, new_k, new_v, sgv.reshape(nb, 1, gw), conv_new_t.transpose(1, 0, 2))


def kernel(x_prompt, x_sample, c_prompt, c_sample, cache_k_b, cache_v_b, state_a_wkv, state_a_shift, state_d_conv, page_table, w_ada, b_ada, g_norm_mix, w_in, w_out, tm_mu, tm_w0, tm_w_up, tm_a0, tm_a_up, tm_g_up, tm_k_k, tm_k_a, tm_r_k, tm_ln_g, tm_ln_b, sg_ws, sg_bs, sg_ln_g, sg_ln_b, cv_w, cv_b, cv_ln_g, cv_ln_b, g_norm_ffn, w_ffn_in, w_ffn_out, g_norm_final):
    depth = w_in.shape[0]
    bp, ns = x_prompt.shape[0], x_sample.shape[0]
    gw = GROUP_WIDTH
    n_pool, ps = cache_k_b.shape[1], cache_k_b.shape[2]
    cache_k = cache_k_b.reshape(depth, n_pool, ps, gw)
    cache_v = cache_v_b.reshape(depth, n_pool, ps, gw)
    rows = bp + ns
    rows_pad = -(-rows // 8) * 8
    c_all = jnp.concatenate([c_prompt, c_sample, jnp.zeros((rows_pad - rows, c_prompt.shape[1]), F32)], axis=0)
    mod = _ada(c_all, w_ada, b_ada)
    e = _head_ones()
    xp, xs = x_prompt, x_sample
    st_p, st_s = [], []
    for l in range(depth):
        p = _layer_params(l, w_in, w_out, w_ffn_in, w_ffn_out, tm_mu, tm_w0, tm_w_up, tm_a0, tm_a_up, tm_g_up,
                          tm_k_k, tm_k_a, tm_r_k, tm_ln_g, tm_ln_b, sg_ws, sg_bs, sg_ln_g, sg_ln_b,
                          cv_w, cv_b, cv_ln_g, cv_ln_b)
        mod_p = mod[l, :bp][:, None, :]
        mod_s = mod[l, bp:rows][None]
        xp, sp = _prompt_layer(xp, mod_p, p, g_norm_mix[l], g_norm_ffn[l], e)
        xs, ss = _sample_layer(xs, mod_s, p, g_norm_mix[l], g_norm_ffn[l], e, l, state_a_shift[l],
                               state_a_wkv[l], state_d_conv[l], cache_k, cache_v, page_table)
        st_p.append(sp)
        st_s.append(ss)
    y_prompt = _norm(xp, g_norm_final, 512)
    y_sample = _norm(xs.reshape(1, ns, -1), g_norm_final, ns).reshape(xs.shape)
    stack = lambda sts, i: jnp.stack([s[i] for s in sts])
    return (y_prompt, y_sample, stack(st_p, 2), stack(st_p, 3), stack(st_s, 2), stack(st_s, 3),
            stack(st_p, 1), stack(st_s, 1), stack(st_p, 0), stack(st_s, 0), stack(st_p, 4), stack(st_s, 5),
            stack(st_s, 4))
```

```python
import functools

import jax
import jax.numpy as jnp
from jax import lax
from jax.experimental import pallas as pl
from jax.experimental.pallas import tpu as pltpu

F32, BF16 = jnp.float32, jnp.bfloat16

D_MODEL = 2048
GROUP_WIDTH = 512
N_HEADS = 8
HEAD_DIM = 64
TM_LORA = 32
A_PROJ = 3 * GROUP_WIDTH + 3 * TM_LORA
IN_WIDTH = A_PROJ + 7 * GROUP_WIDTH
IN_PAD = 11 * GROUP_WIDTH
MOBA_BLOCK = 256
MOBA_TOPK = 3
SG_CHUNK = 128
CONV_WIDTH = 31
CONV_HALO = 32
RMS_EPS = 1e-6
LN_EPS = 1e-5
GN_EPS = 64e-5
RWKV_CHUNK = 64
INV_BLOCK = 16
LANES = 128

CB_Q, CB_K, CB_V, CB_SU, CB_SV, CB_CA, CB_CG, CB_R, CB_RK, CB_RV, CB_L = range(11)
(V_MU_R, V_MU_K, V_MU_V, V_W0, V_A0, V_KK, V_KA, V_RK, V_LNG, V_LNB, V_MU_L) = range(11)

VMEM_LIMIT = 56 * 1024 * 1024


def _cp(*sem):
    return pltpu.CompilerParams(dimension_semantics=sem, vmem_limit_bytes=VMEM_LIMIT)


def _dot(a, b):
    return jnp.dot(a, b, preferred_element_type=F32)


def _split2(x):
    hi = x.astype(BF16)
    return hi, (x - hi.astype(F32)).astype(BF16)


def _split3(x):
    h1 = x.astype(BF16)
    r1 = x - h1.astype(F32)
    h2 = r1.astype(BF16)
    return h1, h2, (r1 - h2.astype(F32)).astype(BF16)


def _dot_exact_rhs(x, e):
    hi, lo = _split2(x)
    return _dot(hi, e) + _dot(lo, e)


def _head_ones():
    r = lax.broadcasted_iota(jnp.int32, (GROUP_WIDTH, GROUP_WIDTH), 0) // HEAD_DIM
    c = lax.broadcasted_iota(jnp.int32, (GROUP_WIDTH, GROUP_WIDTH), 1) // HEAD_DIM
    return (r == c).astype(BF16)


def _layer_norm(x, g, b):
    mu = jnp.mean(x, axis=-1, keepdims=True)
    d = x - mu
    var = jnp.mean(d * d, axis=-1, keepdims=True)
    return d * lax.rsqrt(var + LN_EPS) * g + b


def _softplus(x):
    return jnp.maximum(x, 0.0) + jnp.log1p(jnp.exp(-jnp.abs(x)))


def _ada_kernel(c_ref, w_ref, b_ref, o_ref):
    c = c_ref[...]
    s = (c * jax.nn.sigmoid(c)).astype(BF16)
    o_ref[0] = _dot(s, w_ref[0].astype(BF16)) + b_ref[0]


def _ada(c_all, w_ada, b_ada):
    depth, d, n = w_ada.shape
    r = c_all.shape[0]
    tn = 1024
    return pl.pallas_call(
        _ada_kernel,
        grid=(depth, n // tn),
        in_specs=[pl.BlockSpec((r, d), lambda l, j: (0, 0)),
                  pl.BlockSpec((1, d, tn), lambda l, j: (l, 0, j)),
                  pl.BlockSpec((1, 1, tn), lambda l, j: (l, 0, j))],
        out_specs=pl.BlockSpec((1, r, tn), lambda l, j: (l, 0, j)),
        out_shape=jax.ShapeDtypeStruct((depth, r, n), F32),
        compiler_params=_cp("arbitrary", "arbitrary"),
        name="ada",
    )(c_all, w_ada, b_ada.reshape(depth, 1, n))


def _norm_mod_kernel(x_ref, g_ref, sc_ref, sh_ref, o_ref):
    x = x_ref[0]
    h = x * lax.rsqrt(jnp.mean(x * x, axis=-1, keepdims=True) + RMS_EPS) * g_ref[...]
    o_ref[0] = (h * (1.0 + sc_ref[0]) + sh_ref[0]).astype(o_ref.dtype)


def _norm_mod(x, g, mod, sh_col, sc_col, tt):
    ng, t, d = x.shape
    r = mod.shape[1]
    return pl.pallas_call(
        _norm_mod_kernel,
        grid=(ng, t // tt),
        in_specs=[pl.BlockSpec((1, tt, d), lambda b, i: (b, i, 0)),
                  pl.BlockSpec((1, d), lambda b, i: (0, 0)),
                  pl.BlockSpec((1, r, d), lambda b, i: (b, 0, sc_col)),
                  pl.BlockSpec((1, r, d), lambda b, i: (b, 0, sh_col))],
        out_specs=pl.BlockSpec((1, tt, d), lambda b, i: (b, i, 0)),
        out_shape=jax.ShapeDtypeStruct((ng, t, d), BF16),
        compiler_params=_cp("arbitrary", "arbitrary"),
        name="norm_mod",
    )(x, g.reshape(1, d), mod, mod)


def _norm_kernel(x_ref, g_ref, o_ref):
    x = x_ref[0]
    o_ref[0] = x * lax.rsqrt(jnp.mean(x * x, axis=-1, keepdims=True) + RMS_EPS) * g_ref[...]


def _norm(x, g, tt):
    ng, t, d = x.shape
    return pl.pallas_call(
        _norm_kernel,
        grid=(ng, t // tt),
        in_specs=[pl.BlockSpec((1, tt, d), lambda b, i: (b, i, 0)),
                  pl.BlockSpec((1, d), lambda b, i: (0, 0))],
        out_specs=pl.BlockSpec((1, tt, d), lambda b, i: (b, i, 0)),
        out_shape=jax.ShapeDtypeStruct((ng, t, d), F32),
        compiler_params=_cp("arbitrary", "arbitrary"),
        name="norm_final",
    )(x, g.reshape(1, d))


def _mm_kernel(*refs, n_x, gated):
    xs, w_ref = refs[:n_x], refs[n_x]
    o_ref = refs[-1]
    acc = None
    k0 = 0
    for x_ref in xs:
        k = x_ref.shape[1]
        part = _dot(x_ref[...], w_ref[k0:k0 + k, :])
        acc = part if acc is None else acc + part
        k0 += k
    if gated:
        res_ref, gt_ref = refs[n_x + 1], refs[n_x + 2]
        o_ref[...] = res_ref[...] + gt_ref[0] * acc
    else:
        o_ref[...] = acc


def _mm(xs, w, tm, tn, res=None, gate=None, gate_col=0, rows_per_group=None):
    m = xs[0].shape[0]
    n = w.shape[1]
    gated = res is not None
    in_specs = [pl.BlockSpec((tm, x.shape[1]), lambda i, j: (i, 0)) for x in xs]
    in_specs.append(pl.BlockSpec((w.shape[0], tn), lambda i, j: (0, j)))
    args = list(xs) + [w]
    if gated:
        r = gate.shape[1]
        tiles_per_group = rows_per_group // tm
        nj = n // tn
        in_specs.append(pl.BlockSpec((tm, tn), lambda i, j: (i, j)))
        in_specs.append(pl.BlockSpec((1, r, tn), lambda i, j: (i // tiles_per_group, 0, gate_col * nj + j)))
        args += [res, gate]
    return pl.pallas_call(
        functools.partial(_mm_kernel, n_x=len(xs), gated=gated),
        grid=(m // tm, n // tn),
        in_specs=in_specs,
        out_specs=pl.BlockSpec((tm, tn), lambda i, j: (i, j)),
        out_shape=jax.ShapeDtypeStruct((m, n), F32),
        compiler_params=_cp("arbitrary", "arbitrary"),
        name="mm_gated" if gated else "mm",
    )(*args)


def _swiglu_kernel(x_ref, wg_ref, wu_ref, o_ref):
    x = x_ref[...]
    g = _dot(x, wg_ref[...])
    u = _dot(x, wu_ref[...])
    o_ref[...] = (g * jax.nn.sigmoid(g) * u).astype(o_ref.dtype)


def _mm_swiglu(x, w, tm, tn):
    m, k = x.shape
    nh = w.shape[1] // 2
    nj = nh // tn
    return pl.pallas_call(
        _swiglu_kernel,
        grid=(m // tm, nj),
        in_specs=[pl.BlockSpec((tm, k), lambda i, j: (i, 0)),
                  pl.BlockSpec((k, tn), lambda i, j: (0, j)),
                  pl.BlockSpec((k, tn), lambda i, j: (0, nj + j))],
        out_specs=pl.BlockSpec((tm, tn), lambda i, j: (i, j)),
        out_shape=jax.ShapeDtypeStruct((m, nh), BF16),
        compiler_params=_cp("arbitrary", "arbitrary"),
        name="mm_swiglu",
    )(x, w, w)


def _rwkv_elem(pr, pk, pv, plo, qr, qk, qv, qlo, vec, wup_ref, e):
    row = lambda i: vec[i:i + 1, :]
    xr = pr + (qr - pr) * row(V_MU_R)
    xk = pk + (qk - pk) * row(V_MU_K)
    xv = pv + (qv - pv) * row(V_MU_V)
    xl = plo + (qlo - plo) * vec[V_MU_L:V_MU_L + 1, 0:LANES]
    lw = _dot(jnp.tanh(xl).astype(BF16), wup_ref[0])
    la = _dot(xl.astype(BF16), wup_ref[1])
    g = _dot(jax.nn.sigmoid(xl).astype(BF16), wup_ref[2])
    w_log = -_softplus(-(row(V_W0) + lw)) - 0.5
    log_decay = -jnp.exp(w_log)
    a = jax.nn.sigmoid(row(V_A0) + la)
    kk = xk * row(V_KK)
    kk = kk / jnp.maximum(jnp.sqrt(_dot_exact_rhs(kk * kk, e)), 1e-12)
    kmod = xk * (1.0 + (a - 1.0) * row(V_KA))
    bonus = _dot_exact_rhs(xr * kmod * row(V_RK), e) * xv
    return xr, log_decay, kmod, xv, kk, kk * a, g, bonus


def _rwkv_prep_kernel(pr_ref, pk_ref, pv_ref, pl_ref, sp_ref, vec_ref, wup_ref, e_ref,
                      rt_ref, at_ref, bt_ref, kt_ref, v_ref, bh_ref, kh_ref, gt_ref, g_ref, bonus_ref,
                      carry):
    ti = pl.program_id(1)

    @pl.when(ti == 0)
    def _():
        carry[...] = sp_ref[0]

    tt = pr_ref.shape[0]
    first = lax.broadcasted_iota(jnp.int32, (tt, 1), 0) == 0
    gw = GROUP_WIDTH
    cur = (pr_ref[...], pk_ref[...], pv_ref[...], pl_ref[...])
    offs = (0, gw, 2 * gw, 3 * gw)
    prev = []
    for p, o in zip(cur, offs):
        w = p.shape[1]
        prev.append(jnp.where(first, carry[:, o:o + w], pltpu.roll(p, 1, axis=0)))
        carry[:, o:o + w] = p[tt - 1:tt, :]
    vec = vec_ref[...]
    e = e_ref[...]
    r, lw, kmod, v, kk, b, g, bonus = _rwkv_elem(*cur, *prev, vec, wup_ref, e)

    ri = lax.broadcasted_iota(jnp.int32, (tt, tt), 0)
    ci = lax.broadcasted_iota(jnp.int32, (tt, tt), 1)
    same = (ri // RWKV_CHUNK) == (ci // RWKV_CHUNK)
    tri = (same & (ci <= ri)).astype(BF16)
    blk = same.astype(BF16)
    parts = _split3(lw)
    cum = sum(_dot(tri, p) for p in parts)
    tot = sum(_dot(blk, p) for p in parts)
    e_out = jnp.exp(-cum)
    to_end = jnp.exp(tot - cum)
    outs = ((rt_ref, r * jnp.exp(cum)), (at_ref, -kk * jnp.exp(cum - lw)), (bt_ref, b * e_out),
            (kt_ref, kmod * e_out), (v_ref, v), (bh_ref, b * to_end), (kh_ref, kmod * to_end),
            (gt_ref, jnp.exp(tot)))
    for ref, val in outs:
        for h in range(N_HEADS):
            ref[0, h] = val[:, h * HEAD_DIM:(h + 1) * HEAD_DIM]
    g_ref[0] = g
    bonus_ref[0] = bonus


def _bmm(a, b, ca, cb, passes):
    dn = (((ca,), (cb,)), ((0,), (0,)))
    d = lambda x, y: lax.dot_general(x, y, dn, preferred_element_type=F32)
    if passes == 1:
        return d(a.astype(BF16), b.astype(BF16))
    ah, al = _split2(a)
    bh, bl = _split2(b)
    return d(ah, bh) + d(ah, bl) + d(al, bh)


def _rwkv_scan_kernel(rt_ref, at_ref, bt_ref, kt_ref, v_ref, bh_ref, kh_ref, gt_ref, s0_ref,
                      o_ref, st_ref, s_scr, *, passes):
    c = pl.program_id(1)

    @pl.when(c == 0)
    def _():
        s_scr[...] = s0_ref[0]

    n = RWKV_CHUNK
    rt, at, bt, kt, v, bh, kh = (r[0] for r in (rt_ref, at_ref, bt_ref, kt_ref, v_ref, bh_ref, kh_ref))
    g_end = gt_ref[0][:, 0:1, :]
    shp = (N_HEADS, n, n)
    ti = lax.broadcasted_iota(jnp.int32, shp, 1)
    tj = lax.broadcasted_iota(jnp.int32, shp, 2)
    strict, incl = tj < ti, tj <= ti
    same = (ti // INV_BLOCK) == (tj // INV_BLOCK)
    nn = lambda a, b, p: _bmm(a, b, 2, 1, p)
    nt = lambda a, b, p: _bmm(a, b, 2, 2, p)
    tn = lambda a, b, p: _bmm(a, b, 1, 1, p)
    p_gen, p_chain = passes

    big = nt(jnp.concatenate([at, rt], axis=1), jnp.concatenate([bt, kt], axis=1), p_gen)
    a_ab = jnp.where(strict, big[:, :n, :n], 0.0)
    a_ak = jnp.where(strict, big[:, :n, n:], 0.0)
    a_rb = jnp.where(incl, big[:, n:, :n], 0.0)
    a_rk = jnp.where(incl, big[:, n:, n:], 0.0)
    av = nn(jnp.concatenate([a_ak, a_rk], axis=1), v, p_gen)

    nd = jnp.where(same, a_ab, 0.0)
    no = jnp.where(same, 0.0, a_ab)
    nd2 = nn(nd, nd, p_gen)
    nd4 = nn(nd2, nd2, p_gen)
    nd8 = nn(nd4, nd4, p_gen)
    y = jnp.concatenate([no, at, av[:, :n]], axis=2)
    for pw in (nd8, nd4, nd2, nd):
        y = y + nn(pw, y, p_gen)
    m = y[:, :, :n]
    z = y[:, :, n:]
    m2 = nn(m, m, p_gen)
    z = z + nn(m2, z, p_gen)
    z = z + nn(m, z, p_gen)
    qo = jnp.concatenate([rt, av[:, n:]], axis=2) + nn(a_rb, z, p_gen)
    pg = tn(z, bh, p_gen)
    gm = pg[:, n:] + tn(v, kh, p_gen)

    s = s_scr[...]
    o_ref[0] = nt(qo[:, :, :HEAD_DIM], s, p_chain) + qo[:, :, HEAD_DIM:]
    s_new = s * g_end + nn(s, pg[:, :n], p_chain) + gm
    s_scr[...] = s_new

    @pl.when(c == pl.num_programs(1) - 1)
    def _():
        st_ref[0] = s_new


def _rwkv_post_kernel(o_ref, g_ref, bonus_ref, vec_ref, e_ref, ya_ref):
    o = jnp.concatenate([o_ref[0, h] for h in range(N_HEADS)], axis=-1)
    e = e_ref[...]
    vec = vec_ref[...]
    mu = _dot_exact_rhs(o, e) * (1.0 / HEAD_DIM)
    d = o - mu
    var = _dot_exact_rhs(d * d, e) * (1.0 / HEAD_DIM)
    on = d * lax.rsqrt(var + GN_EPS) * vec[V_LNG:V_LNG + 1, :] + vec[V_LNB:V_LNB + 1, :]
    ya_ref[0] = ((on + bonus_ref[0]) * g_ref[0]).astype(ya_ref.dtype)


def _rwkv_prompt(proj, nb, t, shift_prev, s0, vec, wup, e, passes):
    tt = 256
    nt_ = t // tt
    gw, nh, hd = GROUP_WIDTH, N_HEADS, HEAD_DIM
    pspec = lambda cb: pl.BlockSpec((tt, gw), lambda b, i: (b * nt_ + i, cb))
    hm_spec = pl.BlockSpec((1, nh, tt, hd), lambda b, i: (b, 0, i, 0))
    ld_spec = pl.BlockSpec((1, tt, gw), lambda b, i: (b, i, 0))
    const = lambda shape: pl.BlockSpec(shape, lambda b, i: (0,) * len(shape))
    hm = jax.ShapeDtypeStruct((nb, nh, t, hd), F32)
    ld = jax.ShapeDtypeStruct((nb, t, gw), F32)
    outs = pl.pallas_call(
        _rwkv_prep_kernel,
        grid=(nb, nt_),
        in_specs=[pspec(CB_R), pspec(CB_RK), pspec(CB_RV),
                  pl.BlockSpec((tt, LANES), lambda b, i: (b * nt_ + i, CB_L * (gw // LANES))),
                  pl.BlockSpec((1, 1, 3 * gw + LANES), lambda b, i: (b, 0, 0)),
                  const(vec.shape), const(wup.shape), const(e.shape)],
        out_specs=[hm_spec] * 8 + [ld_spec] * 2,
        out_shape=[hm] * 8 + [ld] * 2,
        scratch_shapes=[pltpu.VMEM((1, 3 * gw + LANES), F32)],
        compiler_params=_cp("arbitrary", "arbitrary"),
        name="rwkv_prep",
    )(proj, proj, proj, proj, shift_prev, vec, wup, e)
    chunk_in, g, bonus = outs[:8], outs[8], outs[9]

    n = RWKV_CHUNK
    cspec = pl.BlockSpec((1, nh, n, hd), lambda b, c: (b, 0, c, 0))
    sspec = pl.BlockSpec((1, nh, hd, hd), lambda b, c: (b, 0, 0, 0))
    o, s_t = pl.pallas_call(
        functools.partial(_rwkv_scan_kernel, passes=passes),
        grid=(nb, t // n),
        in_specs=[cspec] * 8 + [sspec],
        out_specs=[cspec, sspec],
        out_shape=[hm, jax.ShapeDtypeStruct((nb, nh, hd, hd), F32)],
        scratch_shapes=[pltpu.VMEM((nh, hd, hd), F32)],
        compiler_params=_cp("arbitrary", "arbitrary"),
        name="rwkv_scan",
    )(*chunk_in, s0)

    ya = pl.pallas_call(
        _rwkv_post_kernel,
        grid=(nb, nt_),
        in_specs=[hm_spec, ld_spec, ld_spec, const(vec.shape), const(e.shape)],
        out_specs=ld_spec,
        out_shape=jax.ShapeDtypeStruct((nb, t, gw), BF16),
        compiler_params=_cp("arbitrary", "arbitrary"),
        name="rwkv_post",
    )(o, g, bonus, vec, e)
    return ya, s_t


def _alibi_slope(h):
    return 2.0 ** (-8.0 * (h + 1) / N_HEADS)


LOG2E = 1.4426950408889634
MASKED = -1e30


def _moba_prompt_kernel(q_ref, k_ref, v_ref, selm_ref, o_ref, kh_scr, vh_scr, km_scr, sel_scr, m_scr, acc_scr,
                        *, n_blocks):
    qi = pl.program_id(1)
    blk, nh, hd = MOBA_BLOCK, N_HEADS, HEAD_DIM
    t = n_blocks * blk
    nt_dims = (((1,), (1,)), ((), ()))

    @pl.when(qi == 0)
    def _():
        k = k_ref[...]
        v = v_ref[...]
        one_col = (lax.broadcasted_iota(jnp.int32, (t, hd), 1) == 0).astype(F32)
        for h in range(nh):
            kh_scr[h] = k[:, h * hd:(h + 1) * hd].astype(BF16)
            vh_scr[h] = jnp.concatenate([v[:, h * hd:(h + 1) * hd], one_col], axis=1).astype(BF16)
        rows = [jnp.mean(k[n * blk:(n + 1) * blk, :], axis=0, keepdims=True) for n in range(n_blocks)]
        rows += [jnp.zeros((1, k.shape[1]), F32)] * (selm_ref.shape[0] - n_blocks)
        kmean = jnp.concatenate(rows, axis=0)
        tn_dims = (((0,), (0,)), ((), ()))
        spread = sum(lax.dot_general(p, selm_ref[...], tn_dims, preferred_element_type=F32)
                     for p in _split3(kmean))
        rh = lax.broadcasted_iota(jnp.int32, spread.shape, 0) // hd
        ch = lax.broadcasted_iota(jnp.int32, spread.shape, 1) % nh
        for i, p in enumerate(_split3(jnp.where(rh == ch, spread, 0.0))):
            km_scr[i] = p

    own = qi
    q = q_ref[...]
    q1, q2, q3 = _split3(q * (hd ** -0.5))
    k1, k2, k3 = km_scr[0], km_scr[1], km_scr[2]
    gate = (_dot(q1, k1) + _dot(q1, k2) + _dot(q2, k1)) + (_dot(q1, k3) + _dot(q2, k2) + _dot(q3, k1))
    lane_blk = lax.broadcasted_iota(jnp.int32, gate.shape, 1) // nh
    valid = lane_blk < own
    gate = jnp.where(valid, gate, -jnp.inf)
    rank = jnp.zeros_like(gate)
    for d in range(1, n_blocks):
        earlier = pltpu.roll(gate, nh * d, axis=1)
        later = pltpu.roll(gate, LANES - nh * d, axis=1)
        rank = rank + jnp.where(earlier >= gate, 1.0, 0.0) + jnp.where(later > gate, 1.0, 0.0)
    sel = jnp.where(valid & (rank < min(MOBA_TOPK, n_blocks - 1)), 1.0, 0.0)
    for n in range(n_blocks):
        sel_scr[n] = sel if n == 0 else pltpu.roll(sel, LANES - n * nh, axis=1)

    qa = q * ((hd ** -0.5) * LOG2E)
    qh = [qa[:, h * hd:(h + 1) * hd].astype(BF16) for h in range(nh)]
    rc = (lax.broadcasted_iota(jnp.int32, (blk, blk), 0) - lax.broadcasted_iota(jnp.int32, (blk, blk), 1))
    causal = rc >= 0
    rc = rc.astype(F32)

    start = pl.multiple_of(own * blk, blk)
    for h in range(nh):
        s = lax.dot_general(qh[h], kh_scr[h, pl.ds(start, blk), :], nt_dims, preferred_element_type=F32)
        tm = jnp.where(causal, s - (_alibi_slope(h) * LOG2E) * rc, MASKED)
        m = jnp.max(tm, axis=1, keepdims=True)
        p = jnp.exp2(tm - m)
        m_scr[h] = m
        acc_scr[h] = _dot(p.astype(BF16), vh_scr[h, pl.ds(start, blk), :])

    def past_block(n, carry):
        base = pl.multiple_of(n * blk, blk)
        seln = sel_scr[n]
        ahead = ((own - n) * blk).astype(F32)
        for h in range(nh):
            s = lax.dot_general(qh[h], kh_scr[h, pl.ds(base, blk), :], nt_dims, preferred_element_type=F32)
            slope2 = _alibi_slope(h) * LOG2E
            tm = jnp.where(seln[:, h:h + 1] > 0.5, s - slope2 * rc, MASKED)
            shift = slope2 * ahead
            m_old = m_scr[h]
            m_new = jnp.maximum(m_old, jnp.max(tm, axis=1, keepdims=True) - shift)
            p = jnp.exp2(tm - (m_new + shift))
            m_scr[h] = m_new
            acc_scr[h] = jnp.exp2(m_old - m_new) * acc_scr[h] + _dot(p.astype(BF16), vh_scr[h, pl.ds(base, blk), :])
        return carry

    lax.fori_loop(0, own, past_block, 0)
    outs = []
    for h in range(nh):
        acc = acc_scr[h]
        outs.append(acc[:, 0:hd] / acc[:, hd:hd + 1])
    o_ref[...] = jnp.concatenate(outs, axis=-1).astype(o_ref.dtype)


def _moba_prompt(proj, nb, t):
    blk, gw, nh, hd = MOBA_BLOCK, GROUP_WIDTH, N_HEADS, HEAD_DIM
    n_blocks = t // blk
    assert n_blocks * nh <= LANES // 2
    selm = (jnp.arange(8)[:, None] == jnp.arange(LANES)[None, :] // nh).astype(BF16)
    return pl.pallas_call(
        functools.partial(_moba_prompt_kernel, n_blocks=n_blocks),
        grid=(nb, n_blocks),
        in_specs=[pl.BlockSpec((blk, gw), lambda b, i: (b * n_blocks + i, CB_Q)),
                  pl.BlockSpec((t, gw), lambda b, i: (b, CB_K)),
                  pl.BlockSpec((t, gw), lambda b, i: (b, CB_V)),
                  pl.BlockSpec((8, LANES), lambda b, i: (0, 0))],
        out_specs=pl.BlockSpec((blk, gw), lambda b, i: (b * n_blocks + i, 0)),
        out_shape=jax.ShapeDtypeStruct((nb * t, gw), BF16),
        scratch_shapes=[pltpu.VMEM((nh, t, hd), BF16), pltpu.VMEM((nh, t, 2 * hd), BF16),
                        pltpu.VMEM((3, gw, LANES), BF16), pltpu.VMEM((n_blocks, blk, LANES), F32),
                        pltpu.VMEM((nh, blk, 1), F32), pltpu.VMEM((nh, blk, 2 * hd), F32)],
        compiler_params=_cp("arbitrary", "arbitrary"),
        name="moba_prompt",
    )(proj, proj, proj, selm)


def _sgu_prompt_kernel(pu_ref, pv_ref, w_ref, bias_ref, ln_ref, o_ref):
    u = jax.nn.gelu(pu_ref[...])
    v = _layer_norm(jax.nn.gelu(pv_ref[...]), ln_ref[0:1, :], ln_ref[1:2, :]).astype(BF16)
    cl, hd = SG_CHUNK, HEAD_DIM
    ri = lax.broadcasted_iota(jnp.int32, (cl, cl), 0)
    ci = lax.broadcasted_iota(jnp.int32, (cl, cl), 1)
    ws = [jnp.where(ci <= ri, w_ref[h], 0.0).astype(BF16) for h in range(N_HEADS)]
    bias = bias_ref[...]
    for c in range(u.shape[0] // cl):
        vc = v[c * cl:(c + 1) * cl, :]
        mixed = jnp.concatenate([_dot(ws[h], vc[:, h * hd:(h + 1) * hd]) for h in range(N_HEADS)], axis=-1)
        o_ref[c * cl:(c + 1) * cl, :] = (u[c * cl:(c + 1) * cl, :] * (mixed + bias)).astype(o_ref.dtype)


def _sgu_prompt(proj, nrows, w, bias_exp, ln):
    tt, gw = 256, GROUP_WIDTH
    return pl.pallas_call(
        _sgu_prompt_kernel,
        grid=(nrows // tt,),
        in_specs=[pl.BlockSpec((tt, gw), lambda i: (i, CB_SU)),
                  pl.BlockSpec((tt, gw), lambda i: (i, CB_SV)),
                  pl.BlockSpec(w.shape, lambda i: (0, 0, 0)),
                  pl.BlockSpec(bias_exp.shape, lambda i: (0, 0)),
                  pl.BlockSpec(ln.shape, lambda i: (0, 0))],
        out_specs=pl.BlockSpec((tt, gw), lambda i: (i, 0)),
        out_shape=jax.ShapeDtypeStruct((nrows, gw), BF16),
        compiler_params=_cp("arbitrary"),
        name="sgu_prompt",
    )(proj, proj, w, bias_exp, ln)


def _conv_prompt_kernel(pa_ref, pg_ref, prev_ref, w_ref, vec_ref, o_ref, hist_ref, zf):
    ti = pl.program_id(1)
    tt = pa_ref.shape[0]
    halo = CONV_HALO

    @pl.when(ti == 0)
    def _():
        zf[0:halo, :] = prev_ref[0]

    z = pa_ref[...] * jax.nn.sigmoid(pg_ref[...])
    zf[halo:halo + tt, :] = z
    lead = halo - (CONV_WIDTH - 1)
    acc = jnp.zeros_like(z)
    for i in range(CONV_WIDTH):
        acc = acc + w_ref[i:i + 1, :] * zf[pl.ds(lead + i, tt), :]
    y = _layer_norm(acc + vec_ref[0:1, :], vec_ref[1:2, :], vec_ref[2:3, :])
    o_ref[...] = (y * jax.nn.sigmoid(y)).astype(o_ref.dtype)
    tail = zf[pl.ds(tt, halo), :]
    hist_ref[0] = tail
    zf[0:halo, :] = tail


def _conv_prompt(proj, nb, t, prev, w, vec):
    tt, gw = 256, GROUP_WIDTH
    nt_ = t // tt
    return pl.pallas_call(
        _conv_prompt_kernel,
        grid=(nb, nt_),
        in_specs=[pl.BlockSpec((tt, gw), lambda b, i: (b * nt_ + i, CB_CA)),
                  pl.BlockSpec((tt, gw), lambda b, i: (b * nt_ + i, CB_CG)),
                  pl.BlockSpec((1, CONV_HALO, gw), lambda b, i: (b, 0, 0)),
                  pl.BlockSpec(w.shape, lambda b, i: (0, 0)),
                  pl.BlockSpec(vec.shape, lambda b, i: (0, 0))],
        out_specs=[pl.BlockSpec((tt, gw), lambda b, i: (b * nt_ + i, 0)),
                   pl.BlockSpec((1, CONV_HALO, gw), lambda b, i: (b, 0, 0))],
        out_shape=[jax.ShapeDtypeStruct((nb * t, gw), BF16),
                   jax.ShapeDtypeStruct((nb, CONV_HALO, gw), F32)],
        scratch_shapes=[pltpu.VMEM((CONV_HALO + tt, gw), F32)],
        compiler_params=_cp("arbitrary", "arbitrary"),
        name="conv_prompt",
    )(proj, proj, prev, w, vec)


def _sample_elem_kernel(pr_ref, pk_ref, pv_ref, pl_ref, sp_ref, vec_ref, wup_ref, e_ref,
                        su_ref, sv_ref, sgp_ref, ca_ref, cg_ref, cprev_ref, cw_ref, cvec_ref,
                        r_ref, w_ref, k_ref, v_ref, kk_ref, b_ref, g_ref, bonus_ref,
                        yc_ref, sgv_ref, yd_ref, cnew_ref):
    gw = GROUP_WIDTH
    sp = sp_ref[...]
    r, lw, kmod, v, kk, b, g, bonus = _rwkv_elem(
        pr_ref[...], pk_ref[...], pv_ref[...], pl_ref[...],
        sp[:, 0:gw], sp[:, gw:2 * gw], sp[:, 2 * gw:3 * gw], sp[:, 3 * gw:3 * gw + LANES],
        vec_ref[...], wup_ref, e_ref[...])
    for ref, val in ((r_ref, r), (w_ref, jnp.exp(lw)), (k_ref, kmod), (v_ref, v), (kk_ref, kk),
                     (b_ref, b), (g_ref, g), (bonus_ref, bonus)):
        ref[...] = val
    u = jax.nn.gelu(su_ref[...])
    sv = _layer_norm(jax.nn.gelu(sv_ref[...]), sgp_ref[0:1, :], sgp_ref[1:2, :])
    sgv_ref[...] = sv
    yc_ref[...] = (u * (sgp_ref[2:3, :] * sv + sgp_ref[3:4, :])).astype(yc_ref.dtype)
    z = ca_ref[...] * jax.nn.sigmoid(cg_ref[...])
    nprev = CONV_WIDTH - 1
    acc = cw_ref[nprev:nprev + 1, :] * z
    for i in range(nprev):
        acc = acc + cw_ref[i:i + 1, :] * cprev_ref[i]
        if i > 0:
            cnew_ref[i - 1] = cprev_ref[i]
    cnew_ref[nprev - 1] = z
    y = _layer_norm(acc + cvec_ref[0:1, :], cvec_ref[1:2, :], cvec_ref[2:3, :])
    yd_ref[...] = (y * jax.nn.sigmoid(y)).astype(yd_ref.dtype)


def _sample_wkv_kernel(r_ref, w_ref, k_ref, v_ref, kk_ref, b_ref, g_ref, bonus_ref, ln_ref, s_ref,
                       ya_ref, st_ref):
    hd = HEAD_DIM
    r, w, k, v, kk, b, g, bonus = (x[0] for x in (r_ref, w_ref, k_ref, v_ref, kk_ref, b_ref, g_ref, bonus_ref))
    s = s_ref[0]
    eye = (lax.broadcasted_iota(jnp.int32, (hd, hd), 0) == lax.broadcasted_iota(jnp.int32, (hd, hd), 1))
    eye = eye.astype(F32)[None]
    v_col = jnp.sum(eye * v, axis=2, keepdims=True)
    sa = jnp.sum(s * (-kk), axis=2, keepdims=True)
    s_new = s * w + sa * b + v_col * k
    st_ref[0] = s_new
    o_col = jnp.sum(s_new * r, axis=2, keepdims=True)
    o = jnp.sum(o_col * eye, axis=1, keepdims=True)
    mu = jnp.mean(o, axis=-1, keepdims=True)
    d = o - mu
    var = jnp.mean(d * d, axis=-1, keepdims=True)
    on = d * lax.rsqrt(var + GN_EPS) * ln_ref[0] + ln_ref[1]
    ya_ref[0] = (on + bonus) * g


def _sample_mixers(proj, shift_prev, s0, conv_prev_t, vec, wup, e, sgp, cw, cvec):
    nb = proj.shape[0]
    gw, nh, hd = GROUP_WIDTH, N_HEADS, HEAD_DIM
    col = lambda cb: pl.BlockSpec((nb, gw), lambda i: (0, cb))
    full = lambda x: pl.BlockSpec(x.shape, lambda i: (0,) * x.ndim)
    row_f32 = jax.ShapeDtypeStruct((nb, gw), F32)
    outs = pl.pallas_call(
        _sample_elem_kernel,
        grid=(1,),
        in_specs=[col(CB_R), col(CB_RK), col(CB_RV),
                  pl.BlockSpec((nb, LANES), lambda i: (0, CB_L * (gw // LANES))),
                  full(shift_prev), full(vec), full(wup), full(e),
                  col(CB_SU), col(CB_SV), full(sgp), col(CB_CA), col(CB_CG),
                  full(conv_prev_t), full(cw), full(cvec)],
        out_specs=[pl.BlockSpec((nb, gw), lambda i: (0, 0))] * 11 + [full(conv_prev_t)],
        out_shape=[row_f32] * 8 + [jax.ShapeDtypeStruct((nb, gw), BF16), row_f32,
                                   jax.ShapeDtypeStruct((nb, gw), BF16),
                                   jax.ShapeDtypeStruct(conv_prev_t.shape, F32)],
        compiler_params=_cp("arbitrary"),
        name="sample_elem",
    )(proj, proj, proj, proj, shift_prev, vec, wup, e, proj, proj, sgp, proj, proj, conv_prev_t, cw, cvec)
    heads = [x.reshape(nb, nh, 1, hd) for x in outs[:8]]
    yc, sgv, yd, conv_new_t = outs[8:]
    ln = jnp.stack([vec[V_LNG].reshape(nh, 1, hd), vec[V_LNB].reshape(nh, 1, hd)])
    hspec = pl.BlockSpec((1, nh, 1, hd), lambda b: (b, 0, 0, 0))
    sspec = pl.BlockSpec((1, nh, hd, hd), lambda b: (b, 0, 0, 0))
    ya, s_new = pl.pallas_call(
        _sample_wkv_kernel,
        grid=(nb,),
        in_specs=[hspec] * 8 + [pl.BlockSpec(ln.shape, lambda b: (0, 0, 0, 0)), sspec],
        out_specs=[hspec, sspec],
        out_shape=[jax.ShapeDtypeStruct((nb, nh, 1, hd), F32), jax.ShapeDtypeStruct(s0.shape, F32)],
        compiler_params=_cp("arbitrary"),
        name="sample_wkv",
    )(*heads, ln, s0)
    return ya.reshape(nb, gw).astype(BF16), s_new, yc, sgv, yd, conv_new_t


PAGES_PER_STEP = 16


def _head_diag(n_rows):
    shape = (N_HEADS, n_rows * N_HEADS)
    return (lax.broadcasted_iota(jnp.int32, shape, 1) % N_HEADS) == lax.broadcasted_iota(jnp.int32, shape, 0)


def _moba_scores_kernel(pt_ref, q_ref, *refs):
    pages, o_ref = refs[:-1], refs[-1]
    ps, nh, hd = pages[0].shape[2], N_HEADS, HEAD_DIM
    q = q_ref[0] * (hd ** -0.5)
    q_hi = q.astype(BF16).astype(F32)
    qq = jnp.concatenate([q_hi, q - q_hi], axis=0).astype(BF16)
    diag = _head_diag(ps)
    dn = (((1,), (1,)), ((), ()))
    for i, p_ref in enumerate(pages):
        k_hi, k_lo = _split2(p_ref[0, 0].reshape(ps * nh, hd))
        a = lax.dot_general(qq, k_hi, dn, preferred_element_type=F32)
        b = lax.dot_general(q_hi.astype(BF16), k_lo, dn, preferred_element_type=F32)
        s = a[0:nh] + a[nh:] + b
        o_ref[0, :, i * ps * nh:(i + 1) * ps * nh] = jnp.sum(jnp.where(diag, s, 0.0), axis=0, keepdims=True)


def _lane_class_reduce(x, op):
    w = x.shape[1]
    f = x[:, 0:LANES]
    for j in range(1, w // LANES):
        f = op(f, x[:, j * LANES:(j + 1) * LANES])
    s = N_HEADS
    while s < LANES:
        f = op(f, pltpu.roll(f, s, axis=1))
        s *= 2
    return jnp.concatenate([f] * (w // LANES), axis=1)


def _moba_select_kernel(s_ref, q_ref, kn_ref, hi_ref, eh_ref, tile_ref, slope_ref, p_ref, p0_ref,
                        *, n_past_blocks):
    s = s_ref[0]
    nblk, w = s.shape
    nh = N_HEADS
    length = nblk * MOBA_BLOCK
    hi = hi_ref[...]
    gate = sum(_dot(p, hi) for p in _split3(s)) * (1.0 / MOBA_BLOCK)
    row = lax.broadcasted_iota(jnp.int32, gate.shape, 0)
    sel = jnp.zeros(gate.shape, jnp.bool_)
    for _ in range(min(MOBA_TOPK, n_past_blocks)):
        m = jnp.max(gate, axis=0, keepdims=True)
        idx = jnp.min(jnp.where(gate == m, row, nblk), axis=0, keepdims=True)
        pick = row == idx
        sel = sel | pick
        gate = jnp.where(pick, -jnp.inf, gate)
    picked = lax.dot_general(sel.astype(BF16), hi, (((1,), (1,)), ((), ())),
                             preferred_element_type=F32) > 0.5
    pos = (lax.broadcasted_iota(jnp.int32, s.shape, 0) * MOBA_BLOCK
           + lax.broadcasted_iota(jnp.int32, s.shape, 1) // nh)
    dist = (length - pos).astype(F32)
    logit = jnp.where(picked, s - slope_ref[...] * dist, -jnp.inf)
    qk = q_ref[0] * (HEAD_DIM ** -0.5) * kn_ref[0]
    s0h = sum(_dot(p, eh_ref[...]) for p in _split3(qk))
    s0 = sum(_dot(p, tile_ref[...]) for p in _split3(s0h))
    m = jnp.maximum(_lane_class_reduce(jnp.max(logit, axis=0, keepdims=True), jnp.maximum), s0)
    p = jnp.exp(logit - m)
    p0 = jnp.exp(s0 - m)
    inv = 1.0 / (_lane_class_reduce(jnp.sum(p, axis=0, keepdims=True), jnp.add) + p0)
    p_ref[0] = p * inv
    p0_ref[0] = (p0 * inv)[:, 0:LANES]


def _moba_pv_kernel(pt_ref, p_ref, p0_ref, vn_ref, *refs, n_steps):
    pages, o_ref, acc = refs[:-2], refs[-2], refs[-1]
    j = pl.program_id(1)
    ps, nh, hd = pages[0].shape[2], N_HEADS, HEAD_DIM
    n = ps * nh

    @pl.when(j == 0)
    def _():
        acc[...] = jnp.zeros_like(acc)

    diag = _head_diag(ps)
    tot = acc[...]
    for i, v_ref in enumerate(pages):
        pd = jnp.where(diag, p_ref[0, :, i * n:(i + 1) * n], 0.0).astype(BF16)
        tot = tot + _dot(pd, v_ref[0, 0].reshape(n, hd).astype(BF16))
    acc[...] = tot

    @pl.when(j == n_steps - 1)
    def _():
        lane = lax.broadcasted_iota(jnp.int32, (nh, LANES), 1)
        sub = lax.broadcasted_iota(jnp.int32, (nh, LANES), 0)
        p0 = jnp.sum(jnp.where(lane == sub, p0_ref[0], 0.0), axis=1, keepdims=True)
        o_ref[0] = tot + p0 * vn_ref[0]


def _moba_sample(proj, cache_k, cache_v, layer, page_table):
    nb, n_pages = page_table.shape
    ps, gw, nh, hd = cache_k.shape[2], GROUP_WIDTH, N_HEADS, HEAD_DIM
    length = n_pages * ps
    assert length % MOBA_BLOCK == 0
    n_past_blocks = length // MOBA_BLOCK
    w = MOBA_BLOCK * nh
    pp = min(PAGES_PER_STEP, n_pages)
    n_steps = n_pages // pp
    q, kn, vn = (proj[:, cb * gw:(cb + 1) * gw] for cb in (CB_Q, CB_K, CB_V))

    def page_spec(i):
        return pl.BlockSpec((1, 1, ps, nh, hd), lambda b, j, pt: (layer, pt[b, j * pp + i], 0, 0, 0))

    scores = pl.pallas_call(
        _moba_scores_kernel,
        grid_spec=pltpu.PrefetchScalarGridSpec(
            num_scalar_prefetch=1,
            grid=(nb, n_steps),
            in_specs=[pl.BlockSpec((1, nh, hd), lambda b, j, pt: (b, 0, 0))] + [page_spec(i) for i in range(pp)],
            out_specs=pl.BlockSpec((1, 1, pp * ps * nh), lambda b, j, pt: (b, 0, j)),
        ),
        out_shape=jax.ShapeDtypeStruct((nb, 1, length * nh), F32),
        compiler_params=_cp("arbitrary", "arbitrary"),
        name="moba_scores",
    )(page_table, q.reshape(nb, nh, hd), *([cache_k] * pp))

    lane = jnp.arange(w)
    hi = (lane[:, None] % nh == jnp.arange(LANES)[None, :]).astype(BF16)
    eh = (jnp.arange(gw)[:, None] // hd == jnp.arange(LANES)[None, :]).astype(BF16)
    tile = (jnp.arange(LANES)[:, None] == lane[None, :] % nh).astype(BF16)
    slopes = jnp.array([_alibi_slope(h) for h in range(nh)], F32)[lane % nh][None, :]
    row3 = pl.BlockSpec((1, 1, gw), lambda b: (b, 0, 0))
    const = lambda x: pl.BlockSpec(x.shape, lambda b: (0,) * x.ndim)
    p, p0 = pl.pallas_call(
        functools.partial(_moba_select_kernel, n_past_blocks=n_past_blocks),
        grid=(nb,),
        in_specs=[pl.BlockSpec((1, n_past_blocks, w), lambda b: (b, 0, 0)), row3, row3,
                  const(hi), const(eh), const(tile), const(slopes)],
        out_specs=[pl.BlockSpec((1, n_past_blocks, w), lambda b: (b, 0, 0)),
                   pl.BlockSpec((1, 1, LANES), lambda b: (b, 0, 0))],
        out_shape=[jax.ShapeDtypeStruct((nb, n_past_blocks, w), F32),
                   jax.ShapeDtypeStruct((nb, 1, LANES), F32)],
        compiler_params=_cp("arbitrary"),
        name="moba_select",
    )(scores.reshape(nb, n_past_blocks, w), q.reshape(nb, 1, gw), kn.reshape(nb, 1, gw), hi, eh, tile, slopes)

    out = pl.pallas_call(
        functools.partial(_moba_pv_kernel, n_steps=n_steps),
        grid_spec=pltpu.PrefetchScalarGridSpec(
            num_scalar_prefetch=1,
            grid=(nb, n_steps),
            in_specs=[pl.BlockSpec((1, 1, pp * ps * nh), lambda b, j, pt: (b, 0, j)),
                      pl.BlockSpec((1, 1, LANES), lambda b, j, pt: (b, 0, 0)),
                      pl.BlockSpec((1, nh, hd), lambda b, j, pt: (b, 0, 0))]
                     + [page_spec(i) for i in range(pp)],
            out_specs=pl.BlockSpec((1, nh, hd), lambda b, j, pt: (b, 0, 0)),
            scratch_shapes=[pltpu.VMEM((nh, hd), F32)],
        ),
        out_shape=jax.ShapeDtypeStruct((nb, nh, hd), F32),
        compiler_params=_cp("arbitrary", "arbitrary"),
        name="moba_pv",
    )(page_table, p.reshape(nb, 1, length * nh), p0, vn.reshape(nb, nh, hd), *([cache_v] * pp))
    return out.reshape(nb, gw)


RWKV_PASSES = (1, 3)


def _shift_layout(shift):
    n = shift.shape[0]
    pad = jnp.zeros((n, LANES - 3 * TM_LORA), F32)
    return jnp.concatenate([shift, pad], axis=1)


def _layer_params(l, w_in, w_out, w_ffn_in, w_ffn_out, tm_mu, tm_w0, tm_w_up, tm_a0, tm_a_up, tm_g_up,
                  tm_k_k, tm_k_a, tm_r_k, tm_ln_g, tm_ln_b, sg_ws, sg_bs, sg_ln_g, sg_ln_b,
                  cv_w, cv_b, cv_ln_g, cv_ln_b):
    gw, lr, hd = GROUP_WIDTH, TM_LORA, HEAD_DIM
    wi = w_in[l]
    w_in_p = jnp.concatenate(
        [wi[:, A_PROJ:], wi[:, :A_PROJ], jnp.zeros((wi.shape[0], IN_PAD - IN_WIDTH), F32)], axis=1).astype(BF16)
    mu = tm_mu[l]
    mu_l = jnp.concatenate([mu[3 * gw:], jnp.zeros((gw - 3 * lr,), F32)])
    vec = jnp.stack([mu[0:gw], mu[gw:2 * gw], mu[2 * gw:3 * gw], tm_w0[l], tm_a0[l], tm_k_k[l], tm_k_a[l],
                     tm_r_k[l].reshape(gw), tm_ln_g[l], tm_ln_b[l], mu_l]
                    + [jnp.zeros((gw,), F32)] * 5)
    wup = jnp.zeros((3, LANES, gw), F32)
    wup = wup.at[0, 0:lr].set(tm_w_up[l]).at[1, lr:2 * lr].set(tm_a_up[l]).at[2, 2 * lr:3 * lr].set(tm_g_up[l])
    sg_bias = jnp.repeat(sg_bs[l].T, hd, axis=1)
    sg_ln = jnp.stack([sg_ln_g[l], sg_ln_b[l]] + [jnp.zeros((gw,), F32)] * 6)
    sgp = jnp.stack([sg_ln_g[l], sg_ln_b[l], jnp.repeat(sg_ws[l][:, 0, 0], hd), jnp.repeat(sg_bs[l][:, 0], hd)]
                    + [jnp.zeros((gw,), F32)] * 4)
    cw = jnp.concatenate([cv_w[l], jnp.zeros((1, gw), F32)], axis=0)
    cvec = jnp.stack([cv_b[l], cv_ln_g[l], cv_ln_b[l]] + [jnp.zeros((gw,), F32)] * 5)
    return dict(w_in=w_in_p, w_out=w_out[l].astype(BF16), w_ffn_in=w_ffn_in[l].astype(BF16),
                w_ffn_out=w_ffn_out[l].astype(BF16), vec=vec, wup=wup.astype(BF16), sg_w=sg_ws[l],
                sg_bias=sg_bias, sg_ln=sg_ln, sgp=sgp, cw=cw, cvec=cvec)


def _dense_tail(x, ys, mod, p, g_ffn, tm, rows_per_group):
    ng, t, d = x.shape
    m = ng * t
    x1 = _mm(ys, p['w_out'], tm, min(1024, d), res=x.reshape(m, d), gate=mod, gate_col=2,
             rows_per_group=rows_per_group)
    h2 = _norm_mod(x1.reshape(ng, t, d), g_ffn, mod, 3, 4, min(512, t)).reshape(m, d)
    hid = _mm_swiglu(h2, p['w_ffn_in'], tm, 512)
    x2 = _mm([hid], p['w_ffn_out'], min(tm, 512), 512, res=x1, gate=mod, gate_col=5,
             rows_per_group=rows_per_group)
    return x2.reshape(ng, t, d)


def _prompt_layer(x, mod, p, g_mix, g_ffn, e):
    nb, t, d = x.shape
    m = nb * t
    gw, nh, hd = GROUP_WIDTH, N_HEADS, HEAD_DIM
    h = _norm_mod(x, g_mix, mod, 0, 1, 512).reshape(m, d)
    proj = _mm([h], p['w_in'], 1024, 512)
    shift0 = jnp.zeros((nb, 1, 3 * gw + LANES), F32)
    s0 = jnp.zeros((nb, nh, hd, hd), F32)
    ya, wkv = _rwkv_prompt(proj, nb, t, shift0, s0, p['vec'], p['wup'], e, RWKV_PASSES)
    yb = _moba_prompt(proj, nb, t)
    yc = _sgu_prompt(proj, m, p['sg_w'], p['sg_bias'], p['sg_ln'])
    yd, hist = _conv_prompt(proj, nb, t, jnp.zeros((nb, CONV_HALO, gw), F32), p['cw'], p['cvec'])
    x2 = _dense_tail(x, [ya.reshape(m, gw), yb, yc, yd], mod, p, g_ffn, 1024, t)
    proj3 = proj.reshape(nb, t, IN_PAD)
    last = proj3[:, t - 1]
    shift = jnp.concatenate([last[:, CB_R * gw:CB_L * gw], last[:, CB_L * gw:CB_L * gw + 3 * TM_LORA]], axis=1)
    new_k = proj3[:, :, CB_K * gw:(CB_K + 1) * gw].reshape(nb, t, nh, hd)
    new_v = proj3[:, :, CB_V * gw:(CB_V + 1) * gw].reshape(nb, t, nh, hd)
    return x2, (shift, wkv, new_k, new_v, hist[:, CONV_HALO - (CONV_WIDTH - 1):])


def _sample_layer(x, mod, p, g_mix, g_ffn, e, layer, shift_prev, wkv_prev, conv_prev, cache_k, cache_v,
                  page_table):
    nb, t, d = x.shape
    gw, nh, hd = GROUP_WIDTH, N_HEADS, HEAD_DIM
    xg = x.reshape(1, nb, d)
    h = _norm_mod(xg, g_mix, mod, 0, 1, nb).reshape(nb, d)
    proj = _mm([h], p['w_in'], nb, 512)
    ya, wkv, yc, sgv, yd, conv_new_t = _sample_mixers(
        proj, _shift_layout(shift_prev), wkv_prev, conv_prev.transpose(1, 0, 2),
        p['vec'], p['wup'], e, p['sgp'], p['cw'], p['cvec'])
    yb = _moba_sample(proj, cache_k, cache_v, layer, page_table).astype(BF16)
    x2 = _dense_tail(xg, [ya, yb, yc, yd], mod, p, g_ffn, nb, nb)
    shift = jnp.concatenate([proj[:, CB_R * gw:CB_L * gw], proj[:, CB_L * gw:CB_L * gw + 3 * TM_LORA]], axis=1)
    new_k = proj[:, CB_K * gw:(CB_K + 1) * gw].reshape(nb, 1, nh, hd)
    new_v = proj[:, CB_V * gw:(CB_V + 1) * gw].reshape(nb, 1, nh, hd)
    return x2.reshape(nb, t, d), (shift, wkv, new_k, new_v, sgv.reshape(nb, 1, gw), conv_new_t.transpose(1, 0, 2))


def kernel(x_prompt, x_sample, c_prompt, c_sample, cache_k_b, cache_v_b, state_a_wkv, state_a_shift, state_d_conv, page_table, w_ada, b_ada, g_norm_mix, w_in, w_out, tm_mu, tm_w0, tm_w_up, tm_a0, tm_a_up, tm_g_up, tm_k_k, tm_k_a, tm_r_k, tm_ln_g, tm_ln_b, sg_ws, sg_bs, sg_ln_g, sg_ln_b, cv_w, cv_b, cv_ln_g, cv_ln_b, g_norm_ffn, w_ffn_in, w_ffn_out, g_norm_final):
    depth = w_in.shape[0]
    bp, ns = x_prompt.shape[0], x_sample.shape[0]
    gw = GROUP_WIDTH
    rows = bp + ns
    rows_pad = -(-rows // 8) * 8
    c_all = jnp.concatenate([c_prompt, c_sample, jnp.zeros((rows_pad - rows, c_prompt.shape[1]), F32)], axis=0)
    mod = _ada(c_all, w_ada, b_ada)
    e = _head_ones()
    xp, xs = x_prompt, x_sample
    st_p, st_s = [], []
    for l in range(depth):
        p = _layer_params(l, w_in, w_out, w_ffn_in, w_ffn_out, tm_mu, tm_w0, tm_w_up, tm_a0, tm_a_up, tm_g_up,
                          tm_k_k, tm_k_a, tm_r_k, tm_ln_g, tm_ln_b, sg_ws, sg_bs, sg_ln_g, sg_ln_b,
                          cv_w, cv_b, cv_ln_g, cv_ln_b)
        mod_p = mod[l, :bp][:, None, :]
        mod_s = mod[l, bp:rows][None]
        xp, sp = _prompt_layer(xp, mod_p, p, g_norm_mix[l], g_norm_ffn[l], e)
        xs, ss = _sample_layer(xs, mod_s, p, g_norm_mix[l], g_norm_ffn[l], e, l, state_a_shift[l],
                               state_a_wkv[l], state_d_conv[l], cache_k_b, cache_v_b, page_table)
        st_p.append(sp)
        st_s.append(ss)
    y_prompt = _norm(xp, g_norm_final, 512)
    y_sample = _norm(xs.reshape(1, ns, -1), g_norm_final, ns).reshape(xs.shape)
    stack = lambda sts, i: jnp.stack([s[i] for s in sts])
    return (y_prompt, y_sample, stack(st_p, 2), stack(st_p, 3), stack(st_s, 2), stack(st_s, 3),
            stack(st_p, 1), stack(st_s, 1), stack(st_p, 0), stack(st_s, 0), stack(st_p, 4), stack(st_s, 5),
            stack(st_s, 4))
```

```python
import functools

import jax
import jax.numpy as jnp
from jax import lax
from jax.experimental import pallas as pl
from jax.experimental.pallas import tpu as pltpu

F32, BF16 = jnp.float32, jnp.bfloat16

D_MODEL = 2048
GROUP_WIDTH = 512
N_HEADS = 8
HEAD_DIM = 64
TM_LORA = 32
A_PROJ = 3 * GROUP_WIDTH + 3 * TM_LORA
IN_WIDTH = A_PROJ + 7 * GROUP_WIDTH
IN_PAD = 11 * GROUP_WIDTH
MOBA_BLOCK = 256
MOBA_TOPK = 3
SG_CHUNK = 128
CONV_WIDTH = 31
CONV_HALO = 32
RMS_EPS = 1e-6
LN_EPS = 1e-5
GN_EPS = 64e-5
RWKV_CHUNK = 64
RWKV_CHUNKS_PER_STEP = 4
INV_BLOCK = 16
LANES = 128

CB_Q, CB_K, CB_V, CB_SU, CB_SV, CB_CA, CB_CG, CB_R, CB_RK, CB_RV, CB_L = range(11)
(V_MU_R, V_MU_K, V_MU_V, V_W0, V_A0, V_KK, V_KA, V_RK, V_LNG, V_LNB, V_MU_L) = range(11)

VMEM_LIMIT = 56 * 1024 * 1024


def _cp(*sem):
    return pltpu.CompilerParams(dimension_semantics=sem, vmem_limit_bytes=VMEM_LIMIT)


def _dot(a, b):
    return jnp.dot(a, b, preferred_element_type=F32)


def _split2(x):
    hi = x.astype(BF16)
    return hi, (x - hi.astype(F32)).astype(BF16)


def _split3(x):
    h1 = x.astype(BF16)
    r1 = x - h1.astype(F32)
    h2 = r1.astype(BF16)
    return h1, h2, (r1 - h2.astype(F32)).astype(BF16)


def _dot_exact_rhs(x, e):
    hi, lo = _split2(x)
    return _dot(hi, e) + _dot(lo, e)


def _head_ones():
    r = lax.broadcasted_iota(jnp.int32, (GROUP_WIDTH, GROUP_WIDTH), 0) // HEAD_DIM
    c = lax.broadcasted_iota(jnp.int32, (GROUP_WIDTH, GROUP_WIDTH), 1) // HEAD_DIM
    return (r == c).astype(BF16)


def _layer_norm(x, g, b):
    mu = jnp.mean(x, axis=-1, keepdims=True)
    d = x - mu
    var = jnp.mean(d * d, axis=-1, keepdims=True)
    return d * lax.rsqrt(var + LN_EPS) * g + b


def _softplus(x):
    return jnp.maximum(x, 0.0) + jnp.log1p(jnp.exp(-jnp.abs(x)))


def _dot3(x, w):
    xh, xl = _split2(x)
    wh, wl = _split2(w)
    return _dot(xh, wh) + _dot(xh, wl) + _dot(xl, wh)


def _ada_kernel(c_ref, w_ref, b_ref, o_ref):
    c = c_ref[...]
    o_ref[0] = _dot3(c * jax.nn.sigmoid(c), w_ref[0]) + b_ref[0]


def _ada(c_all, w_ada, b_ada):
    depth, d, n = w_ada.shape
    r = c_all.shape[0]
    tn = 512
    return pl.pallas_call(
        _ada_kernel,
        grid=(depth, n // tn),
        in_specs=[pl.BlockSpec((r, d), lambda l, j: (0, 0)),
                  pl.BlockSpec((1, d, tn), lambda l, j: (l, 0, j)),
                  pl.BlockSpec((1, 1, tn), lambda l, j: (l, 0, j))],
        out_specs=pl.BlockSpec((1, r, tn), lambda l, j: (l, 0, j)),
        out_shape=jax.ShapeDtypeStruct((depth, r, n), F32),
        compiler_params=_cp("arbitrary", "arbitrary"),
        name="ada",
    )(c_all, w_ada, b_ada.reshape(depth, 1, n))


def _norm_mod_kernel(x_ref, g_ref, sc_ref, sh_ref, o_ref):
    x = x_ref[0]
    h = x * lax.rsqrt(jnp.mean(x * x, axis=-1, keepdims=True) + RMS_EPS) * g_ref[...]
    o_ref[0] = (h * (1.0 + sc_ref[0]) + sh_ref[0]).astype(o_ref.dtype)


def _norm_mod(x, g, mod, sh_col, sc_col, tt, dtype=BF16):
    ng, t, d = x.shape
    r = mod.shape[1]
    return pl.pallas_call(
        _norm_mod_kernel,
        grid=(ng, t // tt),
        in_specs=[pl.BlockSpec((1, tt, d), lambda b, i: (b, i, 0)),
                  pl.BlockSpec((1, d), lambda b, i: (0, 0)),
                  pl.BlockSpec((1, r, d), lambda b, i: (b, 0, sc_col)),
                  pl.BlockSpec((1, r, d), lambda b, i: (b, 0, sh_col))],
        out_specs=pl.BlockSpec((1, tt, d), lambda b, i: (b, i, 0)),
        out_shape=jax.ShapeDtypeStruct((ng, t, d), dtype),
        compiler_params=_cp("arbitrary", "arbitrary"),
        name="norm_mod",
    )(x, g.reshape(1, d), mod, mod)


def _norm_kernel(x_ref, g_ref, o_ref):
    x = x_ref[0]
    o_ref[0] = x * lax.rsqrt(jnp.mean(x * x, axis=-1, keepdims=True) + RMS_EPS) * g_ref[...]


def _norm(x, g, tt):
    ng, t, d = x.shape
    return pl.pallas_call(
        _norm_kernel,
        grid=(ng, t // tt),
        in_specs=[pl.BlockSpec((1, tt, d), lambda b, i: (b, i, 0)),
                  pl.BlockSpec((1, d), lambda b, i: (0, 0))],
        out_specs=pl.BlockSpec((1, tt, d), lambda b, i: (b, i, 0)),
        out_shape=jax.ShapeDtypeStruct((ng, t, d), F32),
        compiler_params=_cp("arbitrary", "arbitrary"),
        name="norm_final",
    )(x, g.reshape(1, d))


def _mm_kernel(*refs, n_x, gated):
    xs, w_ref = refs[:n_x], refs[n_x]
    o_ref = refs[-1]
    acc = None
    k0 = 0
    for x_ref in xs:
        k = x_ref.shape[1]
        part = _dot(x_ref[...], w_ref[k0:k0 + k, :])
        acc = part if acc is None else acc + part
        k0 += k
    if gated:
        res_ref, gt_ref = refs[n_x + 1], refs[n_x + 2]
        o_ref[...] = res_ref[...] + gt_ref[0] * acc
    else:
        o_ref[...] = acc


def _mm(xs, w, tm, tn, res=None, gate=None, gate_col=0, rows_per_group=None):
    m = xs[0].shape[0]
    n = w.shape[1]
    gated = res is not None
    in_specs = [pl.BlockSpec((tm, x.shape[1]), lambda i, j: (i, 0)) for x in xs]
    in_specs.append(pl.BlockSpec((w.shape[0], tn), lambda i, j: (0, j)))
    args = list(xs) + [w]
    if gated:
        r = gate.shape[1]
        tiles_per_group = rows_per_group // tm
        nj = n // tn
        in_specs.append(pl.BlockSpec((tm, tn), lambda i, j: (i, j)))
        in_specs.append(pl.BlockSpec((1, r, tn), lambda i, j: (i // tiles_per_group, 0, gate_col * nj + j)))
        args += [res, gate]
    return pl.pallas_call(
        functools.partial(_mm_kernel, n_x=len(xs), gated=gated),
        grid=(m // tm, n // tn),
        in_specs=in_specs,
        out_specs=pl.BlockSpec((tm, tn), lambda i, j: (i, j)),
        out_shape=jax.ShapeDtypeStruct((m, n), F32),
        compiler_params=_cp("arbitrary", "arbitrary"),
        name="mm_gated" if gated else "mm",
    )(*args)


def _mm_precise_kernel(*refs, n_x, mode):
    xs = refs[:n_x]
    o_ref = refs[-1]
    if mode == "swiglu":
        x = xs[0][...]
        g = _dot3(x, refs[n_x][0])
        o_ref[...] = g * jax.nn.sigmoid(g) * _dot3(x, refs[n_x + 1][0])
        return
    w_ref = refs[n_x]
    acc = None
    k0 = 0
    for x_ref in xs:
        k = x_ref.shape[1]
        part = _dot3(x_ref[...], w_ref[0, k0:k0 + k, :])
        acc = part if acc is None else acc + part
        k0 += k
    if mode == "gated":
        res_ref, gt_ref = refs[n_x + 1], refs[n_x + 2]
        o_ref[...] = res_ref[...] + gt_ref[0] * acc
    else:
        o_ref[...] = acc


def _mm_precise(xs, w, layer, tn, mode="plain", res=None, gate=None, gate_col=0):
    m = xs[0].shape[0]
    kdim = w.shape[1]
    n = w.shape[2] // 2 if mode == "swiglu" else w.shape[2]
    nj = n // tn
    in_specs = [pl.BlockSpec((m, x.shape[1]), lambda j: (0, 0)) for x in xs]
    in_specs.append(pl.BlockSpec((1, kdim, tn), lambda j: (layer, 0, j)))
    args = list(xs) + [w]
    if mode == "swiglu":
        in_specs.append(pl.BlockSpec((1, kdim, tn), lambda j: (layer, 0, nj + j)))
        args.append(w)
    if mode == "gated":
        in_specs.append(pl.BlockSpec((m, tn), lambda j: (0, j)))
        in_specs.append(pl.BlockSpec((1, m, tn), lambda j: (0, 0, gate_col * nj + j)))
        args += [res, gate]
    return pl.pallas_call(
        functools.partial(_mm_precise_kernel, n_x=len(xs), mode=mode),
        grid=(nj,),
        in_specs=in_specs,
        out_specs=pl.BlockSpec((m, tn), lambda j: (0, j)),
        out_shape=jax.ShapeDtypeStruct((m, n), F32),
        compiler_params=_cp("arbitrary"),
        name="mm_precise_" + mode,
    )(*args)


def _swiglu_kernel(x_ref, wg_ref, wu_ref, o_ref):
    x = x_ref[...]
    g = _dot(x, wg_ref[...])
    u = _dot(x, wu_ref[...])
    o_ref[...] = (g * jax.nn.sigmoid(g) * u).astype(o_ref.dtype)


def _mm_swiglu(x, w, tm, tn):
    m, k = x.shape
    nh = w.shape[1] // 2
    nj = nh // tn
    return pl.pallas_call(
        _swiglu_kernel,
        grid=(m // tm, nj),
        in_specs=[pl.BlockSpec((tm, k), lambda i, j: (i, 0)),
                  pl.BlockSpec((k, tn), lambda i, j: (0, j)),
                  pl.BlockSpec((k, tn), lambda i, j: (0, nj + j))],
        out_specs=pl.BlockSpec((tm, tn), lambda i, j: (i, j)),
        out_shape=jax.ShapeDtypeStruct((m, nh), BF16),
        compiler_params=_cp("arbitrary", "arbitrary"),
        name="mm_swiglu",
    )(x, w, w)


def _rwkv_elem(pr, pk, pv, plo, qr, qk, qv, qlo, vec, wup_ref, e, precise=False):
    row = lambda i: vec[i:i + 1, :]
    xr = pr + (qr - pr) * row(V_MU_R)
    xk = pk + (qk - pk) * row(V_MU_K)
    xv = pv + (qv - pv) * row(V_MU_V)
    xl = plo + (qlo - plo) * vec[V_MU_L:V_MU_L + 1, 0:LANES]
    up = _dot3 if precise else (lambda x, w: _dot(x.astype(BF16), w.astype(BF16)))
    lw = up(jnp.tanh(xl), wup_ref[0])
    la = up(xl, wup_ref[1])
    g = up(jax.nn.sigmoid(xl), wup_ref[2])
    w_log = -_softplus(-(row(V_W0) + lw)) - 0.5
    log_decay = -jnp.exp(w_log)
    a = jax.nn.sigmoid(row(V_A0) + la)
    kk = xk * row(V_KK)
    kk = kk / jnp.maximum(jnp.sqrt(_dot_exact_rhs(kk * kk, e)), 1e-12)
    kmod = xk * (1.0 + (a - 1.0) * row(V_KA))
    bonus = _dot_exact_rhs(xr * kmod * row(V_RK), e) * xv
    return xr, log_decay, kmod, xv, kk, kk * a, g, bonus


def _rwkv_prep_kernel(pr_ref, pk_ref, pv_ref, pl_ref, sp_ref, vec_ref, wup_ref, e_ref,
                      rt_ref, at_ref, bt_ref, kt_ref, v_ref, bh_ref, kh_ref, gt_ref, g_ref, bonus_ref,
                      carry):
    ti = pl.program_id(1)

    @pl.when(ti == 0)
    def _():
        carry[...] = sp_ref[0]

    tt = pr_ref.shape[0]
    first = lax.broadcasted_iota(jnp.int32, (tt, 1), 0) == 0
    gw = GROUP_WIDTH
    cur = (pr_ref[...], pk_ref[...], pv_ref[...], pl_ref[...])
    offs = (0, gw, 2 * gw, 3 * gw)
    prev = []
    for p, o in zip(cur, offs):
        w = p.shape[1]
        prev.append(jnp.where(first, carry[:, o:o + w], pltpu.roll(p, 1, axis=0)))
        carry[:, o:o + w] = p[tt - 1:tt, :]
    vec = vec_ref[...]
    e = e_ref[...]
    r, lw, kmod, v, kk, b, g, bonus = _rwkv_elem(*cur, *prev, vec, wup_ref, e)

    ri = lax.broadcasted_iota(jnp.int32, (tt, tt), 0)
    ci = lax.broadcasted_iota(jnp.int32, (tt, tt), 1)
    same = (ri // RWKV_CHUNK) == (ci // RWKV_CHUNK)
    tri = (same & (ci <= ri)).astype(BF16)
    blk = same.astype(BF16)
    parts = _split3(lw)
    cum = sum(_dot(tri, p) for p in parts)
    tot = sum(_dot(blk, p) for p in parts)
    e_out = jnp.exp(-cum)
    to_end = jnp.exp(tot - cum)
    outs = ((rt_ref, r * jnp.exp(cum)), (at_ref, -kk * jnp.exp(cum - lw)), (bt_ref, b * e_out),
            (kt_ref, kmod * e_out), (v_ref, v), (bh_ref, b * to_end), (kh_ref, kmod * to_end),
            (gt_ref, jnp.exp(tot)))
    for ref, val in outs:
        for h in range(N_HEADS):
            ref[0, h] = val[:, h * HEAD_DIM:(h + 1) * HEAD_DIM]
    g_ref[0] = g
    bonus_ref[0] = bonus


def _bmm(a, b, ca, cb, passes):
    dn = (((ca,), (cb,)), ((0,), (0,)))
    d = lambda x, y: lax.dot_general(x, y, dn, preferred_element_type=F32)
    if passes == 1:
        return d(a.astype(BF16), b.astype(BF16))
    ah, al = _split2(a)
    bh, bl = _split2(b)
    return d(ah, bh) + d(ah, bl) + d(al, bh)


def _rwkv_scan_kernel(rt_ref, at_ref, bt_ref, kt_ref, v_ref, bh_ref, kh_ref, gt_ref, s0_ref,
                      o_ref, st_ref, s_scr, *, passes):
    c = pl.program_id(1)

    @pl.when(c == 0)
    def _():
        s_scr[...] = s0_ref[0]

    n = RWKV_CHUNK
    shp = (N_HEADS, n, n)
    ti = lax.broadcasted_iota(jnp.int32, shp, 1)
    tj = lax.broadcasted_iota(jnp.int32, shp, 2)
    strict, incl = tj < ti, tj <= ti
    same = (ti // INV_BLOCK) == (tj // INV_BLOCK)
    nn = lambda a, b, p: _bmm(a, b, 2, 1, p)
    nt = lambda a, b, p: _bmm(a, b, 2, 2, p)
    tn = lambda a, b, p: _bmm(a, b, 1, 1, p)
    p_gen, p_chain = passes

    def chunk_terms(i):
        sl = slice(i * n, (i + 1) * n)
        rt, at, bt, kt, v, bh, kh = (r[0, :, sl, :] for r in (rt_ref, at_ref, bt_ref, kt_ref, v_ref, bh_ref, kh_ref))
        g_end = gt_ref[0, :, i * n:i * n + 1, :]
        big = nt(jnp.concatenate([at, rt], axis=1), jnp.concatenate([bt, kt], axis=1), p_gen)
        a_ab = jnp.where(strict, big[:, :n, :n], 0.0)
        a_ak = jnp.where(strict, big[:, :n, n:], 0.0)
        a_rb = jnp.where(incl, big[:, n:, :n], 0.0)
        a_rk = jnp.where(incl, big[:, n:, n:], 0.0)
        av = nn(jnp.concatenate([a_ak, a_rk], axis=1), v, p_gen)
        nd = jnp.where(same, a_ab, 0.0)
        no = jnp.where(same, 0.0, a_ab)
        nd2 = nn(nd, nd, p_gen)
        nd4 = nn(nd2, nd2, p_gen)
        nd8 = nn(nd4, nd4, p_gen)
        y = jnp.concatenate([no, at, av[:, :n]], axis=2)
        for pw in (nd8, nd4, nd2, nd):
            y = y + nn(pw, y, p_gen)
        m = y[:, :, :n]
        z = y[:, :, n:]
        m2 = nn(m, m, p_gen)
        z = z + nn(m2, z, p_gen)
        z = z + nn(m, z, p_gen)
        qo = jnp.concatenate([rt, av[:, n:]], axis=2) + nn(a_rb, z, p_gen)
        pg = tn(z, bh, p_gen)
        gm = pg[:, n:] + tn(v, kh, p_gen)
        return qo, pg[:, :n], gm, g_end

    terms = [chunk_terms(i) for i in range(rt_ref.shape[2] // n)]
    s = s_scr[...]
    for i, (qo, pmat, gm, g_end) in enumerate(terms):
        o_ref[0, :, i * n:(i + 1) * n, :] = nt(qo[:, :, :HEAD_DIM], s, p_chain) + qo[:, :, HEAD_DIM:]
        s = s * g_end + nn(s, pmat, p_chain) + gm
    s_scr[...] = s

    @pl.when(c == pl.num_programs(1) - 1)
    def _():
        st_ref[0] = s


def _rwkv_post_kernel(o_ref, g_ref, bonus_ref, vec_ref, e_ref, ya_ref):
    o = jnp.concatenate([o_ref[0, h] for h in range(N_HEADS)], axis=-1)
    e = e_ref[...]
    vec = vec_ref[...]
    mu = _dot_exact_rhs(o, e) * (1.0 / HEAD_DIM)
    d = o - mu
    var = _dot_exact_rhs(d * d, e) * (1.0 / HEAD_DIM)
    on = d * lax.rsqrt(var + GN_EPS) * vec[V_LNG:V_LNG + 1, :] + vec[V_LNB:V_LNB + 1, :]
    ya_ref[0] = ((on + bonus_ref[0]) * g_ref[0]).astype(ya_ref.dtype)


def _rwkv_prompt(proj, nb, t, shift_prev, s0, vec, wup, e, passes):
    tt = 256
    nt_ = t // tt
    gw, nh, hd = GROUP_WIDTH, N_HEADS, HEAD_DIM
    pspec = lambda cb: pl.BlockSpec((tt, gw), lambda b, i: (b * nt_ + i, cb))
    hm_spec = pl.BlockSpec((1, nh, tt, hd), lambda b, i: (b, 0, i, 0))
    ld_spec = pl.BlockSpec((1, tt, gw), lambda b, i: (b, i, 0))
    const = lambda shape: pl.BlockSpec(shape, lambda b, i: (0,) * len(shape))
    hm = jax.ShapeDtypeStruct((nb, nh, t, hd), F32)
    ld = jax.ShapeDtypeStruct((nb, t, gw), F32)
    outs = pl.pallas_call(
        _rwkv_prep_kernel,
        grid=(nb, nt_),
        in_specs=[pspec(CB_R), pspec(CB_RK), pspec(CB_RV),
                  pl.BlockSpec((tt, LANES), lambda b, i: (b * nt_ + i, CB_L * (gw // LANES))),
                  pl.BlockSpec((1, 1, 3 * gw + LANES), lambda b, i: (b, 0, 0)),
                  const(vec.shape), const(wup.shape), const(e.shape)],
        out_specs=[hm_spec] * 8 + [ld_spec] * 2,
        out_shape=[hm] * 8 + [ld] * 2,
        scratch_shapes=[pltpu.VMEM((1, 3 * gw + LANES), F32)],
        compiler_params=_cp("arbitrary", "arbitrary"),
        name="rwkv_prep",
    )(proj, proj, proj, proj, shift_prev, vec, wup, e)
    chunk_in, g, bonus = outs[:8], outs[8], outs[9]

    n = RWKV_CHUNKS_PER_STEP * RWKV_CHUNK
    cspec = pl.BlockSpec((1, nh, n, hd), lambda b, c: (b, 0, c, 0))
    sspec = pl.BlockSpec((1, nh, hd, hd), lambda b, c: (b, 0, 0, 0))
    o, s_t = pl.pallas_call(
        functools.partial(_rwkv_scan_kernel, passes=passes),
        grid=(nb, t // n),
        in_specs=[cspec] * 8 + [sspec],
        out_specs=[cspec, sspec],
        out_shape=[hm, jax.ShapeDtypeStruct((nb, nh, hd, hd), F32)],
        scratch_shapes=[pltpu.VMEM((nh, hd, hd), F32)],
        compiler_params=_cp("arbitrary", "arbitrary"),
        name="rwkv_scan",
    )(*chunk_in, s0)

    ya = pl.pallas_call(
        _rwkv_post_kernel,
        grid=(nb, nt_),
        in_specs=[hm_spec, ld_spec, ld_spec, const(vec.shape), const(e.shape)],
        out_specs=ld_spec,
        out_shape=jax.ShapeDtypeStruct((nb, t, gw), BF16),
        compiler_params=_cp("arbitrary", "arbitrary"),
        name="rwkv_post",
    )(o, g, bonus, vec, e)
    return ya, s_t


def _alibi_slope(h):
    return 2.0 ** (-8.0 * (h + 1) / N_HEADS)


LOG2E = 1.4426950408889634
MASKED = -1e30


def _eye(n):
    return (lax.broadcasted_iota(jnp.int32, (n, n), 0) == lax.broadcasted_iota(jnp.int32, (n, n), 1)).astype(BF16)


def _moba_prompt_kernel(q_ref, k_ref, v_ref, selm_ref, o_ref, kh_scr, vt_scr, km_scr, selt_scr, m_scr, acc_scr,
                        *, n_blocks):
    qi = pl.program_id(1)
    blk, nh, hd = MOBA_BLOCK, N_HEADS, HEAD_DIM
    nt_dims = (((1,), (1,)), ((), ()))

    @pl.when(qi == 0)
    def _():
        k = k_ref[...]
        v = v_ref[...]
        one_row = (lax.broadcasted_iota(jnp.int32, (hd, blk), 0) == 0).astype(BF16)
        eye_hd = _eye(hd)
        for h in range(nh):
            kh_scr[h] = k[:, h * hd:(h + 1) * hd].astype(BF16)
            vt = lax.dot_general(eye_hd, v[:, h * hd:(h + 1) * hd].astype(BF16), nt_dims,
                                 preferred_element_type=F32).astype(BF16)
            for n in range(n_blocks):
                vt_scr[h, n] = jnp.concatenate([vt[:, n * blk:(n + 1) * blk], one_row], axis=0)
        rows = [jnp.mean(k[n * blk:(n + 1) * blk, :], axis=0, keepdims=True) for n in range(n_blocks)]
        rows += [jnp.zeros((1, k.shape[1]), F32)] * (selm_ref.shape[0] - n_blocks)
        kmean = jnp.concatenate(rows, axis=0)
        tn_dims = (((0,), (0,)), ((), ()))
        spread = sum(lax.dot_general(p, selm_ref[...], tn_dims, preferred_element_type=F32)
                     for p in _split3(kmean))
        rh = lax.broadcasted_iota(jnp.int32, spread.shape, 0) // hd
        ch = lax.broadcasted_iota(jnp.int32, spread.shape, 1) % nh
        for i, p in enumerate(_split3(jnp.where(rh == ch, spread, 0.0))):
            km_scr[i] = p

    own = qi
    q = q_ref[...]
    q1, q2, q3 = _split3(q * (hd ** -0.5))
    k1, k2, k3 = km_scr[0], km_scr[1], km_scr[2]
    gate = (_dot(q1, k1) + _dot(q1, k2) + _dot(q2, k1)) + (_dot(q1, k3) + _dot(q2, k2) + _dot(q3, k1))
    lane_blk = lax.broadcasted_iota(jnp.int32, gate.shape, 1) // nh
    valid = lane_blk < own
    gate = jnp.where(valid, gate, -jnp.inf)
    rank = jnp.zeros_like(gate)
    for d in range(1, n_blocks):
        earlier = pltpu.roll(gate, nh * d, axis=1)
        later = pltpu.roll(gate, LANES - nh * d, axis=1)
        rank = rank + jnp.where(earlier >= gate, 1.0, 0.0) + jnp.where(later > gate, 1.0, 0.0)
    sel = jnp.where(valid & (rank < min(MOBA_TOPK, n_blocks - 1)), 1.0, 0.0)
    selt_scr[...] = lax.dot_general(_eye(LANES), sel.astype(BF16), nt_dims, preferred_element_type=F32)

    qa = q * ((hd ** -0.5) * LOG2E)
    qh = [qa[:, h * hd:(h + 1) * hd].astype(BF16) for h in range(nh)]
    ahead_i = (lax.broadcasted_iota(jnp.int32, (blk, blk), 1) - lax.broadcasted_iota(jnp.int32, (blk, blk), 0))
    causal = ahead_i >= 0
    ahead = ahead_i.astype(F32)

    start = pl.multiple_of(own * blk, blk)
    for h in range(nh):
        s = lax.dot_general(kh_scr[h, pl.ds(start, blk), :], qh[h], nt_dims, preferred_element_type=F32)
        tm = jnp.where(causal, s - (_alibi_slope(h) * LOG2E) * ahead, MASKED)
        m = jnp.max(tm, axis=0, keepdims=True)
        p = jnp.exp2(tm - m)
        m_scr[h] = m
        acc_scr[h] = _dot(vt_scr[h, own], p.astype(BF16))

    def past_block(n, carry):
        base = pl.multiple_of(n * blk, blk)
        blocks_ahead = ((own - n) * blk).astype(F32)
        for h in range(nh):
            s = lax.dot_general(kh_scr[h, pl.ds(base, blk), :], qh[h], nt_dims, preferred_element_type=F32)
            slope2 = _alibi_slope(h) * LOG2E
            picked = selt_scr[pl.ds(n * nh + h, 1), :] > 0.5
            tm = jnp.where(picked, s - slope2 * ahead, MASKED)
            shift = slope2 * blocks_ahead
            m_old = m_scr[h]
            m_new = jnp.maximum(m_old, jnp.max(tm, axis=0, keepdims=True) - shift)
            p = jnp.exp2(tm - (m_new + shift))
            m_scr[h] = m_new
            acc_scr[h] = jnp.exp2(m_old - m_new) * acc_scr[h] + _dot(vt_scr[h, n], p.astype(BF16))
        return carry

    lax.fori_loop(0, own, past_block, 0)
    outs = []
    eye_hd = _eye(hd)
    for h in range(nh):
        acc = acc_scr[h]
        out_t = (acc[0:hd, :] / acc[hd:hd + 1, :]).astype(BF16)
        outs.append(lax.dot_general(out_t, eye_hd, (((0,), (0,)), ((), ())), preferred_element_type=F32))
    o_ref[...] = jnp.concatenate(outs, axis=-1).astype(o_ref.dtype)


def _moba_prompt(proj, nb, t):
    blk, gw, nh, hd = MOBA_BLOCK, GROUP_WIDTH, N_HEADS, HEAD_DIM
    n_blocks = t // blk
    assert n_blocks * nh <= LANES // 2
    selm = (jnp.arange(8)[:, None] == jnp.arange(LANES)[None, :] // nh).astype(BF16)
    return pl.pallas_call(
        functools.partial(_moba_prompt_kernel, n_blocks=n_blocks),
        grid=(nb, n_blocks),
        in_specs=[pl.BlockSpec((blk, gw), lambda b, i: (b * n_blocks + i, CB_Q)),
                  pl.BlockSpec((t, gw), lambda b, i: (b, CB_K)),
                  pl.BlockSpec((t, gw), lambda b, i: (b, CB_V)),
                  pl.BlockSpec((8, LANES), lambda b, i: (0, 0))],
        out_specs=pl.BlockSpec((blk, gw), lambda b, i: (b * n_blocks + i, 0)),
        out_shape=jax.ShapeDtypeStruct((nb * t, gw), BF16),
        scratch_shapes=[pltpu.VMEM((nh, t, hd), BF16), pltpu.VMEM((nh, n_blocks, 2 * hd, blk), BF16),
                        pltpu.VMEM((3, gw, LANES), BF16), pltpu.VMEM((LANES, blk), F32),
                        pltpu.VMEM((nh, 1, blk), F32), pltpu.VMEM((nh, 2 * hd, blk), F32)],
        compiler_params=_cp("arbitrary", "arbitrary"),
        name="moba_prompt",
    )(proj, proj, proj, selm)


def _sgu_prompt_kernel(pu_ref, pv_ref, w_ref, bias_ref, ln_ref, o_ref):
    u = jax.nn.gelu(pu_ref[...])
    v = _layer_norm(jax.nn.gelu(pv_ref[...]), ln_ref[0:1, :], ln_ref[1:2, :]).astype(BF16)
    cl, hd = SG_CHUNK, HEAD_DIM
    ri = lax.broadcasted_iota(jnp.int32, (cl, cl), 0)
    ci = lax.broadcasted_iota(jnp.int32, (cl, cl), 1)
    ws = [jnp.where(ci <= ri, w_ref[h], 0.0).astype(BF16) for h in range(N_HEADS)]
    bias = bias_ref[...]
    for c in range(u.shape[0] // cl):
        vc = v[c * cl:(c + 1) * cl, :]
        mixed = jnp.concatenate([_dot(ws[h], vc[:, h * hd:(h + 1) * hd]) for h in range(N_HEADS)], axis=-1)
        o_ref[c * cl:(c + 1) * cl, :] = (u[c * cl:(c + 1) * cl, :] * (mixed + bias)).astype(o_ref.dtype)


def _sgu_prompt(proj, nrows, w, bias_exp, ln):
    tt, gw = 256, GROUP_WIDTH
    return pl.pallas_call(
        _sgu_prompt_kernel,
        grid=(nrows // tt,),
        in_specs=[pl.BlockSpec((tt, gw), lambda i: (i, CB_SU)),
                  pl.BlockSpec((tt, gw), lambda i: (i, CB_SV)),
                  pl.BlockSpec(w.shape, lambda i: (0, 0, 0)),
                  pl.BlockSpec(bias_exp.shape, lambda i: (0, 0)),
                  pl.BlockSpec(ln.shape, lambda i: (0, 0))],
        out_specs=pl.BlockSpec((tt, gw), lambda i: (i, 0)),
        out_shape=jax.ShapeDtypeStruct((nrows, gw), BF16),
        compiler_params=_cp("arbitrary"),
        name="sgu_prompt",
    )(proj, proj, w, bias_exp, ln)


def _conv_prompt_kernel(pa_ref, pg_ref, prev_ref, w_ref, vec_ref, o_ref, hist_ref, zf):
    ti = pl.program_id(1)
    tt = pa_ref.shape[0]
    halo = CONV_HALO

    @pl.when(ti == 0)
    def _():
        zf[0:halo, :] = prev_ref[0]

    z = pa_ref[...] * jax.nn.sigmoid(pg_ref[...])
    zf[halo:halo + tt, :] = z
    lead = halo - (CONV_WIDTH - 1)
    acc = jnp.zeros_like(z)
    for i in range(CONV_WIDTH):
        acc = acc + w_ref[i:i + 1, :] * zf[pl.ds(lead + i, tt), :]
    y = _layer_norm(acc + vec_ref[0:1, :], vec_ref[1:2, :], vec_ref[2:3, :])
    o_ref[...] = (y * jax.nn.sigmoid(y)).astype(o_ref.dtype)
    tail = zf[pl.ds(tt, halo), :]
    hist_ref[0] = tail
    zf[0:halo, :] = tail


def _conv_prompt(proj, nb, t, prev, w, vec):
    tt, gw = 256, GROUP_WIDTH
    nt_ = t // tt
    return pl.pallas_call(
        _conv_prompt_kernel,
        grid=(nb, nt_),
        in_specs=[pl.BlockSpec((tt, gw), lambda b, i: (b * nt_ + i, CB_CA)),
                  pl.BlockSpec((tt, gw), lambda b, i: (b * nt_ + i, CB_CG)),
                  pl.BlockSpec((1, CONV_HALO, gw), lambda b, i: (b, 0, 0)),
                  pl.BlockSpec(w.shape, lambda b, i: (0, 0)),
                  pl.BlockSpec(vec.shape, lambda b, i: (0, 0))],
        out_specs=[pl.BlockSpec((tt, gw), lambda b, i: (b * nt_ + i, 0)),
                   pl.BlockSpec((1, CONV_HALO, gw), lambda b, i: (b, 0, 0))],
        out_shape=[jax.ShapeDtypeStruct((nb * t, gw), BF16),
                   jax.ShapeDtypeStruct((nb, CONV_HALO, gw), F32)],
        scratch_shapes=[pltpu.VMEM((CONV_HALO + tt, gw), F32)],
        compiler_params=_cp("arbitrary", "arbitrary"),
        name="conv_prompt",
    )(proj, proj, prev, w, vec)


def _sample_elem_kernel(pr_ref, pk_ref, pv_ref, pl_ref, sp_ref, vec_ref, wup_ref, e_ref,
                        su_ref, sv_ref, sgp_ref, ca_ref, cg_ref, cprev_ref, cw_ref, cvec_ref,
                        r_ref, w_ref, k_ref, v_ref, kk_ref, b_ref, g_ref, bonus_ref,
                        yc_ref, sgv_ref, yd_ref, cnew_ref):
    gw = GROUP_WIDTH
    sp = sp_ref[...]
    r, lw, kmod, v, kk, b, g, bonus = _rwkv_elem(
        pr_ref[...], pk_ref[...], pv_ref[...], pl_ref[...],
        sp[:, 0:gw], sp[:, gw:2 * gw], sp[:, 2 * gw:3 * gw], sp[:, 3 * gw:3 * gw + LANES],
        vec_ref[...], wup_ref, e_ref[...], precise=True)
    for ref, val in ((r_ref, r), (w_ref, jnp.exp(lw)), (k_ref, kmod), (v_ref, v), (kk_ref, kk),
                     (b_ref, b), (g_ref, g), (bonus_ref, bonus)):
        ref[...] = val
    u = jax.nn.gelu(su_ref[...])
    sv = _layer_norm(jax.nn.gelu(sv_ref[...]), sgp_ref[0:1, :], sgp_ref[1:2, :])
    sgv_ref[...] = sv
    yc_ref[...] = (u * (sgp_ref[2:3, :] * sv + sgp_ref[3:4, :])).astype(yc_ref.dtype)
    z = ca_ref[...] * jax.nn.sigmoid(cg_ref[...])
    nprev = CONV_WIDTH - 1
    acc = cw_ref[nprev:nprev + 1, :] * z
    for i in range(nprev):
        acc = acc + cw_ref[i:i + 1, :] * cprev_ref[i]
        if i > 0:
            cnew_ref[i - 1] = cprev_ref[i]
    cnew_ref[nprev - 1] = z
    y = _layer_norm(acc + cvec_ref[0:1, :], cvec_ref[1:2, :], cvec_ref[2:3, :])
    yd_ref[...] = (y * jax.nn.sigmoid(y)).astype(yd_ref.dtype)


def _sample_wkv_kernel(r_ref, w_ref, k_ref, v_ref, kk_ref, b_ref, g_ref, bonus_ref, ln_ref, s_ref,
                       ya_ref, st_ref):
    hd = HEAD_DIM
    r, w, k, v, kk, b, g, bonus = (x[0] for x in (r_ref, w_ref, k_ref, v_ref, kk_ref, b_ref, g_ref, bonus_ref))
    s = s_ref[0]
    eye = (lax.broadcasted_iota(jnp.int32, (hd, hd), 0) == lax.broadcasted_iota(jnp.int32, (hd, hd), 1))
    eye = eye.astype(F32)[None]
    v_col = jnp.sum(eye * v, axis=2, keepdims=True)
    sa = jnp.sum(s * (-kk), axis=2, keepdims=True)
    s_new = s * w + sa * b + v_col * k
    st_ref[0] = s_new
    o_col = jnp.sum(s_new * r, axis=2, keepdims=True)
    o = jnp.sum(o_col * eye, axis=1, keepdims=True)
    mu = jnp.mean(o, axis=-1, keepdims=True)
    d = o - mu
    var = jnp.mean(d * d, axis=-1, keepdims=True)
    on = d * lax.rsqrt(var + GN_EPS) * ln_ref[0] + ln_ref[1]
    ya_ref[0] = (on + bonus) * g


def _sample_mixers(proj, shift_prev, s0, conv_prev_t, vec, wup, e, sgp, cw, cvec):
    nb = proj.shape[0]
    gw, nh, hd = GROUP_WIDTH, N_HEADS, HEAD_DIM
    col = lambda cb: pl.BlockSpec((nb, gw), lambda i: (0, cb))
    full = lambda x: pl.BlockSpec(x.shape, lambda i: (0,) * x.ndim)
    row_f32 = jax.ShapeDtypeStruct((nb, gw), F32)
    outs = pl.pallas_call(
        _sample_elem_kernel,
        grid=(1,),
        in_specs=[col(CB_R), col(CB_RK), col(CB_RV),
                  pl.BlockSpec((nb, LANES), lambda i: (0, CB_L * (gw // LANES))),
                  full(shift_prev), full(vec), full(wup), full(e),
                  col(CB_SU), col(CB_SV), full(sgp), col(CB_CA), col(CB_CG),
                  full(conv_prev_t), full(cw), full(cvec)],
        out_specs=[pl.BlockSpec((nb, gw), lambda i: (0, 0))] * 11 + [full(conv_prev_t)],
        out_shape=[row_f32] * 11 + [jax.ShapeDtypeStruct(conv_prev_t.shape, F32)],
        compiler_params=_cp("arbitrary"),
        name="sample_elem",
    )(proj, proj, proj, proj, shift_prev, vec, wup, e, proj, proj, sgp, proj, proj, conv_prev_t, cw, cvec)
    heads = [x.reshape(nb, nh, 1, hd) for x in outs[:8]]
    yc, sgv, yd, conv_new_t = outs[8:]
    ln = jnp.stack([vec[V_LNG].reshape(nh, 1, hd), vec[V_LNB].reshape(nh, 1, hd)])
    hspec = pl.BlockSpec((1, nh, 1, hd), lambda b: (b, 0, 0, 0))
    sspec = pl.BlockSpec((1, nh, hd, hd), lambda b: (b, 0, 0, 0))
    ya, s_new = pl.pallas_call(
        _sample_wkv_kernel,
        grid=(nb,),
        in_specs=[hspec] * 8 + [pl.BlockSpec(ln.shape, lambda b: (0, 0, 0, 0)), sspec],
        out_specs=[hspec, sspec],
        out_shape=[jax.ShapeDtypeStruct((nb, nh, 1, hd), F32), jax.ShapeDtypeStruct(s0.shape, F32)],
        compiler_params=_cp("arbitrary"),
        name="sample_wkv",
    )(*heads, ln, s0)
    return ya.reshape(nb, gw), s_new, yc, sgv, yd, conv_new_t


PAGES_PER_STEP = 16


def _moba_scores_kernel(pt_ref, q_ref, *refs):
    pages, o_ref = refs[:-1], refs[-1]
    ps = pages[0].shape[4]
    qb = jnp.broadcast_to(q_ref[0] * (HEAD_DIM ** -0.5), pages[0].shape[2:])
    for i, p_ref in enumerate(pages):
        rows = [jnp.sum(p_ref[0, 0, h] * qb[h], axis=0, keepdims=True) for h in range(N_HEADS)]
        o_ref[0, :, i * ps:(i + 1) * ps] = jnp.concatenate(rows, axis=0)


def _moba_select_kernel(s_ref, q_ref, kn_ref, eht_ref, bi_ref, slope_ref, p_ref, p0_ref, *, n_past_blocks):
    s = s_ref[0]
    nh, length = s.shape
    s0 = jnp.sum(eht_ref[...] * (q_ref[0] * (HEAD_DIM ** -0.5) * kn_ref[0]), axis=1, keepdims=True)
    bi = bi_ref[...]
    gate = sum(_dot(p, bi) for p in _split3(s)) * (1.0 / MOBA_BLOCK)
    lane = lax.broadcasted_iota(jnp.int32, gate.shape, 1)
    gate = jnp.where(lane < n_past_blocks, gate, -jnp.inf)
    sel = jnp.zeros(gate.shape, jnp.bool_)
    for _ in range(min(MOBA_TOPK, n_past_blocks)):
        m = jnp.max(gate, axis=1, keepdims=True)
        idx = jnp.min(jnp.where(gate == m, lane, LANES), axis=1, keepdims=True)
        pick = lane == idx
        sel = sel | pick
        gate = jnp.where(pick, -jnp.inf, gate)
    picked = lax.dot_general(sel.astype(BF16), bi, (((1,), (1,)), ((), ())),
                             preferred_element_type=F32) > 0.5
    pos = lax.broadcasted_iota(jnp.int32, s.shape, 1)
    dist = (length - pos).astype(F32)
    logit = jnp.where(picked, s - slope_ref[:, 0:1] * dist, -jnp.inf)
    m = jnp.maximum(jnp.max(logit, axis=1, keepdims=True), s0)
    p = jnp.exp(logit - m)
    p0 = jnp.exp(s0 - m)
    inv = 1.0 / (jnp.sum(p, axis=1, keepdims=True) + p0)
    p_ref[0] = p * inv
    p0_ref[0] = jnp.broadcast_to(p0 * inv, (nh, LANES))


def _moba_pv_kernel(pt_ref, p_ref, p0_ref, vn_ref, *refs, n_steps):
    pages, o_ref, acc = refs[:-2], refs[-2], refs[-1]
    j = pl.program_id(1)
    ps = pages[0].shape[4]

    @pl.when(j == 0)
    def _():
        acc[...] = jnp.zeros_like(acc)

    for h in range(N_HEADS):
        tot = acc[h]
        for i, v_ref in enumerate(pages):
            tot = tot + v_ref[0, 0, h] * p_ref[0, h:h + 1, i * ps:(i + 1) * ps]
        acc[h] = tot

    @pl.when(j == n_steps - 1)
    def _():
        for h in range(N_HEADS):
            o_ref[0, h] = (jnp.sum(acc[h], axis=1, keepdims=True)
                           + p0_ref[0, h:h + 1, 0:1] * vn_ref[0, h])


def _moba_sample(proj, cache_k, cache_v, layer, page_table):
    nb, n_pages = page_table.shape
    ps, gw, nh, hd = cache_k.shape[2], GROUP_WIDTH, N_HEADS, HEAD_DIM
    length = n_pages * ps
    assert length % MOBA_BLOCK == 0
    n_past_blocks = length // MOBA_BLOCK
    pp = min(PAGES_PER_STEP, n_pages)
    n_steps = n_pages // pp
    q, kn, vn = (proj[:, cb * gw:(cb + 1) * gw] for cb in (CB_Q, CB_K, CB_V))
    cache_kt = cache_k.transpose(0, 1, 3, 4, 2)
    cache_vt = cache_v.transpose(0, 1, 3, 4, 2)
    col4 = lambda x: x.reshape(nb, nh, hd, 1)
    col_spec = pl.BlockSpec((1, nh, hd, 1), lambda b, j, pt: (b, 0, 0, 0))

    def page_spec(i):
        return pl.BlockSpec((1, 1, nh, hd, ps), lambda b, j, pt: (layer, pt[b, j * pp + i], 0, 0, 0))

    scores = pl.pallas_call(
        _moba_scores_kernel,
        grid_spec=pltpu.PrefetchScalarGridSpec(
            num_scalar_prefetch=1,
            grid=(nb, n_steps),
            in_specs=[col_spec] + [page_spec(i) for i in range(pp)],
            out_specs=pl.BlockSpec((1, nh, pp * ps), lambda b, j, pt: (b, 0, j)),
        ),
        out_shape=jax.ShapeDtypeStruct((nb, nh, length), F32),
        compiler_params=_cp("arbitrary", "arbitrary"),
        name="moba_scores",
    )(page_table, col4(q), *([cache_kt] * pp))

    eht = (jnp.arange(nh)[:, None] == jnp.arange(gw)[None, :] // hd).astype(F32)
    bi = (jnp.arange(length)[:, None] // MOBA_BLOCK == jnp.arange(LANES)[None, :]).astype(BF16)
    slopes = jnp.broadcast_to(
        jnp.array([_alibi_slope(h) for h in range(nh)], F32)[:, None], (nh, LANES))
    row3 = pl.BlockSpec((1, 1, gw), lambda b: (b, 0, 0))
    p, p0 = pl.pallas_call(
        functools.partial(_moba_select_kernel, n_past_blocks=n_past_blocks),
        grid=(nb,),
        in_specs=[pl.BlockSpec((1, nh, length), lambda b: (b, 0, 0)), row3, row3,
                  pl.BlockSpec((nh, gw), lambda b: (0, 0)),
                  pl.BlockSpec((length, LANES), lambda b: (0, 0)),
                  pl.BlockSpec((nh, LANES), lambda b: (0, 0))],
        out_specs=[pl.BlockSpec((1, nh, length), lambda b: (b, 0, 0)),
                   pl.BlockSpec((1, nh, LANES), lambda b: (b, 0, 0))],
        out_shape=[jax.ShapeDtypeStruct((nb, nh, length), F32),
                   jax.ShapeDtypeStruct((nb, nh, LANES), F32)],
        compiler_params=_cp("arbitrary"),
        name="moba_select",
    )(scores, q.reshape(nb, 1, gw), kn.reshape(nb, 1, gw), eht, bi, slopes)

    out = pl.pallas_call(
        functools.partial(_moba_pv_kernel, n_steps=n_steps),
        grid_spec=pltpu.PrefetchScalarGridSpec(
            num_scalar_prefetch=1,
            grid=(nb, n_steps),
            in_specs=[pl.BlockSpec((1, nh, pp * ps), lambda b, j, pt: (b, 0, j)),
                      pl.BlockSpec((1, nh, LANES), lambda b, j, pt: (b, 0, 0)),
                      col_spec]
                     + [page_spec(i) for i in range(pp)],
            out_specs=col_spec,
            scratch_shapes=[pltpu.VMEM((nh, hd, ps), F32)],
        ),
        out_shape=jax.ShapeDtypeStruct((nb, nh, hd, 1), F32),
        compiler_params=_cp("arbitrary", "arbitrary"),
        name="moba_pv",
    )(page_table, p, p0, col4(vn), *([cache_vt] * pp))
    return out.reshape(nb, gw)


RWKV_PASSES = (1, 3)


def _shift_layout(shift):
    n = shift.shape[0]
    pad = jnp.zeros((n, LANES - 3 * TM_LORA), F32)
    return jnp.concatenate([shift, pad], axis=1)


def _layer_params(l, w_in, w_out, w_ffn_in, w_ffn_out, tm_mu, tm_w0, tm_w_up, tm_a0, tm_a_up, tm_g_up,
                  tm_k_k, tm_k_a, tm_r_k, tm_ln_g, tm_ln_b, sg_ws, sg_bs, sg_ln_g, sg_ln_b,
                  cv_w, cv_b, cv_ln_g, cv_ln_b):
    gw, lr, hd = GROUP_WIDTH, TM_LORA, HEAD_DIM
    wi = w_in[l]
    w_in_f32 = jnp.concatenate(
        [wi[:, A_PROJ:], wi[:, :A_PROJ], jnp.zeros((wi.shape[0], IN_PAD - IN_WIDTH), F32)], axis=1)
    w_in_p = w_in_f32.astype(BF16)
    mu = tm_mu[l]
    mu_l = jnp.concatenate([mu[3 * gw:], jnp.zeros((gw - 3 * lr,), F32)])
    vec = jnp.stack([mu[0:gw], mu[gw:2 * gw], mu[2 * gw:3 * gw], tm_w0[l], tm_a0[l], tm_k_k[l], tm_k_a[l],
                     tm_r_k[l].reshape(gw), tm_ln_g[l], tm_ln_b[l], mu_l]
                    + [jnp.zeros((gw,), F32)] * 5)
    wup = jnp.zeros((3, LANES, gw), F32)
    wup = wup.at[0, 0:lr].set(tm_w_up[l]).at[1, lr:2 * lr].set(tm_a_up[l]).at[2, 2 * lr:3 * lr].set(tm_g_up[l])
    sg_bias = jnp.repeat(sg_bs[l].T, hd, axis=1)
    sg_ln = jnp.stack([sg_ln_g[l], sg_ln_b[l]] + [jnp.zeros((gw,), F32)] * 6)
    sgp = jnp.stack([sg_ln_g[l], sg_ln_b[l], jnp.repeat(sg_ws[l][:, 0, 0], hd), jnp.repeat(sg_bs[l][:, 0], hd)]
                    + [jnp.zeros((gw,), F32)] * 4)
    cw = jnp.concatenate([cv_w[l], jnp.zeros((1, gw), F32)], axis=0)
    cvec = jnp.stack([cv_b[l], cv_ln_g[l], cv_ln_b[l]] + [jnp.zeros((gw,), F32)] * 5)
    return dict(w_in=w_in_p, w_out=w_out[l].astype(BF16), w_ffn_in=w_ffn_in[l].astype(BF16),
                w_ffn_out=w_ffn_out[l].astype(BF16), vec=vec, wup=wup, sg_w=sg_ws[l],
                sg_bias=sg_bias, sg_ln=sg_ln, sgp=sgp, cw=cw, cvec=cvec,
                w_in_f32=w_in_f32, dense_f32=(w_out, w_ffn_in, w_ffn_out))


def _dense_tail(x, ys, mod, p, g_ffn, tm, rows_per_group):
    ng, t, d = x.shape
    m = ng * t
    x1 = _mm(ys, p['w_out'], tm, min(1024, d), res=x.reshape(m, d), gate=mod, gate_col=2,
             rows_per_group=rows_per_group)
    h2 = _norm_mod(x1.reshape(ng, t, d), g_ffn, mod, 3, 4, min(512, t)).reshape(m, d)
    hid = _mm_swiglu(h2, p['w_ffn_in'], tm, 512)
    x2 = _mm([hid], p['w_ffn_out'], min(tm, 512), 512, res=x1, gate=mod, gate_col=5,
             rows_per_group=rows_per_group)
    return x2.reshape(ng, t, d)


def _prompt_layer(x, mod, p, g_mix, g_ffn, e):
    nb, t, d = x.shape
    m = nb * t
    gw, nh, hd = GROUP_WIDTH, N_HEADS, HEAD_DIM
    h = _norm_mod(x, g_mix, mod, 0, 1, 512).reshape(m, d)
    proj = _mm([h], p['w_in'], 1024, 512)
    shift0 = jnp.zeros((nb, 1, 3 * gw + LANES), F32)
    s0 = jnp.zeros((nb, nh, hd, hd), F32)
    ya, wkv = _rwkv_prompt(proj, nb, t, shift0, s0, p['vec'], p['wup'], e, RWKV_PASSES)
    yb = _moba_prompt(proj, nb, t)
    yc = _sgu_prompt(proj, m, p['sg_w'], p['sg_bias'], p['sg_ln'])
    yd, hist = _conv_prompt(proj, nb, t, jnp.zeros((nb, CONV_HALO, gw), F32), p['cw'], p['cvec'])
    x2 = _dense_tail(x, [ya.reshape(m, gw), yb, yc, yd], mod, p, g_ffn, 1024, t)
    proj3 = proj.reshape(nb, t, IN_PAD)
    last = proj3[:, t - 1]
    shift = jnp.concatenate([last[:, CB_R * gw:CB_L * gw], last[:, CB_L * gw:CB_L * gw + 3 * TM_LORA]], axis=1)
    new_k = proj3[:, :, CB_K * gw:(CB_K + 1) * gw].reshape(nb, t, nh, hd)
    new_v = proj3[:, :, CB_V * gw:(CB_V + 1) * gw].reshape(nb, t, nh, hd)
    return x2, (shift, wkv, new_k, new_v, hist[:, CONV_HALO - (CONV_WIDTH - 1):])


def _sample_layer(x, mod, p, g_mix, g_ffn, e, layer, shift_prev, wkv_prev, conv_prev, cache_k, cache_v,
                  page_table):
    nb, t, d = x.shape
    gw, nh, hd = GROUP_WIDTH, N_HEADS, HEAD_DIM
    xg = x.reshape(1, nb, d)
    h = _norm_mod(xg, g_mix, mod, 0, 1, nb, dtype=F32).reshape(nb, d)
    proj = _mm_precise([h], p['w_in_f32'][None], 0, 512)
    ya, wkv, yc, sgv, yd, conv_new_t = _sample_mixers(
        proj, _shift_layout(shift_prev), wkv_prev, conv_prev.transpose(1, 0, 2),
        p['vec'], p['wup'], e, p['sgp'], p['cw'], p['cvec'])
    yb = _moba_sample(proj, cache_k, cache_v, layer, page_table)
    w_out, w_ffn_in, w_ffn_out = p['dense_f32']
    x1 = _mm_precise([ya, yb, yc, yd], w_out, layer, 512, mode="gated", res=x.reshape(nb, d), gate=mod, gate_col=2)
    h2 = _norm_mod(x1.reshape(1, nb, d), g_ffn, mod, 3, 4, nb, dtype=F32).reshape(nb, d)
    hid = _mm_precise([h2], w_ffn_in, layer, 512, mode="swiglu")
    x2 = _mm_precise([hid], w_ffn_out, layer, 256, mode="gated", res=x1, gate=mod, gate_col=5)
    shift = jnp.concatenate([proj[:, CB_R * gw:CB_L * gw], proj[:, CB_L * gw:CB_L * gw + 3 * TM_LORA]], axis=1)
    new_k = proj[:, CB_K * gw:(CB_K + 1) * gw].reshape(nb, 1, nh, hd)
    new_v = proj[:, CB_V * gw:(CB_V + 1) * gw].reshape(nb, 1, nh, hd)
    return x2.reshape(nb, t, d), (shift, wkv, new_k, new_v, sgv.reshape(nb, 1, gw), conv_new_t.transpose(1, 0, 2))


def kernel(x_prompt, x_sample, c_prompt, c_sample, cache_k_b, cache_v_b, state_a_wkv, state_a_shift, state_d_conv, page_table, w_ada, b_ada, g_norm_mix, w_in, w_out, tm_mu, tm_w0, tm_w_up, tm_a0, tm_a_up, tm_g_up, tm_k_k, tm_k_a, tm_r_k, tm_ln_g, tm_ln_b, sg_ws, sg_bs, sg_ln_g, sg_ln_b, cv_w, cv_b, cv_ln_g, cv_ln_b, g_norm_ffn, w_ffn_in, w_ffn_out, g_norm_final):
    depth = w_in.shape[0]
    bp, ns = x_prompt.shape[0], x_sample.shape[0]
    gw = GROUP_WIDTH
    rows = bp + ns
    rows_pad = -(-rows // 8) * 8
    c_all = jnp.concatenate([c_prompt, c_sample, jnp.zeros((rows_pad - rows, c_prompt.shape[1]), F32)], axis=0)
    mod = _ada(c_all, w_ada, b_ada)
    e = _head_ones()
    xp, xs = x_prompt, x_sample
    st_p, st_s = [], []
    for l in range(depth):
        p = _layer_params(l, w_in, w_out, w_ffn_in, w_ffn_out, tm_mu, tm_w0, tm_w_up, tm_a0, tm_a_up, tm_g_up,
                          tm_k_k, tm_k_a, tm_r_k, tm_ln_g, tm_ln_b, sg_ws, sg_bs, sg_ln_g, sg_ln_b,
                          cv_w, cv_b, cv_ln_g, cv_ln_b)
        mod_p = mod[l, :bp][:, None, :]
        mod_s = mod[l, bp:rows][None]
        xp, sp = _prompt_layer(xp, mod_p, p, g_norm_mix[l], g_norm_ffn[l], e)
        xs, ss = _sample_layer(xs, mod_s, p, g_norm_mix[l], g_norm_ffn[l], e, l, state_a_shift[l],
                               state_a_wkv[l], state_d_conv[l], cache_k_b, cache_v_b, page_table)
        st_p.append(sp)
        st_s.append(ss)
    y_prompt = _norm(xp, g_norm_final, 512)
    y_sample = _norm(xs.reshape(1, ns, -1), g_norm_final, ns).reshape(xs.shape)
    stack = lambda sts, i: jnp.stack([s[i] for s in sts])
    return (y_prompt, y_sample, stack(st_p, 2), stack(st_p, 3), stack(st_s, 2), stack(st_s, 3),
            stack(st_p, 1), stack(st_s, 1), stack(st_p, 0), stack(st_s, 0), stack(st_p, 4), stack(st_s, 5),
            stack(st_s, 4))
```

```python
import functools

import jax
import jax.numpy as jnp
from jax import lax
from jax.experimental import pallas as pl
from jax.experimental.pallas import tpu as pltpu

F32, BF16 = jnp.float32, jnp.bfloat16

D_MODEL = 2048
GROUP_WIDTH = 512
N_HEADS = 8
HEAD_DIM = 64
TM_LORA = 32
A_PROJ = 3 * GROUP_WIDTH + 3 * TM_LORA
IN_WIDTH = A_PROJ + 7 * GROUP_WIDTH
IN_PAD = 11 * GROUP_WIDTH
MOBA_BLOCK = 256
MOBA_TOPK = 3
SG_CHUNK = 128
CONV_WIDTH = 31
CONV_HALO = 32
RMS_EPS = 1e-6
LN_EPS = 1e-5
GN_EPS = 64e-5
RWKV_CHUNK = 64
RWKV_CHUNKS_PER_STEP = 4
INV_BLOCK = 16
LANES = 128

CB_Q, CB_K, CB_V, CB_SU, CB_SV, CB_CA, CB_CG, CB_R, CB_RK, CB_RV, CB_L = range(11)
(V_MU_R, V_MU_K, V_MU_V, V_W0, V_A0, V_KK, V_KA, V_RK, V_LNG, V_LNB, V_MU_L) = range(11)

VMEM_LIMIT = 56 * 1024 * 1024


def _cp(*sem):
    return pltpu.CompilerParams(dimension_semantics=sem, vmem_limit_bytes=VMEM_LIMIT)


def _dot(a, b):
    return jnp.dot(a, b, preferred_element_type=F32)


def _split2(x):
    hi = x.astype(BF16)
    return hi, (x - hi.astype(F32)).astype(BF16)


def _split3(x):
    h1 = x.astype(BF16)
    r1 = x - h1.astype(F32)
    h2 = r1.astype(BF16)
    return h1, h2, (r1 - h2.astype(F32)).astype(BF16)


def _dot_exact_rhs(x, e):
    hi, lo = _split2(x)
    return _dot(hi, e) + _dot(lo, e)


def _head_ones():
    r = lax.broadcasted_iota(jnp.int32, (GROUP_WIDTH, GROUP_WIDTH), 0) // HEAD_DIM
    c = lax.broadcasted_iota(jnp.int32, (GROUP_WIDTH, GROUP_WIDTH), 1) // HEAD_DIM
    return (r == c).astype(BF16)


def _layer_norm(x, g, b):
    mu = jnp.mean(x, axis=-1, keepdims=True)
    d = x - mu
    var = jnp.mean(d * d, axis=-1, keepdims=True)
    return d * lax.rsqrt(var + LN_EPS) * g + b


def _softplus(x):
    return jnp.maximum(x, 0.0) + jnp.log1p(jnp.exp(-jnp.abs(x)))


def _dot3(x, w):
    xh, xl = _split2(x)
    wh, wl = _split2(w)
    return _dot(xh, wh) + _dot(xh, wl) + _dot(xl, wh)


def _ada_kernel(c_ref, w_ref, b_ref, o_ref):
    c = c_ref[...]
    o_ref[0] = _dot3(c * jax.nn.sigmoid(c), w_ref[0]) + b_ref[0]


def _ada(c_all, w_ada, b_ada):
    depth, d, n = w_ada.shape
    r = c_all.shape[0]
    tn = 512
    return pl.pallas_call(
        _ada_kernel,
        grid=(depth, n // tn),
        in_specs=[pl.BlockSpec((r, d), lambda l, j: (0, 0)),
                  pl.BlockSpec((1, d, tn), lambda l, j: (l, 0, j)),
                  pl.BlockSpec((1, 1, tn), lambda l, j: (l, 0, j))],
        out_specs=pl.BlockSpec((1, r, tn), lambda l, j: (l, 0, j)),
        out_shape=jax.ShapeDtypeStruct((depth, r, n), F32),
        compiler_params=_cp("arbitrary", "arbitrary"),
        name="ada",
    )(c_all, w_ada, b_ada.reshape(depth, 1, n))


def _norm_mod_kernel(x_ref, g_ref, sc_ref, sh_ref, o_ref):
    x = x_ref[0]
    h = x * lax.rsqrt(jnp.mean(x * x, axis=-1, keepdims=True) + RMS_EPS) * g_ref[...]
    o_ref[0] = (h * (1.0 + sc_ref[0]) + sh_ref[0]).astype(o_ref.dtype)


def _norm_mod(x, g, mod, sh_col, sc_col, tt, dtype=BF16):
    ng, t, d = x.shape
    r = mod.shape[1]
    return pl.pallas_call(
        _norm_mod_kernel,
        grid=(ng, t // tt),
        in_specs=[pl.BlockSpec((1, tt, d), lambda b, i: (b, i, 0)),
                  pl.BlockSpec((1, d), lambda b, i: (0, 0)),
                  pl.BlockSpec((1, r, d), lambda b, i: (b, 0, sc_col)),
                  pl.BlockSpec((1, r, d), lambda b, i: (b, 0, sh_col))],
        out_specs=pl.BlockSpec((1, tt, d), lambda b, i: (b, i, 0)),
        out_shape=jax.ShapeDtypeStruct((ng, t, d), dtype),
        compiler_params=_cp("arbitrary", "arbitrary"),
        name="norm_mod",
    )(x, g.reshape(1, d), mod, mod)


def _norm_kernel(x_ref, g_ref, o_ref):
    x = x_ref[0]
    o_ref[0] = x * lax.rsqrt(jnp.mean(x * x, axis=-1, keepdims=True) + RMS_EPS) * g_ref[...]


def _norm(x, g, tt):
    ng, t, d = x.shape
    return pl.pallas_call(
        _norm_kernel,
        grid=(ng, t // tt),
        in_specs=[pl.BlockSpec((1, tt, d), lambda b, i: (b, i, 0)),
                  pl.BlockSpec((1, d), lambda b, i: (0, 0))],
        out_specs=pl.BlockSpec((1, tt, d), lambda b, i: (b, i, 0)),
        out_shape=jax.ShapeDtypeStruct((ng, t, d), F32),
        compiler_params=_cp("arbitrary", "arbitrary"),
        name="norm_final",
    )(x, g.reshape(1, d))


def _mm_kernel(*refs, n_x, gated):
    xs, w_ref = refs[:n_x], refs[n_x]
    o_ref, wb = refs[-2], refs[-1]

    @pl.when(pl.program_id(1) == 0)
    def _():
        wb[...] = w_ref[0].astype(BF16)

    acc = None
    k0 = 0
    for x_ref in xs:
        k = x_ref.shape[1]
        part = _dot(x_ref[...], wb[k0:k0 + k, :])
        acc = part if acc is None else acc + part
        k0 += k
    if gated:
        res_ref, gt_ref = refs[n_x + 1], refs[n_x + 2]
        o_ref[...] = res_ref[...] + gt_ref[0] * acc
    else:
        o_ref[...] = acc


def _mm(xs, w, layer, tm, tn, res=None, gate=None, gate_col=0, rows_per_group=None):
    m = xs[0].shape[0]
    kdim, n = w.shape[1], w.shape[2]
    gated = res is not None
    in_specs = [pl.BlockSpec((tm, x.shape[1]), lambda j, i: (i, 0)) for x in xs]
    in_specs.append(pl.BlockSpec((1, kdim, tn), lambda j, i: (layer, 0, j)))
    args = list(xs) + [w]
    if gated:
        r = gate.shape[1]
        tiles_per_group = rows_per_group // tm
        nj = n // tn
        in_specs.append(pl.BlockSpec((tm, tn), lambda j, i: (i, j)))
        in_specs.append(pl.BlockSpec((1, r, tn), lambda j, i: (i // tiles_per_group, 0, gate_col * nj + j)))
        args += [res, gate]
    return pl.pallas_call(
        functools.partial(_mm_kernel, n_x=len(xs), gated=gated),
        grid=(n // tn, m // tm),
        in_specs=in_specs,
        out_specs=pl.BlockSpec((tm, tn), lambda j, i: (i, j)),
        out_shape=jax.ShapeDtypeStruct((m, n), F32),
        scratch_shapes=[pltpu.VMEM((kdim, tn), BF16)],
        compiler_params=_cp("arbitrary", "arbitrary"),
        name="mm_gated" if gated else "mm",
    )(*args)


def _mm_precise_kernel(*refs, n_x, mode):
    xs = refs[:n_x]
    o_ref = refs[-1]
    if mode == "swiglu":
        x = xs[0][...]
        g = _dot3(x, refs[n_x][0])
        o_ref[...] = g * jax.nn.sigmoid(g) * _dot3(x, refs[n_x + 1][0])
        return
    w_ref = refs[n_x]
    acc = None
    k0 = 0
    for x_ref in xs:
        k = x_ref.shape[1]
        part = _dot3(x_ref[...], w_ref[0, k0:k0 + k, :])
        acc = part if acc is None else acc + part
        k0 += k
    if mode == "gated":
        res_ref, gt_ref = refs[n_x + 1], refs[n_x + 2]
        o_ref[...] = res_ref[...] + gt_ref[0] * acc
    else:
        o_ref[...] = acc


def _mm_precise(xs, w, layer, tn, mode="plain", res=None, gate=None, gate_col=0):
    m = xs[0].shape[0]
    kdim = w.shape[1]
    n = w.shape[2] // 2 if mode == "swiglu" else w.shape[2]
    nj = n // tn
    in_specs = [pl.BlockSpec((m, x.shape[1]), lambda j: (0, 0)) for x in xs]
    in_specs.append(pl.BlockSpec((1, kdim, tn), lambda j: (layer, 0, j)))
    args = list(xs) + [w]
    if mode == "swiglu":
        in_specs.append(pl.BlockSpec((1, kdim, tn), lambda j: (layer, 0, nj + j)))
        args.append(w)
    if mode == "gated":
        in_specs.append(pl.BlockSpec((m, tn), lambda j: (0, j)))
        in_specs.append(pl.BlockSpec((1, m, tn), lambda j: (0, 0, gate_col * nj + j)))
        args += [res, gate]
    return pl.pallas_call(
        functools.partial(_mm_precise_kernel, n_x=len(xs), mode=mode),
        grid=(nj,),
        in_specs=in_specs,
        out_specs=pl.BlockSpec((m, tn), lambda j: (0, j)),
        out_shape=jax.ShapeDtypeStruct((m, n), F32),
        compiler_params=_cp("arbitrary"),
        name="mm_precise_" + mode,
    )(*args)


def _swiglu_kernel(x_ref, wg_ref, wu_ref, o_ref, wgb, wub):
    @pl.when(pl.program_id(1) == 0)
    def _():
        wgb[...] = wg_ref[0].astype(BF16)
        wub[...] = wu_ref[0].astype(BF16)

    x = x_ref[...]
    g = _dot(x, wgb[...])
    u = _dot(x, wub[...])
    o_ref[...] = (g * jax.nn.sigmoid(g) * u).astype(o_ref.dtype)


def _mm_swiglu(x, w, layer, tm, tn):
    m, k = x.shape
    nh = w.shape[2] // 2
    nj = nh // tn
    return pl.pallas_call(
        _swiglu_kernel,
        grid=(nj, m // tm),
        in_specs=[pl.BlockSpec((tm, k), lambda j, i: (i, 0)),
                  pl.BlockSpec((1, k, tn), lambda j, i: (layer, 0, j)),
                  pl.BlockSpec((1, k, tn), lambda j, i: (layer, 0, nj + j))],
        out_specs=pl.BlockSpec((tm, tn), lambda j, i: (i, j)),
        out_shape=jax.ShapeDtypeStruct((m, nh), BF16),
        scratch_shapes=[pltpu.VMEM((k, tn), BF16), pltpu.VMEM((k, tn), BF16)],
        compiler_params=_cp("arbitrary", "arbitrary"),
        name="mm_swiglu",
    )(x, w, w)


def _rwkv_elem(pr, pk, pv, plo, qr, qk, qv, qlo, vec, wup_ref, e, precise=False):
    row = lambda i: vec[i:i + 1, :]
    xr = pr + (qr - pr) * row(V_MU_R)
    xk = pk + (qk - pk) * row(V_MU_K)
    xv = pv + (qv - pv) * row(V_MU_V)
    xl = plo + (qlo - plo) * vec[V_MU_L:V_MU_L + 1, 0:LANES]
    up = _dot3 if precise else (lambda x, w: _dot(x.astype(BF16), w.astype(BF16)))
    lw = up(jnp.tanh(xl), wup_ref[0])
    la = up(xl, wup_ref[1])
    g = up(jax.nn.sigmoid(xl), wup_ref[2])
    w_log = -_softplus(-(row(V_W0) + lw)) - 0.5
    log_decay = -jnp.exp(w_log)
    a = jax.nn.sigmoid(row(V_A0) + la)
    kk = xk * row(V_KK)
    kk = kk / jnp.maximum(jnp.sqrt(_dot_exact_rhs(kk * kk, e)), 1e-12)
    kmod = xk * (1.0 + (a - 1.0) * row(V_KA))
    bonus = _dot_exact_rhs(xr * kmod * row(V_RK), e) * xv
    return xr, log_decay, kmod, xv, kk, kk * a, g, bonus


def _rwkv_prep(pr_ref, pk_ref, pv_ref, pl_ref, sp_ref, vec_ref, wup_ref, e_ref, head_refs, carry):
    rt_ref, at_ref, bt_ref, kt_ref, v_ref, bh_ref, kh_ref, gt_ref = head_refs
    ti = pl.program_id(1)

    @pl.when(ti == 0)
    def _():
        carry[...] = sp_ref[0]

    tt = pr_ref.shape[0]
    first = lax.broadcasted_iota(jnp.int32, (tt, 1), 0) == 0
    gw = GROUP_WIDTH
    cur = (pr_ref[...], pk_ref[...], pv_ref[...], pl_ref[...])
    offs = (0, gw, 2 * gw, 3 * gw)
    prev = []
    for p, o in zip(cur, offs):
        w = p.shape[1]
        prev.append(jnp.where(first, carry[:, o:o + w], pltpu.roll(p, 1, axis=0)))
        carry[:, o:o + w] = p[tt - 1:tt, :]
    vec = vec_ref[...]
    e = e_ref[...]
    r, lw, kmod, v, kk, b, g, bonus = _rwkv_elem(*cur, *prev, vec, wup_ref, e)

    ri = lax.broadcasted_iota(jnp.int32, (tt, tt), 0)
    ci = lax.broadcasted_iota(jnp.int32, (tt, tt), 1)
    same = (ri // RWKV_CHUNK) == (ci // RWKV_CHUNK)
    tri = (same & (ci <= ri)).astype(BF16)
    blk = same.astype(BF16)
    parts = _split3(lw)
    cum = sum(_dot(tri, p) for p in parts)
    tot = sum(_dot(blk, p) for p in parts)
    e_out = jnp.exp(-cum)
    to_end = jnp.exp(tot - cum)
    outs = ((rt_ref, r * jnp.exp(cum)), (at_ref, -kk * jnp.exp(cum - lw)), (bt_ref, b * e_out),
            (kt_ref, kmod * e_out), (v_ref, v), (bh_ref, b * to_end), (kh_ref, kmod * to_end),
            (gt_ref, jnp.exp(tot)))
    for ref, val in outs:
        for h in range(N_HEADS):
            ref[h] = val[:, h * HEAD_DIM:(h + 1) * HEAD_DIM]
    return g, bonus


def _bmm(a, b, ca, cb, passes):
    dn = (((ca,), (cb,)), ((0,), (0,)))
    d = lambda x, y: lax.dot_general(x, y, dn, preferred_element_type=F32)
    if passes == 1:
        return d(a.astype(BF16), b.astype(BF16))
    ah, al = _split2(a)
    bh, bl = _split2(b)
    return d(ah, bh) + d(ah, bl) + d(al, bh)


def _rwkv_chunks(head_refs, s0_ref, o_ref, st_ref, s_scr, passes):
    rt_ref, at_ref, bt_ref, kt_ref, v_ref, bh_ref, kh_ref, gt_ref = head_refs
    c = pl.program_id(1)

    @pl.when(c == 0)
    def _():
        s_scr[...] = s0_ref[0]

    n = RWKV_CHUNK
    shp = (N_HEADS, n, n)
    ti = lax.broadcasted_iota(jnp.int32, shp, 1)
    tj = lax.broadcasted_iota(jnp.int32, shp, 2)
    strict, incl = tj < ti, tj <= ti
    same = (ti // INV_BLOCK) == (tj // INV_BLOCK)
    nn = lambda a, b, p: _bmm(a, b, 2, 1, p)
    nt = lambda a, b, p: _bmm(a, b, 2, 2, p)
    tn = lambda a, b, p: _bmm(a, b, 1, 1, p)
    p_gen, p_chain = passes

    def chunk_terms(i):
        sl = slice(i * n, (i + 1) * n)
        rt, at, bt, kt, v, bh, kh = (r[:, sl, :] for r in (rt_ref, at_ref, bt_ref, kt_ref, v_ref, bh_ref, kh_ref))
        g_end = gt_ref[:, i * n:i * n + 1, :]
        big = nt(jnp.concatenate([at, rt], axis=1), jnp.concatenate([bt, kt], axis=1), p_gen)
        a_ab = jnp.where(strict, big[:, :n, :n], 0.0)
        a_ak = jnp.where(strict, big[:, :n, n:], 0.0)
        a_rb = jnp.where(incl, big[:, n:, :n], 0.0)
        a_rk = jnp.where(incl, big[:, n:, n:], 0.0)
        av = nn(jnp.concatenate([a_ak, a_rk], axis=1), v, p_gen)
        nd = jnp.where(same, a_ab, 0.0)
        no = jnp.where(same, 0.0, a_ab)
        nd2 = nn(nd, nd, p_gen)
        nd4 = nn(nd2, nd2, p_gen)
        nd8 = nn(nd4, nd4, p_gen)
        y = jnp.concatenate([no, at, av[:, :n]], axis=2)
        for pw in (nd8, nd4, nd2, nd):
            y = y + nn(pw, y, p_gen)
        m = y[:, :, :n]
        z = y[:, :, n:]
        m2 = nn(m, m, p_gen)
        z = z + nn(m2, z, p_gen)
        z = z + nn(m, z, p_gen)
        qo = jnp.concatenate([rt, av[:, n:]], axis=2) + nn(a_rb, z, p_gen)
        pg = tn(z, bh, p_gen)
        gm = pg[:, n:] + tn(v, kh, p_gen)
        return qo, pg[:, :n], gm, g_end

    terms = [chunk_terms(i) for i in range(rt_ref.shape[1] // n)]
    s = s_scr[...]
    for i, (qo, pmat, gm, g_end) in enumerate(terms):
        o_ref[:, i * n:(i + 1) * n, :] = nt(qo[:, :, :HEAD_DIM], s, p_chain) + qo[:, :, HEAD_DIM:]
        s = s * g_end + nn(s, pmat, p_chain) + gm
    s_scr[...] = s

    @pl.when(c == pl.num_programs(1) - 1)
    def _():
        st_ref[0] = s


def _rwkv_prompt_kernel(pr_ref, pk_ref, pv_ref, pl_ref, sp_ref, vec_ref, wup_ref, e_ref, s0_ref,
                        ya_ref, st_ref, carry, s_scr, o_scr, *head_scr, passes):
    g, bonus = _rwkv_prep(pr_ref, pk_ref, pv_ref, pl_ref, sp_ref, vec_ref, wup_ref, e_ref, head_scr, carry)
    _rwkv_chunks(head_scr, s0_ref, o_scr, st_ref, s_scr, passes)
    o = jnp.concatenate([o_scr[h] for h in range(N_HEADS)], axis=-1)
    e = e_ref[...]
    vec = vec_ref[...]
    mu = _dot_exact_rhs(o, e) * (1.0 / HEAD_DIM)
    d = o - mu
    var = _dot_exact_rhs(d * d, e) * (1.0 / HEAD_DIM)
    on = d * lax.rsqrt(var + GN_EPS) * vec[V_LNG:V_LNG + 1, :] + vec[V_LNB:V_LNB + 1, :]
    ya_ref[0] = ((on + bonus) * g).astype(ya_ref.dtype)


def _rwkv_prompt(proj, nb, t, shift_prev, s0, vec, wup, e, passes):
    tt = RWKV_CHUNKS_PER_STEP * RWKV_CHUNK
    nt_ = t // tt
    gw, nh, hd = GROUP_WIDTH, N_HEADS, HEAD_DIM
    pspec = lambda cb: pl.BlockSpec((tt, gw), lambda b, i: (b * nt_ + i, cb))
    const = lambda shape: pl.BlockSpec(shape, lambda b, i: (0,) * len(shape))
    sspec = pl.BlockSpec((1, nh, hd, hd), lambda b, i: (b, 0, 0, 0))
    head_major = pltpu.VMEM((nh, tt, hd), F32)
    return pl.pallas_call(
        functools.partial(_rwkv_prompt_kernel, passes=passes),
        grid=(nb, nt_),
        in_specs=[pspec(CB_R), pspec(CB_RK), pspec(CB_RV),
                  pl.BlockSpec((tt, LANES), lambda b, i: (b * nt_ + i, CB_L * (gw // LANES))),
                  pl.BlockSpec((1, 1, 3 * gw + LANES), lambda b, i: (b, 0, 0)),
                  const(vec.shape), const(wup.shape), const(e.shape), sspec],
        out_specs=[pl.BlockSpec((1, tt, gw), lambda b, i: (b, i, 0)), sspec],
        out_shape=[jax.ShapeDtypeStruct((nb, t, gw), BF16), jax.ShapeDtypeStruct((nb, nh, hd, hd), F32)],
        scratch_shapes=[pltpu.VMEM((1, 3 * gw + LANES), F32), pltpu.VMEM((nh, hd, hd), F32), head_major]
                       + [head_major] * 8,
        compiler_params=_cp("arbitrary", "arbitrary"),
        name="rwkv_prompt",
    )(proj, proj, proj, proj, shift_prev, vec, wup, e, s0)


def _alibi_slope(h):
    return 2.0 ** (-8.0 * (h + 1) / N_HEADS)


LOG2E = 1.4426950408889634
MASKED = -1e30


def _eye(n):
    return (lax.broadcasted_iota(jnp.int32, (n, n), 0) == lax.broadcasted_iota(jnp.int32, (n, n), 1)).astype(BF16)


def _moba_prompt_kernel(q_ref, k_ref, v_ref, selm_ref, o_ref, kh_scr, vt_scr, km_scr, selt_scr, m_scr, acc_scr,
                        *, n_blocks):
    qi = pl.program_id(1)
    blk, nh, hd = MOBA_BLOCK, N_HEADS, HEAD_DIM
    nt_dims = (((1,), (1,)), ((), ()))

    @pl.when(qi == 0)
    def _():
        k = k_ref[...]
        v = v_ref[...]
        one_row = (lax.broadcasted_iota(jnp.int32, (hd, blk), 0) == 0).astype(BF16)
        eye_hd = _eye(hd)
        for h in range(nh):
            kh_scr[h] = k[:, h * hd:(h + 1) * hd].astype(BF16)
            vt = lax.dot_general(eye_hd, v[:, h * hd:(h + 1) * hd].astype(BF16), nt_dims,
                                 preferred_element_type=F32).astype(BF16)
            for n in range(n_blocks):
                vt_scr[h, n] = jnp.concatenate([vt[:, n * blk:(n + 1) * blk], one_row], axis=0)
        rows = [jnp.mean(k[n * blk:(n + 1) * blk, :], axis=0, keepdims=True) for n in range(n_blocks)]
        rows += [jnp.zeros((1, k.shape[1]), F32)] * (selm_ref.shape[0] - n_blocks)
        kmean = jnp.concatenate(rows, axis=0)
        tn_dims = (((0,), (0,)), ((), ()))
        spread = sum(lax.dot_general(p, selm_ref[...], tn_dims, preferred_element_type=F32)
                     for p in _split3(kmean))
        rh = lax.broadcasted_iota(jnp.int32, spread.shape, 0) // hd
        ch = lax.broadcasted_iota(jnp.int32, spread.shape, 1) % nh
        for i, p in enumerate(_split3(jnp.where(rh == ch, spread, 0.0))):
            km_scr[i] = p

    own = qi
    q = q_ref[...]
    q1, q2, q3 = _split3(q * (hd ** -0.5))
    k1, k2, k3 = km_scr[0], km_scr[1], km_scr[2]
    gate = (_dot(q1, k1) + _dot(q1, k2) + _dot(q2, k1)) + (_dot(q1, k3) + _dot(q2, k2) + _dot(q3, k1))
    lane_blk = lax.broadcasted_iota(jnp.int32, gate.shape, 1) // nh
    valid = lane_blk < own
    gate = jnp.where(valid, gate, -jnp.inf)
    rank = jnp.zeros_like(gate)
    for d in range(1, n_blocks):
        earlier = pltpu.roll(gate, nh * d, axis=1)
        later = pltpu.roll(gate, LANES - nh * d, axis=1)
        rank = rank + jnp.where(earlier >= gate, 1.0, 0.0) + jnp.where(later > gate, 1.0, 0.0)
    sel = jnp.where(valid & (rank < min(MOBA_TOPK, n_blocks - 1)), 1.0, 0.0)
    selt_scr[...] = lax.dot_general(_eye(LANES), sel.astype(BF16), nt_dims, preferred_element_type=F32)

    qa = q * ((hd ** -0.5) * LOG2E)
    qh = [qa[:, h * hd:(h + 1) * hd].astype(BF16) for h in range(nh)]
    ahead_i = (lax.broadcasted_iota(jnp.int32, (blk, blk), 1) - lax.broadcasted_iota(jnp.int32, (blk, blk), 0))
    causal = ahead_i >= 0
    ahead = ahead_i.astype(F32)

    start = pl.multiple_of(own * blk, blk)
    for h in range(nh):
        s = lax.dot_general(kh_scr[h, pl.ds(start, blk), :], qh[h], nt_dims, preferred_element_type=F32)
        tm = jnp.where(causal, s - (_alibi_slope(h) * LOG2E) * ahead, MASKED)
        m = jnp.max(tm, axis=0, keepdims=True)
        p = jnp.exp2(tm - m)
        m_scr[h] = m
        acc_scr[h] = _dot(vt_scr[h, own], p.astype(BF16))

    def past_block(n, carry):
        base = pl.multiple_of(n * blk, blk)
        blocks_ahead = ((own - n) * blk).astype(F32)
        for h in range(nh):
            s = lax.dot_general(kh_scr[h, pl.ds(base, blk), :], qh[h], nt_dims, preferred_element_type=F32)
            slope2 = _alibi_slope(h) * LOG2E
            picked = selt_scr[pl.ds(n * nh + h, 1), :] > 0.5
            tm = jnp.where(picked, s - slope2 * ahead, MASKED)
            shift = slope2 * blocks_ahead
            m_old = m_scr[h]
            m_new = jnp.maximum(m_old, jnp.max(tm, axis=0, keepdims=True) - shift)
            p = jnp.exp2(tm - (m_new + shift))
            m_scr[h] = m_new
            acc_scr[h] = jnp.exp2(m_old - m_new) * acc_scr[h] + _dot(vt_scr[h, n], p.astype(BF16))
        return carry

    lax.fori_loop(0, own, past_block, 0)
    outs = []
    eye_hd = _eye(hd)
    for h in range(nh):
        acc = acc_scr[h]
        out_t = (acc[0:hd, :] / acc[hd:hd + 1, :]).astype(BF16)
        outs.append(lax.dot_general(out_t, eye_hd, (((0,), (0,)), ((), ())), preferred_element_type=F32))
    o_ref[...] = jnp.concatenate(outs, axis=-1).astype(o_ref.dtype)


def _moba_prompt(proj, nb, t):
    blk, gw, nh, hd = MOBA_BLOCK, GROUP_WIDTH, N_HEADS, HEAD_DIM
    n_blocks = t // blk
    assert n_blocks * nh <= LANES // 2
    selm = (jnp.arange(8)[:, None] == jnp.arange(LANES)[None, :] // nh).astype(BF16)
    return pl.pallas_call(
        functools.partial(_moba_prompt_kernel, n_blocks=n_blocks),
        grid=(nb, n_blocks),
        in_specs=[pl.BlockSpec((blk, gw), lambda b, i: (b * n_blocks + i, CB_Q)),
                  pl.BlockSpec((t, gw), lambda b, i: (b, CB_K)),
                  pl.BlockSpec((t, gw), lambda b, i: (b, CB_V)),
                  pl.BlockSpec((8, LANES), lambda b, i: (0, 0))],
        out_specs=pl.BlockSpec((blk, gw), lambda b, i: (b * n_blocks + i, 0)),
        out_shape=jax.ShapeDtypeStruct((nb * t, gw), BF16),
        scratch_shapes=[pltpu.VMEM((nh, t, hd), BF16), pltpu.VMEM((nh, n_blocks, 2 * hd, blk), BF16),
                        pltpu.VMEM((3, gw, LANES), BF16), pltpu.VMEM((LANES, blk), F32),
                        pltpu.VMEM((nh, 1, blk), F32), pltpu.VMEM((nh, 2 * hd, blk), F32)],
        compiler_params=_cp("arbitrary", "arbitrary"),
        name="moba_prompt",
    )(proj, proj, proj, selm)


def _sgu_prompt_kernel(pu_ref, pv_ref, w_ref, bias_ref, ln_ref, o_ref):
    u = jax.nn.gelu(pu_ref[...])
    v = _layer_norm(jax.nn.gelu(pv_ref[...]), ln_ref[0:1, :], ln_ref[1:2, :]).astype(BF16)
    cl, hd = SG_CHUNK, HEAD_DIM
    ri = lax.broadcasted_iota(jnp.int32, (cl, cl), 0)
    ci = lax.broadcasted_iota(jnp.int32, (cl, cl), 1)
    ws = [jnp.where(ci <= ri, w_ref[h], 0.0).astype(BF16) for h in range(N_HEADS)]
    bias = bias_ref[...]
    for c in range(u.shape[0] // cl):
        vc = v[c * cl:(c + 1) * cl, :]
        mixed = jnp.concatenate([_dot(ws[h], vc[:, h * hd:(h + 1) * hd]) for h in range(N_HEADS)], axis=-1)
        o_ref[c * cl:(c + 1) * cl, :] = (u[c * cl:(c + 1) * cl, :] * (mixed + bias)).astype(o_ref.dtype)


def _sgu_prompt(proj, nrows, w, bias_exp, ln):
    tt, gw = 256, GROUP_WIDTH
    return pl.pallas_call(
        _sgu_prompt_kernel,
        grid=(nrows // tt,),
        in_specs=[pl.BlockSpec((tt, gw), lambda i: (i, CB_SU)),
                  pl.BlockSpec((tt, gw), lambda i: (i, CB_SV)),
                  pl.BlockSpec(w.shape, lambda i: (0, 0, 0)),
                  pl.BlockSpec(bias_exp.shape, lambda i: (0, 0)),
                  pl.BlockSpec(ln.shape, lambda i: (0, 0))],
        out_specs=pl.BlockSpec((tt, gw), lambda i: (i, 0)),
        out_shape=jax.ShapeDtypeStruct((nrows, gw), BF16),
        compiler_params=_cp("arbitrary"),
        name="sgu_prompt",
    )(proj, proj, w, bias_exp, ln)


def _conv_prompt_kernel(pa_ref, pg_ref, prev_ref, w_ref, vec_ref, o_ref, hist_ref, zf):
    ti = pl.program_id(1)
    tt = pa_ref.shape[0]
    halo = CONV_HALO

    @pl.when(ti == 0)
    def _():
        zf[0:halo, :] = prev_ref[0]

    z = pa_ref[...] * jax.nn.sigmoid(pg_ref[...])
    zf[halo:halo + tt, :] = z
    lead = halo - (CONV_WIDTH - 1)
    acc = jnp.zeros_like(z)
    for i in range(CONV_WIDTH):
        acc = acc + w_ref[i:i + 1, :] * zf[pl.ds(lead + i, tt), :]
    y = _layer_norm(acc + vec_ref[0:1, :], vec_ref[1:2, :], vec_ref[2:3, :])
    o_ref[...] = (y * jax.nn.sigmoid(y)).astype(o_ref.dtype)
    tail = zf[pl.ds(tt, halo), :]
    hist_ref[0] = tail
    zf[0:halo, :] = tail


def _conv_prompt(proj, nb, t, prev, w, vec):
    tt, gw = 256, GROUP_WIDTH
    nt_ = t // tt
    return pl.pallas_call(
        _conv_prompt_kernel,
        grid=(nb, nt_),
        in_specs=[pl.BlockSpec((tt, gw), lambda b, i: (b * nt_ + i, CB_CA)),
                  pl.BlockSpec((tt, gw), lambda b, i: (b * nt_ + i, CB_CG)),
                  pl.BlockSpec((1, CONV_HALO, gw), lambda b, i: (b, 0, 0)),
                  pl.BlockSpec(w.shape, lambda b, i: (0, 0)),
                  pl.BlockSpec(vec.shape, lambda b, i: (0, 0))],
        out_specs=[pl.BlockSpec((tt, gw), lambda b, i: (b * nt_ + i, 0)),
                   pl.BlockSpec((1, CONV_HALO, gw), lambda b, i: (b, 0, 0))],
        out_shape=[jax.ShapeDtypeStruct((nb * t, gw), BF16),
                   jax.ShapeDtypeStruct((nb, CONV_HALO, gw), F32)],
        scratch_shapes=[pltpu.VMEM((CONV_HALO + tt, gw), F32)],
        compiler_params=_cp("arbitrary", "arbitrary"),
        name="conv_prompt",
    )(proj, proj, prev, w, vec)


def _sample_elem_kernel(pr_ref, pk_ref, pv_ref, pl_ref, sp_ref, vec_ref, wup_ref, e_ref,
                        su_ref, sv_ref, sgp_ref, ca_ref, cg_ref, cprev_ref, cw_ref, cvec_ref,
                        r_ref, w_ref, k_ref, v_ref, kk_ref, b_ref, g_ref, bonus_ref,
                        yc_ref, sgv_ref, yd_ref, cnew_ref):
    gw = GROUP_WIDTH
    sp = sp_ref[...]
    r, lw, kmod, v, kk, b, g, bonus = _rwkv_elem(
        pr_ref[...], pk_ref[...], pv_ref[...], pl_ref[...],
        sp[:, 0:gw], sp[:, gw:2 * gw], sp[:, 2 * gw:3 * gw], sp[:, 3 * gw:3 * gw + LANES],
        vec_ref[...], wup_ref, e_ref[...], precise=True)
    for ref, val in ((r_ref, r), (w_ref, jnp.exp(lw)), (k_ref, kmod), (v_ref, v), (kk_ref, kk),
                     (b_ref, b), (g_ref, g), (bonus_ref, bonus)):
        ref[...] = val
    u = jax.nn.gelu(su_ref[...])
    sv = _layer_norm(jax.nn.gelu(sv_ref[...]), sgp_ref[0:1, :], sgp_ref[1:2, :])
    sgv_ref[...] = sv
    yc_ref[...] = (u * (sgp_ref[2:3, :] * sv + sgp_ref[3:4, :])).astype(yc_ref.dtype)
    z = ca_ref[...] * jax.nn.sigmoid(cg_ref[...])
    nprev = CONV_WIDTH - 1
    acc = cw_ref[nprev:nprev + 1, :] * z
    for i in range(nprev):
        acc = acc + cw_ref[i:i + 1, :] * cprev_ref[i]
        if i > 0:
            cnew_ref[i - 1] = cprev_ref[i]
    cnew_ref[nprev - 1] = z
    y = _layer_norm(acc + cvec_ref[0:1, :], cvec_ref[1:2, :], cvec_ref[2:3, :])
    yd_ref[...] = (y * jax.nn.sigmoid(y)).astype(yd_ref.dtype)


def _sample_wkv_kernel(r_ref, w_ref, k_ref, v_ref, kk_ref, b_ref, g_ref, bonus_ref, ln_ref, s_ref,
                       ya_ref, st_ref):
    hd = HEAD_DIM
    r, w, k, v, kk, b, g, bonus = (x[0] for x in (r_ref, w_ref, k_ref, v_ref, kk_ref, b_ref, g_ref, bonus_ref))
    s = s_ref[0]
    eye = (lax.broadcasted_iota(jnp.int32, (hd, hd), 0) == lax.broadcasted_iota(jnp.int32, (hd, hd), 1))
    eye = eye.astype(F32)[None]
    v_col = jnp.sum(eye * v, axis=2, keepdims=True)
    sa = jnp.sum(s * (-kk), axis=2, keepdims=True)
    s_new = s * w + sa * b + v_col * k
    st_ref[0] = s_new
    o_col = jnp.sum(s_new * r, axis=2, keepdims=True)
    o = jnp.sum(o_col * eye, axis=1, keepdims=True)
    mu = jnp.mean(o, axis=-1, keepdims=True)
    d = o - mu
    var = jnp.mean(d * d, axis=-1, keepdims=True)
    on = d * lax.rsqrt(var + GN_EPS) * ln_ref[0] + ln_ref[1]
    ya_ref[0] = (on + bonus) * g


def _sample_mixers(proj, shift_prev, s0, conv_prev_t, vec, wup, e, sgp, cw, cvec):
    nb = proj.shape[0]
    gw, nh, hd = GROUP_WIDTH, N_HEADS, HEAD_DIM
    col = lambda cb: pl.BlockSpec((nb, gw), lambda i: (0, cb))
    full = lambda x: pl.BlockSpec(x.shape, lambda i: (0,) * x.ndim)
    row_f32 = jax.ShapeDtypeStruct((nb, gw), F32)
    outs = pl.pallas_call(
        _sample_elem_kernel,
        grid=(1,),
        in_specs=[col(CB_R), col(CB_RK), col(CB_RV),
                  pl.BlockSpec((nb, LANES), lambda i: (0, CB_L * (gw // LANES))),
                  full(shift_prev), full(vec), full(wup), full(e),
                  col(CB_SU), col(CB_SV), full(sgp), col(CB_CA), col(CB_CG),
                  full(conv_prev_t), full(cw), full(cvec)],
        out_specs=[pl.BlockSpec((nb, gw), lambda i: (0, 0))] * 11 + [full(conv_prev_t)],
        out_shape=[row_f32] * 11 + [jax.ShapeDtypeStruct(conv_prev_t.shape, F32)],
        compiler_params=_cp("arbitrary"),
        name="sample_elem",
    )(proj, proj, proj, proj, shift_prev, vec, wup, e, proj, proj, sgp, proj, proj, conv_prev_t, cw, cvec)
    heads = [x.reshape(nb, nh, 1, hd) for x in outs[:8]]
    yc, sgv, yd, conv_new_t = outs[8:]
    ln = jnp.stack([vec[V_LNG].reshape(nh, 1, hd), vec[V_LNB].reshape(nh, 1, hd)])
    hspec = pl.BlockSpec((1, nh, 1, hd), lambda b: (b, 0, 0, 0))
    sspec = pl.BlockSpec((1, nh, hd, hd), lambda b: (b, 0, 0, 0))
    ya, s_new = pl.pallas_call(
        _sample_wkv_kernel,
        grid=(nb,),
        in_specs=[hspec] * 8 + [pl.BlockSpec(ln.shape, lambda b: (0, 0, 0, 0)), sspec],
        out_specs=[hspec, sspec],
        out_shape=[jax.ShapeDtypeStruct((nb, nh, 1, hd), F32), jax.ShapeDtypeStruct(s0.shape, F32)],
        compiler_params=_cp("arbitrary"),
        name="sample_wkv",
    )(*heads, ln, s0)
    return ya.reshape(nb, gw), s_new, yc, sgv, yd, conv_new_t


PAGES_PER_STEP = 16


def _moba_scores_kernel(pt_ref, q_ref, *refs):
    pages, o_ref = refs[:-1], refs[-1]
    ps = pages[0].shape[4]
    qb = jnp.broadcast_to(q_ref[0] * (HEAD_DIM ** -0.5), pages[0].shape[2:])
    for i, p_ref in enumerate(pages):
        rows = [jnp.sum(p_ref[0, 0, h] * qb[h], axis=0, keepdims=True) for h in range(N_HEADS)]
        o_ref[0, :, i * ps:(i + 1) * ps] = jnp.concatenate(rows, axis=0)


def _moba_select(s, q, kn, eht, bi, slope, n_past_blocks):
    nh, length = s.shape
    s0 = jnp.sum(eht * (q * (HEAD_DIM ** -0.5) * kn), axis=1, keepdims=True)
    gate = sum(_dot(p, bi) for p in _split3(s)) * (1.0 / MOBA_BLOCK)
    lane = lax.broadcasted_iota(jnp.int32, gate.shape, 1)
    gate = jnp.where(lane < n_past_blocks, gate, -jnp.inf)
    sel = jnp.zeros(gate.shape, jnp.bool_)
    for _ in range(min(MOBA_TOPK, n_past_blocks)):
        m = jnp.max(gate, axis=1, keepdims=True)
        idx = jnp.min(jnp.where(gate == m, lane, LANES), axis=1, keepdims=True)
        pick = lane == idx
        sel = sel | pick
        gate = jnp.where(pick, -jnp.inf, gate)
    picked = lax.dot_general(sel.astype(BF16), bi, (((1,), (1,)), ((), ())),
                             preferred_element_type=F32) > 0.5
    pos = lax.broadcasted_iota(jnp.int32, s.shape, 1)
    dist = (length - pos).astype(F32)
    logit = jnp.where(picked, s - slope[:, 0:1] * dist, -jnp.inf)
    m = jnp.maximum(jnp.max(logit, axis=1, keepdims=True), s0)
    p = jnp.exp(logit - m)
    p0 = jnp.exp(s0 - m)
    inv = 1.0 / (jnp.sum(p, axis=1, keepdims=True) + p0)
    return p * inv, p0 * inv


def _moba_pv_kernel(pt_ref, s_ref, q_ref, kn_ref, eht_ref, bi_ref, slope_ref, vn_ref, *refs,
                    n_steps, n_past_blocks):
    pages, o_ref, acc, p_scr, p0_scr = refs[:-4], refs[-4], refs[-3], refs[-2], refs[-1]
    j = pl.program_id(1)
    ps = pages[0].shape[4]
    w = len(pages) * ps

    @pl.when(j == 0)
    def _():
        acc[...] = jnp.zeros_like(acc)
        p, p0 = _moba_select(s_ref[0], q_ref[0], kn_ref[0], eht_ref[...], bi_ref[...], slope_ref[...],
                             n_past_blocks)
        for jj in range(n_steps):
            p_scr[jj] = p[:, jj * w:(jj + 1) * w]
        p0_scr[...] = jnp.broadcast_to(p0, p0_scr.shape)

    p_blk = p_scr[j]
    for h in range(N_HEADS):
        tot = acc[h]
        for i, v_ref in enumerate(pages):
            tot = tot + v_ref[0, 0, h] * p_blk[h:h + 1, i * ps:(i + 1) * ps]
        acc[h] = tot

    @pl.when(j == n_steps - 1)
    def _():
        for h in range(N_HEADS):
            o_ref[0, h] = (jnp.sum(acc[h], axis=1, keepdims=True)
                           + p0_scr[h:h + 1, 0:1] * vn_ref[0, h])


def _moba_sample(proj, cache_k, cache_v, layer, page_table):
    nb, n_pages = page_table.shape
    ps, gw, nh, hd = cache_k.shape[2], GROUP_WIDTH, N_HEADS, HEAD_DIM
    length = n_pages * ps
    assert length % MOBA_BLOCK == 0
    n_past_blocks = length // MOBA_BLOCK
    pp = min(PAGES_PER_STEP, n_pages)
    n_steps = n_pages // pp
    q, kn, vn = (proj[:, cb * gw:(cb + 1) * gw] for cb in (CB_Q, CB_K, CB_V))
    cache_kt = cache_k.transpose(0, 1, 3, 4, 2)
    cache_vt = cache_v.transpose(0, 1, 3, 4, 2)
    col4 = lambda x: x.reshape(nb, nh, hd, 1)
    col_spec = pl.BlockSpec((1, nh, hd, 1), lambda b, j, pt: (b, 0, 0, 0))

    def page_spec(i):
        return pl.BlockSpec((1, 1, nh, hd, ps), lambda b, j, pt: (layer, pt[b, j * pp + i], 0, 0, 0))

    scores = pl.pallas_call(
        _moba_scores_kernel,
        grid_spec=pltpu.PrefetchScalarGridSpec(
            num_scalar_prefetch=1,
            grid=(nb, n_steps),
            in_specs=[col_spec] + [page_spec(i) for i in range(pp)],
            out_specs=pl.BlockSpec((1, nh, pp * ps), lambda b, j, pt: (b, 0, j)),
        ),
        out_shape=jax.ShapeDtypeStruct((nb, nh, length), F32),
        compiler_params=_cp("arbitrary", "arbitrary"),
        name="moba_scores",
    )(page_table, col4(q), *([cache_kt] * pp))

    eht = (jnp.arange(nh)[:, None] == jnp.arange(gw)[None, :] // hd).astype(F32)
    bi = (jnp.arange(length)[:, None] // MOBA_BLOCK == jnp.arange(LANES)[None, :]).astype(BF16)
    slopes = jnp.broadcast_to(
        jnp.array([_alibi_slope(h) for h in range(nh)], F32)[:, None], (nh, LANES))
    row3 = pl.BlockSpec((1, 1, gw), lambda b, j, pt: (b, 0, 0))
    const = lambda x: pl.BlockSpec(x.shape, lambda b, j, pt: (0,) * x.ndim)
    out = pl.pallas_call(
        functools.partial(_moba_pv_kernel, n_steps=n_steps, n_past_blocks=n_past_blocks),
        grid_spec=pltpu.PrefetchScalarGridSpec(
            num_scalar_prefetch=1,
            grid=(nb, n_steps),
            in_specs=[pl.BlockSpec((1, nh, length), lambda b, j, pt: (b, 0, 0)), row3, row3,
                      const(eht), const(bi), const(slopes), col_spec]
                     + [page_spec(i) for i in range(pp)],
            out_specs=col_spec,
            scratch_shapes=[pltpu.VMEM((nh, hd, ps), F32), pltpu.VMEM((n_steps, nh, pp * ps), F32),
                            pltpu.VMEM((nh, LANES), F32)],
        ),
        out_shape=jax.ShapeDtypeStruct((nb, nh, hd, 1), F32),
        compiler_params=_cp("arbitrary", "arbitrary"),
        name="moba_pv",
    )(page_table, scores, q.reshape(nb, 1, gw), kn.reshape(nb, 1, gw), eht, bi, slopes, col4(vn),
      *([cache_vt] * pp))
    return out.reshape(nb, gw)


RWKV_PASSES = (1, 3)


def _shift_layout(shift):
    n = shift.shape[0]
    pad = jnp.zeros((n, LANES - 3 * TM_LORA), F32)
    return jnp.concatenate([shift, pad], axis=1)


def _layer_params(l, w_in, w_out, w_ffn_in, w_ffn_out, tm_mu, tm_w0, tm_w_up, tm_a0, tm_a_up, tm_g_up,
                  tm_k_k, tm_k_a, tm_r_k, tm_ln_g, tm_ln_b, sg_ws, sg_bs, sg_ln_g, sg_ln_b,
                  cv_w, cv_b, cv_ln_g, cv_ln_b):
    gw, lr, hd = GROUP_WIDTH, TM_LORA, HEAD_DIM
    wi = w_in[l]
    w_in_f32 = jnp.concatenate(
        [wi[:, A_PROJ:], wi[:, :A_PROJ], jnp.zeros((wi.shape[0], IN_PAD - IN_WIDTH), F32)], axis=1)
    mu = tm_mu[l]
    mu_l = jnp.concatenate([mu[3 * gw:], jnp.zeros((gw - 3 * lr,), F32)])
    vec = jnp.stack([mu[0:gw], mu[gw:2 * gw], mu[2 * gw:3 * gw], tm_w0[l], tm_a0[l], tm_k_k[l], tm_k_a[l],
                     tm_r_k[l].reshape(gw), tm_ln_g[l], tm_ln_b[l], mu_l]
                    + [jnp.zeros((gw,), F32)] * 5)
    wup = jnp.zeros((3, LANES, gw), F32)
    wup = wup.at[0, 0:lr].set(tm_w_up[l]).at[1, lr:2 * lr].set(tm_a_up[l]).at[2, 2 * lr:3 * lr].set(tm_g_up[l])
    sg_bias = jnp.repeat(sg_bs[l].T, hd, axis=1)
    sg_ln = jnp.stack([sg_ln_g[l], sg_ln_b[l]] + [jnp.zeros((gw,), F32)] * 6)
    sgp = jnp.stack([sg_ln_g[l], sg_ln_b[l], jnp.repeat(sg_ws[l][:, 0, 0], hd), jnp.repeat(sg_bs[l][:, 0], hd)]
                    + [jnp.zeros((gw,), F32)] * 4)
    cw = jnp.concatenate([cv_w[l], jnp.zeros((1, gw), F32)], axis=0)
    cvec = jnp.stack([cv_b[l], cv_ln_g[l], cv_ln_b[l]] + [jnp.zeros((gw,), F32)] * 5)
    return dict(vec=vec, wup=wup, sg_w=sg_ws[l], sg_bias=sg_bias, sg_ln=sg_ln, sgp=sgp, cw=cw, cvec=cvec,
                w_in_f32=w_in_f32[None], dense_f32=(w_out, w_ffn_in, w_ffn_out))


def _prompt_dense_tail(x, ys, mod, p, g_ffn, layer):
    ng, t, d = x.shape
    m = ng * t
    w_out, w_ffn_in, w_ffn_out = p['dense_f32']
    x1 = _mm(ys, w_out, layer, 1024, 512, res=x.reshape(m, d), gate=mod, gate_col=2, rows_per_group=t)
    h2 = _norm_mod(x1.reshape(ng, t, d), g_ffn, mod, 3, 4, 512).reshape(m, d)
    hid = _mm_swiglu(h2, w_ffn_in, layer, 1024, 512)
    x2 = _mm([hid], w_ffn_out, layer, 512, 256, res=x1, gate=mod, gate_col=5, rows_per_group=t)
    return x2.reshape(ng, t, d)


def _prompt_layer(x, mod, p, g_mix, g_ffn, e, layer):
    nb, t, d = x.shape
    m = nb * t
    gw, nh, hd = GROUP_WIDTH, N_HEADS, HEAD_DIM
    h = _norm_mod(x, g_mix, mod, 0, 1, 512).reshape(m, d)
    proj = _mm([h], p['w_in_f32'], 0, 1024, 512)
    shift0 = jnp.zeros((nb, 1, 3 * gw + LANES), F32)
    s0 = jnp.zeros((nb, nh, hd, hd), F32)
    ya, wkv = _rwkv_prompt(proj, nb, t, shift0, s0, p['vec'], p['wup'], e, RWKV_PASSES)
    yb = _moba_prompt(proj, nb, t)
    yc = _sgu_prompt(proj, m, p['sg_w'], p['sg_bias'], p['sg_ln'])
    yd, hist = _conv_prompt(proj, nb, t, jnp.zeros((nb, CONV_HALO, gw), F32), p['cw'], p['cvec'])
    x2 = _prompt_dense_tail(x, [ya.reshape(m, gw), yb, yc, yd], mod, p, g_ffn, layer)
    proj3 = proj.reshape(nb, t, IN_PAD)
    last = proj3[:, t - 1]
    shift = jnp.concatenate([last[:, CB_R * gw:CB_L * gw], last[:, CB_L * gw:CB_L * gw + 3 * TM_LORA]], axis=1)
    new_k = proj3[:, :, CB_K * gw:(CB_K + 1) * gw].reshape(nb, t, nh, hd)
    new_v = proj3[:, :, CB_V * gw:(CB_V + 1) * gw].reshape(nb, t, nh, hd)
    return x2, (shift, wkv, new_k, new_v, hist[:, CONV_HALO - (CONV_WIDTH - 1):])


def _sample_layer(x, mod, p, g_mix, g_ffn, e, layer, shift_prev, wkv_prev, conv_prev, cache_k, cache_v,
                  page_table):
    nb, t, d = x.shape
    gw, nh, hd = GROUP_WIDTH, N_HEADS, HEAD_DIM
    xg = x.reshape(1, nb, d)
    h = _norm_mod(xg, g_mix, mod, 0, 1, nb, dtype=F32).reshape(nb, d)
    proj = _mm_precise([h], p['w_in_f32'], 0, 512)
    ya, wkv, yc, sgv, yd, conv_new_t = _sample_mixers(
        proj, _shift_layout(shift_prev), wkv_prev, conv_prev.transpose(1, 0, 2),
        p['vec'], p['wup'], e, p['sgp'], p['cw'], p['cvec'])
    yb = _moba_sample(proj, cache_k, cache_v, layer, page_table)
    w_out, w_ffn_in, w_ffn_out = p['dense_f32']
    x1 = _mm_precise([ya, yb, yc, yd], w_out, layer, 512, mode="gated", res=x.reshape(nb, d), gate=mod, gate_col=2)
    h2 = _norm_mod(x1.reshape(1, nb, d), g_ffn, mod, 3, 4, nb, dtype=F32).reshape(nb, d)
    hid = _mm_precise([h2], w_ffn_in, layer, 512, mode="swiglu")
    x2 = _mm_precise([hid], w_ffn_out, layer, 256, mode="gated", res=x1, gate=mod, gate_col=5)
    shift = jnp.concatenate([proj[:, CB_R * gw:CB_L * gw], proj[:, CB_L * gw:CB_L * gw + 3 * TM_LORA]], axis=1)
    new_k = proj[:, CB_K * gw:(CB_K + 1) * gw].reshape(nb, 1, nh, hd)
    new_v = proj[:, CB_V * gw:(CB_V + 1) * gw].reshape(nb, 1, nh, hd)
    return x2.reshape(nb, t, d), (shift, wkv, new_k, new_v, sgv.reshape(nb, 1, gw), conv_new_t.transpose(1, 0, 2))


def kernel(x_prompt, x_sample, c_prompt, c_sample, cache_k_b, cache_v_b, state_a_wkv, state_a_shift, state_d_conv, page_table, w_ada, b_ada, g_norm_mix, w_in, w_out, tm_mu, tm_w0, tm_w_up, tm_a0, tm_a_up, tm_g_up, tm_k_k, tm_k_a, tm_r_k, tm_ln_g, tm_ln_b, sg_ws, sg_bs, sg_ln_g, sg_ln_b, cv_w, cv_b, cv_ln_g, cv_ln_b, g_norm_ffn, w_ffn_in, w_ffn_out, g_norm_final):
    depth = w_in.shape[0]
    bp, ns = x_prompt.shape[0], x_sample.shape[0]
    gw = GROUP_WIDTH
    rows = bp + ns
    rows_pad = -(-rows // 8) * 8
    c_all = jnp.concatenate([c_prompt, c_sample, jnp.zeros((rows_pad - rows, c_prompt.shape[1]), F32)], axis=0)
    mod = _ada(c_all, w_ada, b_ada)
    e = _head_ones()
    xp, xs = x_prompt, x_sample
    st_p, st_s = [], []
    for l in range(depth):
        p = _layer_params(l, w_in, w_out, w_ffn_in, w_ffn_out, tm_mu, tm_w0, tm_w_up, tm_a0, tm_a_up, tm_g_up,
                          tm_k_k, tm_k_a, tm_r_k, tm_ln_g, tm_ln_b, sg_ws, sg_bs, sg_ln_g, sg_ln_b,
                          cv_w, cv_b, cv_ln_g, cv_ln_b)
        mod_p = mod[l, :bp][:, None, :]
        mod_s = mod[l, bp:rows][None]
        xp, sp = _prompt_layer(xp, mod_p, p, g_norm_mix[l], g_norm_ffn[l], e, l)
        xs, ss = _sample_layer(xs, mod_s, p, g_norm_mix[l], g_norm_ffn[l], e, l, state_a_shift[l],
                               state_a_wkv[l], state_d_conv[l], cache_k_b, cache_v_b, page_table)
        st_p.append(sp)
        st_s.append(ss)
    y_prompt = _norm(xp, g_norm_final, 512)
    y_sample = _norm(xs.reshape(1, ns, -1), g_norm_final, ns).reshape(xs.shape)
    stack = lambda sts, i: jnp.stack([s[i] for s in sts])
    return (y_prompt, y_sample, stack(st_p, 2), stack(st_p, 3), stack(st_s, 2), stack(st_s, 3),
            stack(st_p, 1), stack(st_s, 1), stack(st_p, 0), stack(st_s, 0), stack(st_p, 4), stack(st_s, 5),
            stack(st_s, 4))
```

```python
import functools

import jax
import jax.numpy as jnp
from jax import lax
from jax.experimental import pallas as pl
from jax.experimental.pallas import tpu as pltpu

F32, BF16 = jnp.float32, jnp.bfloat16

D_MODEL = 2048
GROUP_WIDTH = 512
N_HEADS = 8
HEAD_DIM = 64
TM_LORA = 32
A_PROJ = 3 * GROUP_WIDTH + 3 * TM_LORA
IN_WIDTH = A_PROJ + 7 * GROUP_WIDTH
IN_PAD = 11 * GROUP_WIDTH
MOBA_BLOCK = 256
MOBA_TOPK = 3
SG_CHUNK = 128
CONV_WIDTH = 31
CONV_HALO = 32
RMS_EPS = 1e-6
LN_EPS = 1e-5
GN_EPS = 64e-5
RWKV_CHUNK = 64
RWKV_CHUNKS_PER_STEP = 4
INV_BLOCK = 16
LANES = 128

CB_Q, CB_K, CB_V, CB_SU, CB_SV, CB_CA, CB_CG, CB_R, CB_RK, CB_RV, CB_L = range(11)
(V_MU_R, V_MU_K, V_MU_V, V_W0, V_A0, V_KK, V_KA, V_RK, V_LNG, V_LNB, V_MU_L) = range(11)

VMEM_LIMIT = 56 * 1024 * 1024


def _cp(*sem):
    return pltpu.CompilerParams(dimension_semantics=sem, vmem_limit_bytes=VMEM_LIMIT)


def _dot(a, b):
    return jnp.dot(a, b, preferred_element_type=F32)


def _split2(x):
    hi = x.astype(BF16)
    return hi, (x - hi.astype(F32)).astype(BF16)


def _split3(x):
    h1 = x.astype(BF16)
    r1 = x - h1.astype(F32)
    h2 = r1.astype(BF16)
    return h1, h2, (r1 - h2.astype(F32)).astype(BF16)


def _dot_exact_rhs(x, e):
    hi, lo = _split2(x)
    return _dot(hi, e) + _dot(lo, e)


def _head_ones():
    r = lax.broadcasted_iota(jnp.int32, (GROUP_WIDTH, GROUP_WIDTH), 0) // HEAD_DIM
    c = lax.broadcasted_iota(jnp.int32, (GROUP_WIDTH, GROUP_WIDTH), 1) // HEAD_DIM
    return (r == c).astype(BF16)


def _layer_norm(x, g, b):
    mu = jnp.mean(x, axis=-1, keepdims=True)
    d = x - mu
    var = jnp.mean(d * d, axis=-1, keepdims=True)
    return d * lax.rsqrt(var + LN_EPS) * g + b


def _softplus(x):
    return jnp.maximum(x, 0.0) + jnp.log1p(jnp.exp(-jnp.abs(x)))


def _dot3(x, w):
    xh, xl = _split2(x)
    wh, wl = _split2(w)
    return _dot(xh, wh) + _dot(xh, wl) + _dot(xl, wh)


def _ada_kernel(c_ref, w_ref, b_ref, o_ref):
    c = c_ref[...]
    o_ref[0] = _dot3(c * jax.nn.sigmoid(c), w_ref[0]) + b_ref[0]


def _ada(c_all, w_ada, b_ada):
    depth, d, n = w_ada.shape
    r = c_all.shape[0]
    tn = 512
    return pl.pallas_call(
        _ada_kernel,
        grid=(depth, n // tn),
        in_specs=[pl.BlockSpec((r, d), lambda l, j: (0, 0)),
                  pl.BlockSpec((1, d, tn), lambda l, j: (l, 0, j)),
                  pl.BlockSpec((1, 1, tn), lambda l, j: (l, 0, j))],
        out_specs=pl.BlockSpec((1, r, tn), lambda l, j: (l, 0, j)),
        out_shape=jax.ShapeDtypeStruct((depth, r, n), F32),
        compiler_params=_cp("arbitrary", "arbitrary"),
        name="ada",
    )(c_all, w_ada, b_ada.reshape(depth, 1, n))


def _norm_mod_kernel(x_ref, g_ref, sc_ref, sh_ref, o_ref):
    x = x_ref[0]
    h = x * lax.rsqrt(jnp.mean(x * x, axis=-1, keepdims=True) + RMS_EPS) * g_ref[...]
    o_ref[0] = (h * (1.0 + sc_ref[0]) + sh_ref[0]).astype(o_ref.dtype)


def _norm_mod(x, g, mod, sh_col, sc_col, tt, dtype=BF16):
    ng, t, d = x.shape
    r = mod.shape[1]
    return pl.pallas_call(
        _norm_mod_kernel,
        grid=(ng, t // tt),
        in_specs=[pl.BlockSpec((1, tt, d), lambda b, i: (b, i, 0)),
                  pl.BlockSpec((1, d), lambda b, i: (0, 0)),
                  pl.BlockSpec((1, r, d), lambda b, i: (b, 0, sc_col)),
                  pl.BlockSpec((1, r, d), lambda b, i: (b, 0, sh_col))],
        out_specs=pl.BlockSpec((1, tt, d), lambda b, i: (b, i, 0)),
        out_shape=jax.ShapeDtypeStruct((ng, t, d), dtype),
        compiler_params=_cp("arbitrary", "arbitrary"),
        name="norm_mod",
    )(x, g.reshape(1, d), mod, mod)


def _norm_kernel(x_ref, g_ref, o_ref):
    x = x_ref[0]
    o_ref[0] = x * lax.rsqrt(jnp.mean(x * x, axis=-1, keepdims=True) + RMS_EPS) * g_ref[...]


def _norm(x, g, tt):
    ng, t, d = x.shape
    return pl.pallas_call(
        _norm_kernel,
        grid=(ng, t // tt),
        in_specs=[pl.BlockSpec((1, tt, d), lambda b, i: (b, i, 0)),
                  pl.BlockSpec((1, d), lambda b, i: (0, 0))],
        out_specs=pl.BlockSpec((1, tt, d), lambda b, i: (b, i, 0)),
        out_shape=jax.ShapeDtypeStruct((ng, t, d), F32),
        compiler_params=_cp("arbitrary", "arbitrary"),
        name="norm_final",
    )(x, g.reshape(1, d))


def _mm_kernel(*refs, n_x, gated):
    xs, w_ref = refs[:n_x], refs[n_x]
    o_ref, wb = refs[-2], refs[-1]

    @pl.when(pl.program_id(1) == 0)
    def _():
        wb[...] = w_ref[0].astype(BF16)

    acc = None
    k0 = 0
    for x_ref in xs:
        k = x_ref.shape[1]
        part = _dot(x_ref[...], wb[k0:k0 + k, :])
        acc = part if acc is None else acc + part
        k0 += k
    if gated:
        res_ref, gt_ref = refs[n_x + 1], refs[n_x + 2]
        o_ref[...] = res_ref[...] + gt_ref[0] * acc
    else:
        o_ref[...] = acc


def _mm(xs, w, layer, tm, tn, res=None, gate=None, gate_col=0, rows_per_group=None):
    m = xs[0].shape[0]
    kdim, n = w.shape[1], w.shape[2]
    gated = res is not None
    in_specs = [pl.BlockSpec((tm, x.shape[1]), lambda j, i: (i, 0)) for x in xs]
    in_specs.append(pl.BlockSpec((1, kdim, tn), lambda j, i: (layer, 0, j)))
    args = list(xs) + [w]
    if gated:
        r = gate.shape[1]
        tiles_per_group = rows_per_group // tm
        nj = n // tn
        in_specs.append(pl.BlockSpec((tm, tn), lambda j, i: (i, j)))
        in_specs.append(pl.BlockSpec((1, r, tn), lambda j, i: (i // tiles_per_group, 0, gate_col * nj + j)))
        args += [res, gate]
    return pl.pallas_call(
        functools.partial(_mm_kernel, n_x=len(xs), gated=gated),
        grid=(n // tn, m // tm),
        in_specs=in_specs,
        out_specs=pl.BlockSpec((tm, tn), lambda j, i: (i, j)),
        out_shape=jax.ShapeDtypeStruct((m, n), F32),
        scratch_shapes=[pltpu.VMEM((kdim, tn), BF16)],
        compiler_params=_cp("arbitrary", "arbitrary"),
        name="mm_gated" if gated else "mm",
    )(*args)


def _mm_rows_kernel(*refs, n_x, gated):
    xs, w_ref = refs[:n_x], refs[n_x]
    o_ref = refs[-1]
    acc = None
    k0 = 0
    for x_ref in xs:
        k = x_ref.shape[1]
        part = _dot(x_ref[...], w_ref[k0:k0 + k, :])
        acc = part if acc is None else acc + part
        k0 += k
    if gated:
        res_ref, gt_ref = refs[n_x + 1], refs[n_x + 2]
        o_ref[...] = res_ref[...] + gt_ref[0] * acc
    else:
        o_ref[...] = acc


def _mm_rows(xs, w, tm, tn, res=None, gate=None, gate_col=0, rows_per_group=None):
    m = xs[0].shape[0]
    n = w.shape[1]
    gated = res is not None
    in_specs = [pl.BlockSpec((tm, x.shape[1]), lambda i, j: (i, 0)) for x in xs]
    in_specs.append(pl.BlockSpec((w.shape[0], tn), lambda i, j: (0, j)))
    args = list(xs) + [w]
    if gated:
        r = gate.shape[1]
        tiles_per_group = rows_per_group // tm
        nj = n // tn
        in_specs.append(pl.BlockSpec((tm, tn), lambda i, j: (i, j)))
        in_specs.append(pl.BlockSpec((1, r, tn), lambda i, j: (i // tiles_per_group, 0, gate_col * nj + j)))
        args += [res, gate]
    return pl.pallas_call(
        functools.partial(_mm_rows_kernel, n_x=len(xs), gated=gated),
        grid=(m // tm, n // tn),
        in_specs=in_specs,
        out_specs=pl.BlockSpec((tm, tn), lambda i, j: (i, j)),
        out_shape=jax.ShapeDtypeStruct((m, n), F32),
        compiler_params=_cp("arbitrary", "arbitrary"),
        name="mm_rows_gated" if gated else "mm_rows",
    )(*args)


def _mm_precise_kernel(*refs, n_x, mode):
    xs = refs[:n_x]
    o_ref = refs[-1]
    if mode == "swiglu":
        x = xs[0][...]
        g = _dot3(x, refs[n_x][0])
        o_ref[...] = g * jax.nn.sigmoid(g) * _dot3(x, refs[n_x + 1][0])
        return
    w_ref = refs[n_x]
    acc = None
    k0 = 0
    for x_ref in xs:
        k = x_ref.shape[1]
        part = _dot3(x_ref[...], w_ref[0, k0:k0 + k, :])
        acc = part if acc is None else acc + part
        k0 += k
    if mode == "gated":
        res_ref, gt_ref = refs[n_x + 1], refs[n_x + 2]
        o_ref[...] = res_ref[...] + gt_ref[0] * acc
    else:
        o_ref[...] = acc


def _mm_precise(xs, w, layer, tn, mode="plain", res=None, gate=None, gate_col=0):
    m = xs[0].shape[0]
    kdim = w.shape[1]
    n = w.shape[2] // 2 if mode == "swiglu" else w.shape[2]
    nj = n // tn
    in_specs = [pl.BlockSpec((m, x.shape[1]), lambda j: (0, 0)) for x in xs]
    in_specs.append(pl.BlockSpec((1, kdim, tn), lambda j: (layer, 0, j)))
    args = list(xs) + [w]
    if mode == "swiglu":
        in_specs.append(pl.BlockSpec((1, kdim, tn), lambda j: (layer, 0, nj + j)))
        args.append(w)
    if mode == "gated":
        in_specs.append(pl.BlockSpec((m, tn), lambda j: (0, j)))
        in_specs.append(pl.BlockSpec((1, m, tn), lambda j: (0, 0, gate_col * nj + j)))
        args += [res, gate]
    return pl.pallas_call(
        functools.partial(_mm_precise_kernel, n_x=len(xs), mode=mode),
        grid=(nj,),
        in_specs=in_specs,
        out_specs=pl.BlockSpec((m, tn), lambda j: (0, j)),
        out_shape=jax.ShapeDtypeStruct((m, n), F32),
        compiler_params=_cp("arbitrary"),
        name="mm_precise_" + mode,
    )(*args)


def _swiglu_kernel(x_ref, wg_ref, wu_ref, o_ref, wgb, wub):
    @pl.when(pl.program_id(1) == 0)
    def _():
        wgb[...] = wg_ref[0].astype(BF16)
        wub[...] = wu_ref[0].astype(BF16)

    x = x_ref[...]
    g = _dot(x, wgb[...])
    u = _dot(x, wub[...])
    o_ref[...] = (g * jax.nn.sigmoid(g) * u).astype(o_ref.dtype)


def _mm_swiglu(x, w, layer, tm, tn):
    m, k = x.shape
    nh = w.shape[2] // 2
    nj = nh // tn
    return pl.pallas_call(
        _swiglu_kernel,
        grid=(nj, m // tm),
        in_specs=[pl.BlockSpec((tm, k), lambda j, i: (i, 0)),
                  pl.BlockSpec((1, k, tn), lambda j, i: (layer, 0, j)),
                  pl.BlockSpec((1, k, tn), lambda j, i: (layer, 0, nj + j))],
        out_specs=pl.BlockSpec((tm, tn), lambda j, i: (i, j)),
        out_shape=jax.ShapeDtypeStruct((m, nh), BF16),
        scratch_shapes=[pltpu.VMEM((k, tn), BF16), pltpu.VMEM((k, tn), BF16)],
        compiler_params=_cp("arbitrary", "arbitrary"),
        name="mm_swiglu",
    )(x, w, w)


def _rwkv_elem(pr, pk, pv, plo, qr, qk, qv, qlo, vec, wup_ref, e, precise=False):
    row = lambda i: vec[i:i + 1, :]
    xr = pr + (qr - pr) * row(V_MU_R)
    xk = pk + (qk - pk) * row(V_MU_K)
    xv = pv + (qv - pv) * row(V_MU_V)
    xl = plo + (qlo - plo) * vec[V_MU_L:V_MU_L + 1, 0:LANES]
    up = _dot3 if precise else (lambda x, w: _dot(x.astype(BF16), w.astype(BF16)))
    lw = up(jnp.tanh(xl), wup_ref[0])
    la = up(xl, wup_ref[1])
    g = up(jax.nn.sigmoid(xl), wup_ref[2])
    w_log = -_softplus(-(row(V_W0) + lw)) - 0.5
    log_decay = -jnp.exp(w_log)
    a = jax.nn.sigmoid(row(V_A0) + la)
    kk = xk * row(V_KK)
    kk = kk / jnp.maximum(jnp.sqrt(_dot_exact_rhs(kk * kk, e)), 1e-12)
    kmod = xk * (1.0 + (a - 1.0) * row(V_KA))
    bonus = _dot_exact_rhs(xr * kmod * row(V_RK), e) * xv
    return xr, log_decay, kmod, xv, kk, kk * a, g, bonus


def _rwkv_prep(pr_ref, pk_ref, pv_ref, pl_ref, sp_ref, vec_ref, wup_ref, e_ref, head_refs, carry):
    rt_ref, at_ref, bt_ref, kt_ref, v_ref, bh_ref, kh_ref, gt_ref = head_refs
    ti = pl.program_id(1)

    @pl.when(ti == 0)
    def _():
        carry[...] = sp_ref[0]

    tt = pr_ref.shape[0]
    first = lax.broadcasted_iota(jnp.int32, (tt, 1), 0) == 0
    gw = GROUP_WIDTH
    cur = (pr_ref[...], pk_ref[...], pv_ref[...], pl_ref[...])
    offs = (0, gw, 2 * gw, 3 * gw)
    prev = []
    for p, o in zip(cur, offs):
        w = p.shape[1]
        prev.append(jnp.where(first, carry[:, o:o + w], pltpu.roll(p, 1, axis=0)))
        carry[:, o:o + w] = p[tt - 1:tt, :]
    vec = vec_ref[...]
    e = e_ref[...]
    r, lw, kmod, v, kk, b, g, bonus = _rwkv_elem(*cur, *prev, vec, wup_ref, e)

    ri = lax.broadcasted_iota(jnp.int32, (tt, tt), 0)
    ci = lax.broadcasted_iota(jnp.int32, (tt, tt), 1)
    same = (ri // RWKV_CHUNK) == (ci // RWKV_CHUNK)
    tri = (same & (ci <= ri)).astype(BF16)
    blk = same.astype(BF16)
    parts = _split3(lw)
    cum = sum(_dot(tri, p) for p in parts)
    tot = sum(_dot(blk, p) for p in parts)
    e_out = jnp.exp(-cum)
    to_end = jnp.exp(tot - cum)
    outs = ((rt_ref, r * jnp.exp(cum)), (at_ref, -kk * jnp.exp(cum - lw)), (bt_ref, b * e_out),
            (kt_ref, kmod * e_out), (v_ref, v), (bh_ref, b * to_end), (kh_ref, kmod * to_end),
            (gt_ref, jnp.exp(tot)))
    for ref, val in outs:
        for h in range(N_HEADS):
            ref[h] = val[:, h * HEAD_DIM:(h + 1) * HEAD_DIM]
    return g, bonus


def _bmm(a, b, ca, cb, passes):
    dn = (((ca,), (cb,)), ((0,), (0,)))
    d = lambda x, y: lax.dot_general(x, y, dn, preferred_element_type=F32)
    if passes == 1:
        return d(a.astype(BF16), b.astype(BF16))
    ah, al = _split2(a)
    bh, bl = _split2(b)
    return d(ah, bh) + d(ah, bl) + d(al, bh)


def _rwkv_chunks(head_refs, s0_ref, o_ref, st_ref, s_scr, passes):
    rt_ref, at_ref, bt_ref, kt_ref, v_ref, bh_ref, kh_ref, gt_ref = head_refs
    c = pl.program_id(1)

    @pl.when(c == 0)
    def _():
        s_scr[...] = s0_ref[0]

    n = RWKV_CHUNK
    nc = rt_ref.shape[1] // n
    shp = (1, n, n)
    ti = lax.broadcasted_iota(jnp.int32, shp, 1)
    tj = lax.broadcasted_iota(jnp.int32, shp, 2)
    strict, incl = tj < ti, tj <= ti
    same = (ti // INV_BLOCK) == (tj // INV_BLOCK)
    nn = lambda a, b, p: _bmm(a, b, 2, 1, p)
    nt = lambda a, b, p: _bmm(a, b, 2, 2, p)
    tn = lambda a, b, p: _bmm(a, b, 1, 1, p)
    p_gen, p_chain = passes

    per_chunk = lambda ref: ref[...].reshape(N_HEADS * nc, n, HEAD_DIM)
    rt, at, bt, kt, v, bh, kh = (per_chunk(r) for r in (rt_ref, at_ref, bt_ref, kt_ref, v_ref, bh_ref, kh_ref))
    g_end = per_chunk(gt_ref)[:, 0:1, :]
    big = nt(jnp.concatenate([at, rt], axis=1), jnp.concatenate([bt, kt], axis=1), p_gen)
    a_ab = jnp.where(strict, big[:, :n, :n], 0.0)
    a_ak = jnp.where(strict, big[:, :n, n:], 0.0)
    a_rb = jnp.where(incl, big[:, n:, :n], 0.0)
    a_rk = jnp.where(incl, big[:, n:, n:], 0.0)
    av = nn(jnp.concatenate([a_ak, a_rk], axis=1), v, p_gen)
    nd = jnp.where(same, a_ab, 0.0)
    no = jnp.where(same, 0.0, a_ab)
    nd2 = nn(nd, nd, p_gen)
    nd4 = nn(nd2, nd2, p_gen)
    nd8 = nn(nd4, nd4, p_gen)
    y = jnp.concatenate([no, at, av[:, :n]], axis=2)
    for pw in (nd8, nd4, nd2, nd):
        y = y + nn(pw, y, p_gen)
    m = y[:, :, :n]
    z = y[:, :, n:]
    m2 = nn(m, m, p_gen)
    z = z + nn(m2, z, p_gen)
    z = z + nn(m, z, p_gen)
    qo = jnp.concatenate([rt, av[:, n:]], axis=2) + nn(a_rb, z, p_gen)
    pg = tn(z, bh, p_gen)
    gm = pg[:, n:] + tn(v, kh, p_gen)

    by_chunk = lambda x: x.reshape((N_HEADS, nc) + x.shape[1:])
    qo, pmat, gm, g_end = by_chunk(qo), by_chunk(pg[:, :n]), by_chunk(gm), by_chunk(g_end)
    s = s_scr[...]
    for i in range(nc):
        o_ref[:, i * n:(i + 1) * n, :] = nt(qo[:, i, :, :HEAD_DIM], s, p_chain) + qo[:, i, :, HEAD_DIM:]
        s = s * g_end[:, i] + nn(s, pmat[:, i], p_chain) + gm[:, i]
    s_scr[...] = s

    @pl.when(c == pl.num_programs(1) - 1)
    def _():
        st_ref[0] = s


def _rwkv_prompt_kernel(pr_ref, pk_ref, pv_ref, pl_ref, sp_ref, vec_ref, wup_ref, e_ref, s0_ref,
                        ya_ref, st_ref, carry, s_scr, o_scr, *head_scr, passes):
    g, bonus = _rwkv_prep(pr_ref, pk_ref, pv_ref, pl_ref, sp_ref, vec_ref, wup_ref, e_ref, head_scr, carry)
    _rwkv_chunks(head_scr, s0_ref, o_scr, st_ref, s_scr, passes)
    o = jnp.concatenate([o_scr[h] for h in range(N_HEADS)], axis=-1)
    e = e_ref[...]
    vec = vec_ref[...]
    mu = _dot_exact_rhs(o, e) * (1.0 / HEAD_DIM)
    d = o - mu
    var = _dot_exact_rhs(d * d, e) * (1.0 / HEAD_DIM)
    on = d * lax.rsqrt(var + GN_EPS) * vec[V_LNG:V_LNG + 1, :] + vec[V_LNB:V_LNB + 1, :]
    ya_ref[0] = ((on + bonus) * g).astype(ya_ref.dtype)


def _rwkv_prompt(proj, nb, t, shift_prev, s0, vec, wup, e, passes):
    tt = RWKV_CHUNKS_PER_STEP * RWKV_CHUNK
    nt_ = t // tt
    gw, nh, hd = GROUP_WIDTH, N_HEADS, HEAD_DIM
    pspec = lambda cb: pl.BlockSpec((tt, gw), lambda b, i: (b * nt_ + i, cb))
    const = lambda shape: pl.BlockSpec(shape, lambda b, i: (0,) * len(shape))
    sspec = pl.BlockSpec((1, nh, hd, hd), lambda b, i: (b, 0, 0, 0))
    head_major = pltpu.VMEM((nh, tt, hd), F32)
    return pl.pallas_call(
        functools.partial(_rwkv_prompt_kernel, passes=passes),
        grid=(nb, nt_),
        in_specs=[pspec(CB_R), pspec(CB_RK), pspec(CB_RV),
                  pl.BlockSpec((tt, LANES), lambda b, i: (b * nt_ + i, CB_L * (gw // LANES))),
                  pl.BlockSpec((1, 1, 3 * gw + LANES), lambda b, i: (b, 0, 0)),
                  const(vec.shape), const(wup.shape), const(e.shape), sspec],
        out_specs=[pl.BlockSpec((1, tt, gw), lambda b, i: (b, i, 0)), sspec],
        out_shape=[jax.ShapeDtypeStruct((nb, t, gw), BF16), jax.ShapeDtypeStruct((nb, nh, hd, hd), F32)],
        scratch_shapes=[pltpu.VMEM((1, 3 * gw + LANES), F32), pltpu.VMEM((nh, hd, hd), F32), head_major]
                       + [head_major] * 8,
        compiler_params=_cp("arbitrary", "arbitrary"),
        name="rwkv_prompt",
    )(proj, proj, proj, proj, shift_prev, vec, wup, e, s0)


def _alibi_slope(h):
    return 2.0 ** (-8.0 * (h + 1) / N_HEADS)


LOG2E = 1.4426950408889634
MASKED = -1e30


def _eye(n):
    return (lax.broadcasted_iota(jnp.int32, (n, n), 0) == lax.broadcasted_iota(jnp.int32, (n, n), 1)).astype(BF16)


def _moba_prompt_kernel(q_ref, k_ref, v_ref, selm_ref, slope_ref, o_ref, kh_scr, vt_scr, km_scr, selt_scr, m_scr,
                        acc_scr, *, n_blocks):
    qi = pl.program_id(1)
    blk, nh, hd = MOBA_BLOCK, N_HEADS, HEAD_DIM
    nt_dims = (((1,), (1,)), ((), ()))

    @pl.when(qi == 0)
    def _():
        k = k_ref[...]
        v = v_ref[...]
        one_row = (lax.broadcasted_iota(jnp.int32, (hd, blk), 0) == 0).astype(BF16)
        eye_hd = _eye(hd)
        for h in range(nh):
            kh_scr[h] = k[:, h * hd:(h + 1) * hd].astype(BF16)
            vt = lax.dot_general(eye_hd, v[:, h * hd:(h + 1) * hd].astype(BF16), nt_dims,
                                 preferred_element_type=F32).astype(BF16)
            for n in range(n_blocks):
                vt_scr[n, h] = jnp.concatenate([vt[:, n * blk:(n + 1) * blk], one_row], axis=0)
        rows = [jnp.mean(k[n * blk:(n + 1) * blk, :], axis=0, keepdims=True) for n in range(n_blocks)]
        rows += [jnp.zeros((1, k.shape[1]), F32)] * (selm_ref.shape[0] - n_blocks)
        kmean = jnp.concatenate(rows, axis=0)
        tn_dims = (((0,), (0,)), ((), ()))
        spread = sum(lax.dot_general(p, selm_ref[...], tn_dims, preferred_element_type=F32)
                     for p in _split3(kmean))
        rh = lax.broadcasted_iota(jnp.int32, spread.shape, 0) // hd
        ch = lax.broadcasted_iota(jnp.int32, spread.shape, 1) % nh
        for i, p in enumerate(_split3(jnp.where(rh == ch, spread, 0.0))):
            km_scr[i] = p

    own = qi
    q = q_ref[...]
    q1, q2, q3 = _split3(q * (hd ** -0.5))
    k1, k2, k3 = km_scr[0], km_scr[1], km_scr[2]
    gate = (_dot(q1, k1) + _dot(q1, k2) + _dot(q2, k1)) + (_dot(q1, k3) + _dot(q2, k2) + _dot(q3, k1))
    lane_blk = lax.broadcasted_iota(jnp.int32, gate.shape, 1) // nh
    valid = lane_blk < own
    gate = jnp.where(valid, gate, -jnp.inf)
    rank = jnp.zeros_like(gate)
    for d in range(1, n_blocks):
        earlier = pltpu.roll(gate, nh * d, axis=1)
        later = pltpu.roll(gate, LANES - nh * d, axis=1)
        rank = rank + jnp.where(earlier >= gate, 1.0, 0.0) + jnp.where(later > gate, 1.0, 0.0)
    sel = jnp.where(valid & (rank < min(MOBA_TOPK, n_blocks - 1)), 1.0, 0.0)
    selt_scr[...] = lax.dot_general(_eye(LANES), sel.astype(BF16), nt_dims, preferred_element_type=F32)

    qa = q * ((hd ** -0.5) * LOG2E)
    qh = jnp.stack([qa[:, h * hd:(h + 1) * hd].astype(BF16) for h in range(nh)])
    ahead_i = (lax.broadcasted_iota(jnp.int32, (blk, blk), 1) - lax.broadcasted_iota(jnp.int32, (blk, blk), 0))
    causal = ahead_i >= 0
    ahead = ahead_i.astype(F32)
    slope2 = slope_ref[...] * LOG2E
    bnt = (((2,), (2,)), ((0,), (0,)))
    bnn = (((2,), (1,)), ((0,), (0,)))

    def scores(first_key):
        k_blk = kh_scr[:, pl.ds(first_key, blk), :]
        s = lax.dot_general(k_blk, qh, bnt, preferred_element_type=F32)
        return [s[h] - (_alibi_slope(h) * LOG2E) * ahead for h in range(nh)]

    tm = jnp.stack([jnp.where(causal, sh, MASKED) for sh in scores(pl.multiple_of(own * blk, blk))])
    m = jnp.max(tm, axis=1, keepdims=True)
    p = jnp.exp2(tm - m)
    m_scr[...] = m
    acc_scr[...] = lax.dot_general(vt_scr[own], p.astype(BF16), bnn, preferred_element_type=F32)

    def past_block(n, carry):
        shift = slope2 * ((own - n) * blk).astype(F32)
        sh = scores(pl.multiple_of(n * blk, blk))
        tm = jnp.stack([jnp.where(selt_scr[pl.ds(n * nh + h, 1), :] > 0.5, sh[h], MASKED) for h in range(nh)])
        m_old = m_scr[...]
        m_new = jnp.maximum(m_old, jnp.max(tm, axis=1, keepdims=True) - shift)
        p = jnp.exp2(tm - (m_new + shift))
        m_scr[...] = m_new
        acc_scr[...] = (jnp.exp2(m_old - m_new) * acc_scr[...]
                        + lax.dot_general(vt_scr[n], p.astype(BF16), bnn, preferred_element_type=F32))
        return carry

    lax.fori_loop(0, own, past_block, 0)
    outs = []
    eye_hd = _eye(hd)
    for h in range(nh):
        acc = acc_scr[h]
        out_t = (acc[0:hd, :] / acc[hd:hd + 1, :]).astype(BF16)
        outs.append(lax.dot_general(out_t, eye_hd, (((0,), (0,)), ((), ())), preferred_element_type=F32))
    o_ref[...] = jnp.concatenate(outs, axis=-1).astype(o_ref.dtype)


def _moba_prompt(proj, nb, t):
    blk, gw, nh, hd = MOBA_BLOCK, GROUP_WIDTH, N_HEADS, HEAD_DIM
    n_blocks = t // blk
    assert n_blocks * nh <= LANES // 2
    selm = (jnp.arange(8)[:, None] == jnp.arange(LANES)[None, :] // nh).astype(BF16)
    slopes = jnp.broadcast_to(jnp.array([_alibi_slope(h) for h in range(nh)], F32)[:, None, None], (nh, 1, blk))
    return pl.pallas_call(
        functools.partial(_moba_prompt_kernel, n_blocks=n_blocks),
        grid=(nb, n_blocks),
        in_specs=[pl.BlockSpec((blk, gw), lambda b, i: (b * n_blocks + i, CB_Q)),
                  pl.BlockSpec((t, gw), lambda b, i: (b, CB_K)),
                  pl.BlockSpec((t, gw), lambda b, i: (b, CB_V)),
                  pl.BlockSpec((8, LANES), lambda b, i: (0, 0)),
                  pl.BlockSpec((nh, 1, blk), lambda b, i: (0, 0, 0))],
        out_specs=pl.BlockSpec((blk, gw), lambda b, i: (b * n_blocks + i, 0)),
        out_shape=jax.ShapeDtypeStruct((nb * t, gw), BF16),
        scratch_shapes=[pltpu.VMEM((nh, t, hd), BF16), pltpu.VMEM((n_blocks, nh, 2 * hd, blk), BF16),
                        pltpu.VMEM((3, gw, LANES), BF16), pltpu.VMEM((LANES, blk), F32),
                        pltpu.VMEM((nh, 1, blk), F32), pltpu.VMEM((nh, 2 * hd, blk), F32)],
        compiler_params=_cp("arbitrary", "arbitrary"),
        name="moba_prompt",
    )(proj, proj, proj, selm, slopes)


def _sgu_prompt_kernel(pu_ref, pv_ref, w_ref, bias_ref, ln_ref, o_ref):
    u = jax.nn.gelu(pu_ref[...])
    v = _layer_norm(jax.nn.gelu(pv_ref[...]), ln_ref[0:1, :], ln_ref[1:2, :]).astype(BF16)
    cl, hd = SG_CHUNK, HEAD_DIM
    ri = lax.broadcasted_iota(jnp.int32, (cl, cl), 0)
    ci = lax.broadcasted_iota(jnp.int32, (cl, cl), 1)
    ws = [jnp.where(ci <= ri, w_ref[h], 0.0).astype(BF16) for h in range(N_HEADS)]
    bias = bias_ref[...]
    for c in range(u.shape[0] // cl):
        vc = v[c * cl:(c + 1) * cl, :]
        mixed = jnp.concatenate([_dot(ws[h], vc[:, h * hd:(h + 1) * hd]) for h in range(N_HEADS)], axis=-1)
        o_ref[c * cl:(c + 1) * cl, :] = (u[c * cl:(c + 1) * cl, :] * (mixed + bias)).astype(o_ref.dtype)


def _sgu_prompt(proj, nrows, w, bias_exp, ln):
    tt, gw = 256, GROUP_WIDTH
    return pl.pallas_call(
        _sgu_prompt_kernel,
        grid=(nrows // tt,),
        in_specs=[pl.BlockSpec((tt, gw), lambda i: (i, CB_SU)),
                  pl.BlockSpec((tt, gw), lambda i: (i, CB_SV)),
                  pl.BlockSpec(w.shape, lambda i: (0, 0, 0)),
                  pl.BlockSpec(bias_exp.shape, lambda i: (0, 0)),
                  pl.BlockSpec(ln.shape, lambda i: (0, 0))],
        out_specs=pl.BlockSpec((tt, gw), lambda i: (i, 0)),
        out_shape=jax.ShapeDtypeStruct((nrows, gw), BF16),
        compiler_params=_cp("arbitrary"),
        name="sgu_prompt",
    )(proj, proj, w, bias_exp, ln)


def _conv_prompt_kernel(pa_ref, pg_ref, prev_ref, w_ref, vec_ref, o_ref, hist_ref, zf):
    ti = pl.program_id(1)
    tt = pa_ref.shape[0]
    halo = CONV_HALO

    @pl.when(ti == 0)
    def _():
        zf[0:halo, :] = prev_ref[0]

    z = pa_ref[...] * jax.nn.sigmoid(pg_ref[...])
    zf[halo:halo + tt, :] = z
    lead = halo - (CONV_WIDTH - 1)
    acc = jnp.zeros_like(z)
    for i in range(CONV_WIDTH):
        acc = acc + w_ref[i:i + 1, :] * zf[pl.ds(lead + i, tt), :]
    y = _layer_norm(acc + vec_ref[0:1, :], vec_ref[1:2, :], vec_ref[2:3, :])
    o_ref[...] = (y * jax.nn.sigmoid(y)).astype(o_ref.dtype)
    tail = zf[pl.ds(tt, halo), :]
    hist_ref[0] = tail
    zf[0:halo, :] = tail


def _conv_prompt(proj, nb, t, prev, w, vec):
    tt, gw = 256, GROUP_WIDTH
    nt_ = t // tt
    return pl.pallas_call(
        _conv_prompt_kernel,
        grid=(nb, nt_),
        in_specs=[pl.BlockSpec((tt, gw), lambda b, i: (b * nt_ + i, CB_CA)),
                  pl.BlockSpec((tt, gw), lambda b, i: (b * nt_ + i, CB_CG)),
                  pl.BlockSpec((1, CONV_HALO, gw), lambda b, i: (b, 0, 0)),
                  pl.BlockSpec(w.shape, lambda b, i: (0, 0)),
                  pl.BlockSpec(vec.shape, lambda b, i: (0, 0))],
        out_specs=[pl.BlockSpec((tt, gw), lambda b, i: (b * nt_ + i, 0)),
                   pl.BlockSpec((1, CONV_HALO, gw), lambda b, i: (b, 0, 0))],
        out_shape=[jax.ShapeDtypeStruct((nb * t, gw), BF16),
                   jax.ShapeDtypeStruct((nb, CONV_HALO, gw), F32)],
        scratch_shapes=[pltpu.VMEM((CONV_HALO + tt, gw), F32)],
        compiler_params=_cp("arbitrary", "arbitrary"),
        name="conv_prompt",
    )(proj, proj, prev, w, vec)


def _sample_elem_kernel(pr_ref, pk_ref, pv_ref, pl_ref, sp_ref, vec_ref, wup_ref, e_ref,
                        su_ref, sv_ref, sgp_ref, ca_ref, cg_ref, cprev_ref, cw_ref, cvec_ref,
                        r_ref, w_ref, k_ref, v_ref, kk_ref, b_ref, g_ref, bonus_ref,
                        yc_ref, sgv_ref, yd_ref, cnew_ref):
    gw = GROUP_WIDTH
    sp = sp_ref[...]
    r, lw, kmod, v, kk, b, g, bonus = _rwkv_elem(
        pr_ref[...], pk_ref[...], pv_ref[...], pl_ref[...],
        sp[:, 0:gw], sp[:, gw:2 * gw], sp[:, 2 * gw:3 * gw], sp[:, 3 * gw:3 * gw + LANES],
        vec_ref[...], wup_ref, e_ref[...], precise=True)
    for ref, val in ((r_ref, r), (w_ref, jnp.exp(lw)), (k_ref, kmod), (v_ref, v), (kk_ref, kk),
                     (b_ref, b), (g_ref, g), (bonus_ref, bonus)):
        ref[...] = val
    u = jax.nn.gelu(su_ref[...])
    sv = _layer_norm(jax.nn.gelu(sv_ref[...]), sgp_ref[0:1, :], sgp_ref[1:2, :])
    sgv_ref[...] = sv
    yc_ref[...] = (u * (sgp_ref[2:3, :] * sv + sgp_ref[3:4, :])).astype(yc_ref.dtype)
    z = ca_ref[...] * jax.nn.sigmoid(cg_ref[...])
    nprev = CONV_WIDTH - 1
    acc = cw_ref[nprev:nprev + 1, :] * z
    for i in range(nprev):
        acc = acc + cw_ref[i:i + 1, :] * cprev_ref[i]
        if i > 0:
            cnew_ref[i - 1] = cprev_ref[i]
    cnew_ref[nprev - 1] = z
    y = _layer_norm(acc + cvec_ref[0:1, :], cvec_ref[1:2, :], cvec_ref[2:3, :])
    yd_ref[...] = (y * jax.nn.sigmoid(y)).astype(yd_ref.dtype)


def _sample_wkv_kernel(r_ref, w_ref, k_ref, v_ref, kk_ref, b_ref, g_ref, bonus_ref, ln_ref, s_ref,
                       ya_ref, st_ref):
    hd = HEAD_DIM
    r, w, k, v, kk, b, g, bonus = (x[0] for x in (r_ref, w_ref, k_ref, v_ref, kk_ref, b_ref, g_ref, bonus_ref))
    s = s_ref[0]
    eye = (lax.broadcasted_iota(jnp.int32, (hd, hd), 0) == lax.broadcasted_iota(jnp.int32, (hd, hd), 1))
    eye = eye.astype(F32)[None]
    v_col = jnp.sum(eye * v, axis=2, keepdims=True)
    sa = jnp.sum(s * (-kk), axis=2, keepdims=True)
    s_new = s * w + sa * b + v_col * k
    st_ref[0] = s_new
    o_col = jnp.sum(s_new * r, axis=2, keepdims=True)
    o = jnp.sum(o_col * eye, axis=1, keepdims=True)
    mu = jnp.mean(o, axis=-1, keepdims=True)
    d = o - mu
    var = jnp.mean(d * d, axis=-1, keepdims=True)
    on = d * lax.rsqrt(var + GN_EPS) * ln_ref[0] + ln_ref[1]
    ya_ref[0] = (on + bonus) * g


def _sample_mixers(proj, shift_prev, s0, conv_prev_t, vec, wup, e, sgp, cw, cvec):
    nb = proj.shape[0]
    gw, nh, hd = GROUP_WIDTH, N_HEADS, HEAD_DIM
    col = lambda cb: pl.BlockSpec((nb, gw), lambda i: (0, cb))
    full = lambda x: pl.BlockSpec(x.shape, lambda i: (0,) * x.ndim)
    row_f32 = jax.ShapeDtypeStruct((nb, gw), F32)
    outs = pl.pallas_call(
        _sample_elem_kernel,
        grid=(1,),
        in_specs=[col(CB_R), col(CB_RK), col(CB_RV),
                  pl.BlockSpec((nb, LANES), lambda i: (0, CB_L * (gw // LANES))),
                  full(shift_prev), full(vec), full(wup), full(e),
                  col(CB_SU), col(CB_SV), full(sgp), col(CB_CA), col(CB_CG),
                  full(conv_prev_t), full(cw), full(cvec)],
        out_specs=[pl.BlockSpec((nb, gw), lambda i: (0, 0))] * 11 + [full(conv_prev_t)],
        out_shape=[row_f32] * 11 + [jax.ShapeDtypeStruct(conv_prev_t.shape, F32)],
        compiler_params=_cp("arbitrary"),
        name="sample_elem",
    )(proj, proj, proj, proj, shift_prev, vec, wup, e, proj, proj, sgp, proj, proj, conv_prev_t, cw, cvec)
    heads = [x.reshape(nb, nh, 1, hd) for x in outs[:8]]
    yc, sgv, yd, conv_new_t = outs[8:]
    ln = jnp.stack([vec[V_LNG].reshape(nh, 1, hd), vec[V_LNB].reshape(nh, 1, hd)])
    hspec = pl.BlockSpec((1, nh, 1, hd), lambda b: (b, 0, 0, 0))
    sspec = pl.BlockSpec((1, nh, hd, hd), lambda b: (b, 0, 0, 0))
    ya, s_new = pl.pallas_call(
        _sample_wkv_kernel,
        grid=(nb,),
        in_specs=[hspec] * 8 + [pl.BlockSpec(ln.shape, lambda b: (0, 0, 0, 0)), sspec],
        out_specs=[hspec, sspec],
        out_shape=[jax.ShapeDtypeStruct((nb, nh, 1, hd), F32), jax.ShapeDtypeStruct(s0.shape, F32)],
        compiler_params=_cp("arbitrary"),
        name="sample_wkv",
    )(*heads, ln, s0)
    return ya.reshape(nb, gw), s_new, yc, sgv, yd, conv_new_t


PAGES_PER_STEP = 16


def _moba_scores_kernel(pt_ref, q_ref, *refs):
    pages, o_ref = refs[:-1], refs[-1]
    ps = pages[0].shape[4]
    qb = jnp.broadcast_to(q_ref[0] * (HEAD_DIM ** -0.5), pages[0].shape[2:])
    for i, p_ref in enumerate(pages):
        rows = [jnp.sum(p_ref[0, 0, h] * qb[h], axis=0, keepdims=True) for h in range(N_HEADS)]
        o_ref[0, :, i * ps:(i + 1) * ps] = jnp.concatenate(rows, axis=0)


def _moba_select(s, q, kn, eht, bi, slope, n_past_blocks):
    nh, length = s.shape
    s0 = jnp.sum(eht * (q * (HEAD_DIM ** -0.5) * kn), axis=1, keepdims=True)
    gate = sum(_dot(p, bi) for p in _split3(s)) * (1.0 / MOBA_BLOCK)
    lane = lax.broadcasted_iota(jnp.int32, gate.shape, 1)
    gate = jnp.where(lane < n_past_blocks, gate, -jnp.inf)
    sel = jnp.zeros(gate.shape, jnp.bool_)
    for _ in range(min(MOBA_TOPK, n_past_blocks)):
        m = jnp.max(gate, axis=1, keepdims=True)
        idx = jnp.min(jnp.where(gate == m, lane, LANES), axis=1, keepdims=True)
        pick = lane == idx
        sel = sel | pick
        gate = jnp.where(pick, -jnp.inf, gate)
    picked = lax.dot_general(sel.astype(BF16), bi, (((1,), (1,)), ((), ())),
                             preferred_element_type=F32) > 0.5
    pos = lax.broadcasted_iota(jnp.int32, s.shape, 1)
    dist = (length - pos).astype(F32)
    logit = jnp.where(picked, s - slope[:, 0:1] * dist, -jnp.inf)
    m = jnp.maximum(jnp.max(logit, axis=1, keepdims=True), s0)
    p = jnp.exp(logit - m)
    p0 = jnp.exp(s0 - m)
    inv = 1.0 / (jnp.sum(p, axis=1, keepdims=True) + p0)
    return p * inv, p0 * inv


def _moba_pv_kernel(pt_ref, s_ref, q_ref, kn_ref, eht_ref, bi_ref, slope_ref, vn_ref, *refs,
                    n_steps, n_past_blocks):
    pages, o_ref, acc, p_scr, p0_scr = refs[:-4], refs[-4], refs[-3], refs[-2], refs[-1]
    j = pl.program_id(1)
    ps = pages[0].shape[4]
    w = len(pages) * ps

    @pl.when(j == 0)
    def _():
        acc[...] = jnp.zeros_like(acc)
        p, p0 = _moba_select(s_ref[0], q_ref[0], kn_ref[0], eht_ref[...], bi_ref[...], slope_ref[...],
                             n_past_blocks)
        for jj in range(n_steps):
            p_scr[jj] = p[:, jj * w:(jj + 1) * w]
        p0_scr[...] = jnp.broadcast_to(p0, p0_scr.shape)

    p_blk = p_scr[j]
    for h in range(N_HEADS):
        tot = acc[h]
        for i, v_ref in enumerate(pages):
            tot = tot + v_ref[0, 0, h] * p_blk[h:h + 1, i * ps:(i + 1) * ps]
        acc[h] = tot

    @pl.when(j == n_steps - 1)
    def _():
        for h in range(N_HEADS):
            o_ref[0, h] = (jnp.sum(acc[h], axis=1, keepdims=True)
                           + p0_scr[h:h + 1, 0:1] * vn_ref[0, h])


def _moba_sample(proj, cache_k, cache_v, layer, page_table):
    nb, n_pages = page_table.shape
    ps, gw, nh, hd = cache_k.shape[2], GROUP_WIDTH, N_HEADS, HEAD_DIM
    length = n_pages * ps
    assert length % MOBA_BLOCK == 0
    n_past_blocks = length // MOBA_BLOCK
    pp = min(PAGES_PER_STEP, n_pages)
    n_steps = n_pages // pp
    q, kn, vn = (proj[:, cb * gw:(cb + 1) * gw] for cb in (CB_Q, CB_K, CB_V))
    cache_kt = cache_k.transpose(0, 1, 3, 4, 2)
    cache_vt = cache_v.transpose(0, 1, 3, 4, 2)
    col4 = lambda x: x.reshape(nb, nh, hd, 1)
    col_spec = pl.BlockSpec((1, nh, hd, 1), lambda b, j, pt: (b, 0, 0, 0))

    def page_spec(i):
        return pl.BlockSpec((1, 1, nh, hd, ps), lambda b, j, pt: (layer, pt[b, j * pp + i], 0, 0, 0))

    scores = pl.pallas_call(
        _moba_scores_kernel,
        grid_spec=pltpu.PrefetchScalarGridSpec(
            num_scalar_prefetch=1,
            grid=(nb, n_steps),
            in_specs=[col_spec] + [page_spec(i) for i in range(pp)],
            out_specs=pl.BlockSpec((1, nh, pp * ps), lambda b, j, pt: (b, 0, j)),
        ),
        out_shape=jax.ShapeDtypeStruct((nb, nh, length), F32),
        compiler_params=_cp("arbitrary", "arbitrary"),
        name="moba_scores",
    )(page_table, col4(q), *([cache_kt] * pp))

    eht = (jnp.arange(nh)[:, None] == jnp.arange(gw)[None, :] // hd).astype(F32)
    bi = (jnp.arange(length)[:, None] // MOBA_BLOCK == jnp.arange(LANES)[None, :]).astype(BF16)
    slopes = jnp.broadcast_to(
        jnp.array([_alibi_slope(h) for h in range(nh)], F32)[:, None], (nh, LANES))
    row3 = pl.BlockSpec((1, 1, gw), lambda b, j, pt: (b, 0, 0))
    const = lambda x: pl.BlockSpec(x.shape, lambda b, j, pt: (0,) * x.ndim)
    out = pl.pallas_call(
        functools.partial(_moba_pv_kernel, n_steps=n_steps, n_past_blocks=n_past_blocks),
        grid_spec=pltpu.PrefetchScalarGridSpec(
            num_scalar_prefetch=1,
            grid=(nb, n_steps),
            in_specs=[pl.BlockSpec((1, nh, length), lambda b, j, pt: (b, 0, 0)), row3, row3,
                      const(eht), const(bi), const(slopes), col_spec]
                     + [page_spec(i) for i in range(pp)],
            out_specs=col_spec,
            scratch_shapes=[pltpu.VMEM((nh, hd, ps), F32), pltpu.VMEM((n_steps, nh, pp * ps), F32),
                            pltpu.VMEM((nh, LANES), F32)],
        ),
        out_shape=jax.ShapeDtypeStruct((nb, nh, hd, 1), F32),
        compiler_params=_cp("arbitrary", "arbitrary"),
        name="moba_pv",
    )(page_table, scores, q.reshape(nb, 1, gw), kn.reshape(nb, 1, gw), eht, bi, slopes, col4(vn),
      *([cache_vt] * pp))
    return out.reshape(nb, gw)


RWKV_PASSES = (1, 3)


def _shift_layout(shift):
    n = shift.shape[0]
    pad = jnp.zeros((n, LANES - 3 * TM_LORA), F32)
    return jnp.concatenate([shift, pad], axis=1)


def _layer_params(l, w_in, w_out, w_ffn_in, w_ffn_out, tm_mu, tm_w0, tm_w_up, tm_a0, tm_a_up, tm_g_up,
                  tm_k_k, tm_k_a, tm_r_k, tm_ln_g, tm_ln_b, sg_ws, sg_bs, sg_ln_g, sg_ln_b,
                  cv_w, cv_b, cv_ln_g, cv_ln_b):
    gw, lr, hd = GROUP_WIDTH, TM_LORA, HEAD_DIM
    wi = w_in[l]
    w_in_f32 = jnp.concatenate(
        [wi[:, A_PROJ:], wi[:, :A_PROJ], jnp.zeros((wi.shape[0], IN_PAD - IN_WIDTH), F32)], axis=1)
    mu = tm_mu[l]
    mu_l = jnp.concatenate([mu[3 * gw:], jnp.zeros((gw - 3 * lr,), F32)])
    vec = jnp.stack([mu[0:gw], mu[gw:2 * gw], mu[2 * gw:3 * gw], tm_w0[l], tm_a0[l], tm_k_k[l], tm_k_a[l],
                     tm_r_k[l].reshape(gw), tm_ln_g[l], tm_ln_b[l], mu_l]
                    + [jnp.zeros((gw,), F32)] * 5)
    wup = jnp.zeros((3, LANES, gw), F32)
    wup = wup.at[0, 0:lr].set(tm_w_up[l]).at[1, lr:2 * lr].set(tm_a_up[l]).at[2, 2 * lr:3 * lr].set(tm_g_up[l])
    sg_bias = jnp.repeat(sg_bs[l].T, hd, axis=1)
    sg_ln = jnp.stack([sg_ln_g[l], sg_ln_b[l]] + [jnp.zeros((gw,), F32)] * 6)
    sgp = jnp.stack([sg_ln_g[l], sg_ln_b[l], jnp.repeat(sg_ws[l][:, 0, 0], hd), jnp.repeat(sg_bs[l][:, 0], hd)]
                    + [jnp.zeros((gw,), F32)] * 4)
    cw = jnp.concatenate([cv_w[l], jnp.zeros((1, gw), F32)], axis=0)
    cvec = jnp.stack([cv_b[l], cv_ln_g[l], cv_ln_b[l]] + [jnp.zeros((gw,), F32)] * 5)
    return dict(vec=vec, wup=wup, sg_w=sg_ws[l], sg_bias=sg_bias, sg_ln=sg_ln, sgp=sgp, cw=cw, cvec=cvec,
                w_in_f32=w_in_f32[None], dense_f32=(w_out, w_ffn_in, w_ffn_out))


def _prompt_dense_tail(x, ys, mod, p, g_ffn, layer):
    ng, t, d = x.shape
    m = ng * t
    w_out, w_ffn_in, w_ffn_out = p['dense_f32']
    x1 = _mm(ys, w_out, layer, 1024, 512, res=x.reshape(m, d), gate=mod, gate_col=2, rows_per_group=t)
    h2 = _norm_mod(x1.reshape(ng, t, d), g_ffn, mod, 3, 4, 512).reshape(m, d)
    hid = _mm_swiglu(h2, w_ffn_in, layer, 1024, 512)
    x2 = _mm_rows([hid], w_ffn_out[layer].astype(BF16), 512, 512, res=x1, gate=mod, gate_col=5, rows_per_group=t)
    return x2.reshape(ng, t, d)


def _prompt_layer(x, mod, p, g_mix, g_ffn, e, layer):
    nb, t, d = x.shape
    m = nb * t
    gw, nh, hd = GROUP_WIDTH, N_HEADS, HEAD_DIM
    h = _norm_mod(x, g_mix, mod, 0, 1, 512).reshape(m, d)
    proj = _mm([h], p['w_in_f32'], 0, 1024, 512)
    shift0 = jnp.zeros((nb, 1, 3 * gw + LANES), F32)
    s0 = jnp.zeros((nb, nh, hd, hd), F32)
    ya, wkv = _rwkv_prompt(proj, nb, t, shift0, s0, p['vec'], p['wup'], e, RWKV_PASSES)
    yb = _moba_prompt(proj, nb, t)
    yc = _sgu_prompt(proj, m, p['sg_w'], p['sg_bias'], p['sg_ln'])
    yd, hist = _conv_prompt(proj, nb, t, jnp.zeros((nb, CONV_HALO, gw), F32), p['cw'], p['cvec'])
    x2 = _prompt_dense_tail(x, [ya.reshape(m, gw), yb, yc, yd], mod, p, g_ffn, layer)
    proj3 = proj.reshape(nb, t, IN_PAD)
    last = proj3[:, t - 1]
    shift = jnp.concatenate([last[:, CB_R * gw:CB_L * gw], last[:, CB_L * gw:CB_L * gw + 3 * TM_LORA]], axis=1)
    new_k = proj3[:, :, CB_K * gw:(CB_K + 1) * gw].reshape(nb, t, nh, hd)
    new_v = proj3[:, :, CB_V * gw:(CB_V + 1) * gw].reshape(nb, t, nh, hd)
    return x2, (shift, wkv, new_k, new_v, hist[:, CONV_HALO - (CONV_WIDTH - 1):])


def _sample_layer(x, mod, p, g_mix, g_ffn, e, layer, shift_prev, wkv_prev, conv_prev, cache_k, cache_v,
                  page_table):
    nb, t, d = x.shape
    gw, nh, hd = GROUP_WIDTH, N_HEADS, HEAD_DIM
    xg = x.reshape(1, nb, d)
    h = _norm_mod(xg, g_mix, mod, 0, 1, nb, dtype=F32).reshape(nb, d)
    proj = _mm_precise([h], p['w_in_f32'], 0, 512)
    ya, wkv, yc, sgv, yd, conv_new_t = _sample_mixers(
        proj, _shift_layout(shift_prev), wkv_prev, conv_prev.transpose(1, 0, 2),
        p['vec'], p['wup'], e, p['sgp'], p['cw'], p['cvec'])
    yb = _moba_sample(proj, cache_k, cache_v, layer, page_table)
    w_out, w_ffn_in, w_ffn_out = p['dense_f32']
    x1 = _mm_precise([ya, yb, yc, yd], w_out, layer, 512, mode="gated", res=x.reshape(nb, d), gate=mod, gate_col=2)
    h2 = _norm_mod(x1.reshape(1, nb, d), g_ffn, mod, 3, 4, nb, dtype=F32).reshape(nb, d)
    hid = _mm_precise([h2], w_ffn_in, layer, 512, mode="swiglu")
    x2 = _mm_precise([hid], w_ffn_out, layer, 256, mode="gated", res=x1, gate=mod, gate_col=5)
    shift = jnp.concatenate([proj[:, CB_R * gw:CB_L * gw], proj[:, CB_L * gw:CB_L * gw + 3 * TM_LORA]], axis=1)
    new_k = proj[:, CB_K * gw:(CB_K + 1) * gw].reshape(nb, 1, nh, hd)
    new_v = proj[:, CB_V * gw:(CB_V + 1) * gw].reshape(nb, 1, nh, hd)
    return x2.reshape(nb, t, d), (shift, wkv, new_k, new_v, sgv.reshape(nb, 1, gw), conv_new_t.transpose(1, 0, 2))


def kernel(x_prompt, x_sample, c_prompt, c_sample, cache_k_b, cache_v_b, state_a_wkv, state_a_shift, state_d_conv, page_table, w_ada, b_ada, g_norm_mix, w_in, w_out, tm_mu, tm_w0, tm_w_up, tm_a0, tm_a_up, tm_g_up, tm_k_k, tm_k_a, tm_r_k, tm_ln_g, tm_ln_b, sg_ws, sg_bs, sg_ln_g, sg_ln_b, cv_w, cv_b, cv_ln_g, cv_ln_b, g_norm_ffn, w_ffn_in, w_ffn_out, g_norm_final):
    depth = w_in.shape[0]
    bp, ns = x_prompt.shape[0], x_sample.shape[0]
    gw = GROUP_WIDTH
    rows = bp + ns
    rows_pad = -(-rows // 8) * 8
    c_all = jnp.concatenate([c_prompt, c_sample, jnp.zeros((rows_pad - rows, c_prompt.shape[1]), F32)], axis=0)
    mod = _ada(c_all, w_ada, b_ada)
    e = _head_ones()
    xp, xs = x_prompt, x_sample
    st_p, st_s = [], []
    for l in range(depth):
        p = _layer_params(l, w_in, w_out, w_ffn_in, w_ffn_out, tm_mu, tm_w0, tm_w_up, tm_a0, tm_a_up, tm_g_up,
                          tm_k_k, tm_k_a, tm_r_k, tm_ln_g, tm_ln_b, sg_ws, sg_bs, sg_ln_g, sg_ln_b,
                          cv_w, cv_b, cv_ln_g, cv_ln_b)
        mod_p = mod[l, :bp][:, None, :]
        mod_s = mod[l, bp:rows][None]
        xp, sp = _prompt_layer(xp, mod_p, p, g_norm_mix[l], g_norm_ffn[l], e, l)
        xs, ss = _sample_layer(xs, mod_s, p, g_norm_mix[l], g_norm_ffn[l], e, l, state_a_shift[l],
                               state_a_wkv[l], state_d_conv[l], cache_k_b, cache_v_b, page_table)
        st_p.append(sp)
        st_s.append(ss)
    y_prompt = _norm(xp, g_norm_final, 512)
    y_sample = _norm(xs.reshape(1, ns, -1), g_norm_final, ns).reshape(xs.shape)
    stack = lambda sts, i: jnp.stack([s[i] for s in sts])
    return (y_prompt, y_sample, stack(st_p, 2), stack(st_p, 3), stack(st_s, 2), stack(st_s, 3),
            stack(st_p, 1), stack(st_s, 1), stack(st_p, 0), stack(st_s, 0), stack(st_p, 4), stack(st_s, 5),
            stack(st_s, 4))
```

```python
import functools

import jax
import jax.numpy as jnp
from jax import lax
from jax.experimental import pallas as pl
from jax.experimental.pallas import tpu as pltpu

F32, BF16 = jnp.float32, jnp.bfloat16

D_MODEL = 2048
GROUP_WIDTH = 512
N_HEADS = 8
HEAD_DIM = 64
TM_LORA = 32
A_PROJ = 3 * GROUP_WIDTH + 3 * TM_LORA
IN_WIDTH = A_PROJ + 7 * GROUP_WIDTH
IN_PAD = 11 * GROUP_WIDTH
MOBA_BLOCK = 256
MOBA_TOPK = 3
SG_CHUNK = 128
CONV_WIDTH = 31
CONV_HALO = 32
RMS_EPS = 1e-6
LN_EPS = 1e-5
GN_EPS = 64e-5
RWKV_CHUNK = 64
RWKV_CHUNKS_PER_STEP = 4
INV_BLOCK = 16
LANES = 128

CB_Q, CB_K, CB_V, CB_SU, CB_SV, CB_CA, CB_CG, CB_R, CB_RK, CB_RV, CB_L = range(11)
(V_MU_R, V_MU_K, V_MU_V, V_W0, V_A0, V_KK, V_KA, V_RK, V_LNG, V_LNB, V_MU_L) = range(11)

VMEM_LIMIT = 56 * 1024 * 1024


def _cp(*sem):
    return pltpu.CompilerParams(dimension_semantics=sem, vmem_limit_bytes=VMEM_LIMIT)


def _dot(a, b):
    return jnp.dot(a, b, preferred_element_type=F32)


def _split2(x):
    hi = x.astype(BF16)
    return hi, (x - hi.astype(F32)).astype(BF16)


def _split3(x):
    h1 = x.astype(BF16)
    r1 = x - h1.astype(F32)
    h2 = r1.astype(BF16)
    return h1, h2, (r1 - h2.astype(F32)).astype(BF16)


def _dot_exact_rhs(x, e):
    hi, lo = _split2(x)
    return _dot(hi, e) + _dot(lo, e)


def _head_ones():
    r = lax.broadcasted_iota(jnp.int32, (GROUP_WIDTH, GROUP_WIDTH), 0) // HEAD_DIM
    c = lax.broadcasted_iota(jnp.int32, (GROUP_WIDTH, GROUP_WIDTH), 1) // HEAD_DIM
    return (r == c).astype(BF16)


def _layer_norm(x, g, b):
    mu = jnp.mean(x, axis=-1, keepdims=True)
    d = x - mu
    var = jnp.mean(d * d, axis=-1, keepdims=True)
    return d * lax.rsqrt(var + LN_EPS) * g + b


def _softplus(x):
    return jnp.maximum(x, 0.0) + jnp.log1p(jnp.exp(-jnp.abs(x)))


def _dot3(x, w):
    rows = x.shape[0]
    x_hi = x.astype(BF16).astype(F32)
    xs = jnp.concatenate([x_hi, x - x_hi], axis=0).astype(BF16)
    wh, wl = _split2(w)
    r = _dot(xs, wh)
    return r[:rows] + r[rows:] + _dot(x_hi.astype(BF16), wl)


def _ada_kernel(c_ref, w_ref, b_ref, o_ref):
    c = c_ref[...]
    o_ref[0] = _dot3(c * jax.nn.sigmoid(c), w_ref[0]) + b_ref[0]


def _ada(c_all, w_ada, b_ada):
    depth, d, n = w_ada.shape
    r = c_all.shape[0]
    tn = 512
    return pl.pallas_call(
        _ada_kernel,
        grid=(depth, n // tn),
        in_specs=[pl.BlockSpec((r, d), lambda l, j: (0, 0)),
                  pl.BlockSpec((1, d, tn), lambda l, j: (l, 0, j)),
                  pl.BlockSpec((1, 1, tn), lambda l, j: (l, 0, j))],
        out_specs=pl.BlockSpec((1, r, tn), lambda l, j: (l, 0, j)),
        out_shape=jax.ShapeDtypeStruct((depth, r, n), F32),
        compiler_params=_cp("arbitrary", "arbitrary"),
        name="ada",
    )(c_all, w_ada, b_ada.reshape(depth, 1, n))


def _norm_mod_kernel(x_ref, g_ref, sc_ref, sh_ref, o_ref):
    x = x_ref[0]
    h = x * lax.rsqrt(jnp.mean(x * x, axis=-1, keepdims=True) + RMS_EPS) * g_ref[...]
    o_ref[0] = (h * (1.0 + sc_ref[0]) + sh_ref[0]).astype(o_ref.dtype)


def _norm_mod(x, g, mod, sh_col, sc_col, tt, dtype=BF16):
    ng, t, d = x.shape
    r = mod.shape[1]
    return pl.pallas_call(
        _norm_mod_kernel,
        grid=(ng, t // tt),
        in_specs=[pl.BlockSpec((1, tt, d), lambda b, i: (b, i, 0)),
                  pl.BlockSpec((1, d), lambda b, i: (0, 0)),
                  pl.BlockSpec((1, r, d), lambda b, i: (b, 0, sc_col)),
                  pl.BlockSpec((1, r, d), lambda b, i: (b, 0, sh_col))],
        out_specs=pl.BlockSpec((1, tt, d), lambda b, i: (b, i, 0)),
        out_shape=jax.ShapeDtypeStruct((ng, t, d), dtype),
        compiler_params=_cp("arbitrary", "arbitrary"),
        name="norm_mod",
    )(x, g.reshape(1, d), mod, mod)


def _norm_kernel(x_ref, g_ref, o_ref):
    x = x_ref[0]
    o_ref[0] = x * lax.rsqrt(jnp.mean(x * x, axis=-1, keepdims=True) + RMS_EPS) * g_ref[...]


def _norm(x, g, tt):
    ng, t, d = x.shape
    return pl.pallas_call(
        _norm_kernel,
        grid=(ng, t // tt),
        in_specs=[pl.BlockSpec((1, tt, d), lambda b, i: (b, i, 0)),
                  pl.BlockSpec((1, d), lambda b, i: (0, 0))],
        out_specs=pl.BlockSpec((1, tt, d), lambda b, i: (b, i, 0)),
        out_shape=jax.ShapeDtypeStruct((ng, t, d), F32),
        compiler_params=_cp("arbitrary", "arbitrary"),
        name="norm_final",
    )(x, g.reshape(1, d))


def _mm_kernel(*refs, n_x, gated):
    xs, w_ref = refs[:n_x], refs[n_x]
    o_ref, wb = refs[-2], refs[-1]

    @pl.when(pl.program_id(1) == 0)
    def _():
        wb[...] = w_ref[0].astype(BF16)

    acc = None
    k0 = 0
    for x_ref in xs:
        k = x_ref.shape[1]
        part = _dot(x_ref[...], wb[k0:k0 + k, :])
        acc = part if acc is None else acc + part
        k0 += k
    if gated:
        res_ref, gt_ref = refs[n_x + 1], refs[n_x + 2]
        o_ref[...] = res_ref[...] + gt_ref[0] * acc
    else:
        o_ref[...] = acc


def _mm(xs, w, layer, tm, tn, res=None, gate=None, gate_col=0, rows_per_group=None):
    m = xs[0].shape[0]
    kdim, n = w.shape[1], w.shape[2]
    gated = res is not None
    in_specs = [pl.BlockSpec((tm, x.shape[1]), lambda j, i: (i, 0)) for x in xs]
    in_specs.append(pl.BlockSpec((1, kdim, tn), lambda j, i: (layer, 0, j)))
    args = list(xs) + [w]
    if gated:
        r = gate.shape[1]
        tiles_per_group = rows_per_group // tm
        nj = n // tn
        in_specs.append(pl.BlockSpec((tm, tn), lambda j, i: (i, j)))
        in_specs.append(pl.BlockSpec((1, r, tn), lambda j, i: (i // tiles_per_group, 0, gate_col * nj + j)))
        args += [res, gate]
    return pl.pallas_call(
        functools.partial(_mm_kernel, n_x=len(xs), gated=gated),
        grid=(n // tn, m // tm),
        in_specs=in_specs,
        out_specs=pl.BlockSpec((tm, tn), lambda j, i: (i, j)),
        out_shape=jax.ShapeDtypeStruct((m, n), F32),
        scratch_shapes=[pltpu.VMEM((kdim, tn), BF16)],
        compiler_params=_cp("arbitrary", "arbitrary"),
        name="mm_gated" if gated else "mm",
    )(*args)


def _mm_rows_kernel(*refs, n_x, gated):
    xs, w_ref = refs[:n_x], refs[n_x]
    o_ref = refs[-1]
    acc = None
    k0 = 0
    for x_ref in xs:
        k = x_ref.shape[1]
        part = _dot(x_ref[...], w_ref[k0:k0 + k, :])
        acc = part if acc is None else acc + part
        k0 += k
    if gated:
        res_ref, gt_ref = refs[n_x + 1], refs[n_x + 2]
        o_ref[...] = res_ref[...] + gt_ref[0] * acc
    else:
        o_ref[...] = acc


def _mm_rows(xs, w, tm, tn, res=None, gate=None, gate_col=0, rows_per_group=None):
    m = xs[0].shape[0]
    n = w.shape[1]
    gated = res is not None
    in_specs = [pl.BlockSpec((tm, x.shape[1]), lambda i, j: (i, 0)) for x in xs]
    in_specs.append(pl.BlockSpec((w.shape[0], tn), lambda i, j: (0, j)))
    args = list(xs) + [w]
    if gated:
        r = gate.shape[1]
        tiles_per_group = rows_per_group // tm
        nj = n // tn
        in_specs.append(pl.BlockSpec((tm, tn), lambda i, j: (i, j)))
        in_specs.append(pl.BlockSpec((1, r, tn), lambda i, j: (i // tiles_per_group, 0, gate_col * nj + j)))
        args += [res, gate]
    return pl.pallas_call(
        functools.partial(_mm_rows_kernel, n_x=len(xs), gated=gated),
        grid=(m // tm, n // tn),
        in_specs=in_specs,
        out_specs=pl.BlockSpec((tm, tn), lambda i, j: (i, j)),
        out_shape=jax.ShapeDtypeStruct((m, n), F32),
        compiler_params=_cp("arbitrary", "arbitrary"),
        name="mm_rows_gated" if gated else "mm_rows",
    )(*args)


def _mm_precise_kernel(*refs, n_x, mode):
    xs = refs[:n_x]
    o_ref = refs[-1]
    if mode == "swiglu":
        x = xs[0][...]
        g = _dot3(x, refs[n_x][0])
        o_ref[...] = g * jax.nn.sigmoid(g) * _dot3(x, refs[n_x + 1][0])
        return
    w_ref = refs[n_x]
    acc = None
    k0 = 0
    for x_ref in xs:
        k = x_ref.shape[1]
        part = _dot3(x_ref[...], w_ref[0, k0:k0 + k, :])
        acc = part if acc is None else acc + part
        k0 += k
    if mode == "gated":
        res_ref, gt_ref = refs[n_x + 1], refs[n_x + 2]
        o_ref[...] = res_ref[...] + gt_ref[0] * acc
    else:
        o_ref[...] = acc


def _mm_precise(xs, w, layer, tn, mode="plain", res=None, gate=None, gate_col=0):
    m = xs[0].shape[0]
    kdim = w.shape[1]
    n = w.shape[2] // 2 if mode == "swiglu" else w.shape[2]
    nj = n // tn
    in_specs = [pl.BlockSpec((m, x.shape[1]), lambda j: (0, 0)) for x in xs]
    in_specs.append(pl.BlockSpec((1, kdim, tn), lambda j: (layer, 0, j)))
    args = list(xs) + [w]
    if mode == "swiglu":
        in_specs.append(pl.BlockSpec((1, kdim, tn), lambda j: (layer, 0, nj + j)))
        args.append(w)
    if mode == "gated":
        in_specs.append(pl.BlockSpec((m, tn), lambda j: (0, j)))
        in_specs.append(pl.BlockSpec((1, m, tn), lambda j: (0, 0, gate_col * nj + j)))
        args += [res, gate]
    return pl.pallas_call(
        functools.partial(_mm_precise_kernel, n_x=len(xs), mode=mode),
        grid=(nj,),
        in_specs=in_specs,
        out_specs=pl.BlockSpec((m, tn), lambda j: (0, j)),
        out_shape=jax.ShapeDtypeStruct((m, n), F32),
        compiler_params=_cp("arbitrary"),
        name="mm_precise_" + mode,
    )(*args)


def _swiglu_kernel(x_ref, wg_ref, wu_ref, o_ref, wgb, wub):
    @pl.when(pl.program_id(1) == 0)
    def _():
        wgb[...] = wg_ref[0].astype(BF16)
        wub[...] = wu_ref[0].astype(BF16)

    x = x_ref[...]
    g = _dot(x, wgb[...])
    u = _dot(x, wub[...])
    o_ref[...] = (g * jax.nn.sigmoid(g) * u).astype(o_ref.dtype)


def _mm_swiglu(x, w, layer, tm, tn):
    m, k = x.shape
    nh = w.shape[2] // 2
    nj = nh // tn
    return pl.pallas_call(
        _swiglu_kernel,
        grid=(nj, m // tm),
        in_specs=[pl.BlockSpec((tm, k), lambda j, i: (i, 0)),
                  pl.BlockSpec((1, k, tn), lambda j, i: (layer, 0, j)),
                  pl.BlockSpec((1, k, tn), lambda j, i: (layer, 0, nj + j))],
        out_specs=pl.BlockSpec((tm, tn), lambda j, i: (i, j)),
        out_shape=jax.ShapeDtypeStruct((m, nh), BF16),
        scratch_shapes=[pltpu.VMEM((k, tn), BF16), pltpu.VMEM((k, tn), BF16)],
        compiler_params=_cp("arbitrary", "arbitrary"),
        name="mm_swiglu",
    )(x, w, w)


def _rwkv_elem(pr, pk, pv, plo, qr, qk, qv, qlo, vec, wup_ref, e, precise=False):
    row = lambda i: vec[i:i + 1, :]
    xr = pr + (qr - pr) * row(V_MU_R)
    xk = pk + (qk - pk) * row(V_MU_K)
    xv = pv + (qv - pv) * row(V_MU_V)
    xl = plo + (qlo - plo) * vec[V_MU_L:V_MU_L + 1, 0:LANES]
    up = _dot3 if precise else (lambda x, w: _dot(x.astype(BF16), w.astype(BF16)))
    lw = up(jnp.tanh(xl), wup_ref[0])
    la = up(xl, wup_ref[1])
    g = up(jax.nn.sigmoid(xl), wup_ref[2])
    w_log = -_softplus(-(row(V_W0) + lw)) - 0.5
    log_decay = -jnp.exp(w_log)
    a = jax.nn.sigmoid(row(V_A0) + la)
    kk = xk * row(V_KK)
    kk = kk / jnp.maximum(jnp.sqrt(_dot_exact_rhs(kk * kk, e)), 1e-12)
    kmod = xk * (1.0 + (a - 1.0) * row(V_KA))
    bonus = _dot_exact_rhs(xr * kmod * row(V_RK), e) * xv
    return xr, log_decay, kmod, xv, kk, kk * a, g, bonus


def _rwkv_prep(pr_ref, pk_ref, pv_ref, pl_ref, sp_ref, vec_ref, wup_ref, e_ref, head_refs, carry):
    rt_ref, at_ref, bt_ref, kt_ref, v_ref, bh_ref, kh_ref, gt_ref = head_refs
    ti = pl.program_id(1)

    @pl.when(ti == 0)
    def _():
        carry[...] = sp_ref[0]

    tt = pr_ref.shape[0]
    first = lax.broadcasted_iota(jnp.int32, (tt, 1), 0) == 0
    gw = GROUP_WIDTH
    cur = (pr_ref[...], pk_ref[...], pv_ref[...], pl_ref[...])
    offs = (0, gw, 2 * gw, 3 * gw)
    prev = []
    for p, o in zip(cur, offs):
        w = p.shape[1]
        prev.append(jnp.where(first, carry[:, o:o + w], pltpu.roll(p, 1, axis=0)))
        carry[:, o:o + w] = p[tt - 1:tt, :]
    vec = vec_ref[...]
    e = e_ref[...]
    r, lw, kmod, v, kk, b, g, bonus = _rwkv_elem(*cur, *prev, vec, wup_ref, e)

    ri = lax.broadcasted_iota(jnp.int32, (tt, tt), 0)
    ci = lax.broadcasted_iota(jnp.int32, (tt, tt), 1)
    same = (ri // RWKV_CHUNK) == (ci // RWKV_CHUNK)
    tri = (same & (ci <= ri)).astype(BF16)
    blk = same.astype(BF16)
    parts = _split3(lw)
    cum = sum(_dot(tri, p) for p in parts)
    tot = sum(_dot(blk, p) for p in parts)
    e_out = jnp.exp(-cum)
    to_end = jnp.exp(tot - cum)
    outs = ((rt_ref, r * jnp.exp(cum)), (at_ref, -kk * jnp.exp(cum - lw)), (bt_ref, b * e_out),
            (kt_ref, kmod * e_out), (v_ref, v), (bh_ref, b * to_end), (kh_ref, kmod * to_end),
            (gt_ref, jnp.exp(tot)))
    for ref, val in outs:
        for h in range(N_HEADS):
            ref[h] = val[:, h * HEAD_DIM:(h + 1) * HEAD_DIM]
    return g, bonus


def _bmm(a, b, ca, cb, passes):
    dn = (((ca,), (cb,)), ((0,), (0,)))
    d = lambda x, y: lax.dot_general(x, y, dn, preferred_element_type=F32)
    if passes == 1:
        return d(a.astype(BF16), b.astype(BF16))
    ah, al = _split2(a)
    bh, bl = _split2(b)
    return d(ah, bh) + d(ah, bl) + d(al, bh)


def _rwkv_chunks(head_refs, s0_ref, o_ref, st_ref, s_scr, passes):
    rt_ref, at_ref, bt_ref, kt_ref, v_ref, bh_ref, kh_ref, gt_ref = head_refs
    c = pl.program_id(1)

    @pl.when(c == 0)
    def _():
        s_scr[...] = s0_ref[0]

    n = RWKV_CHUNK
    nc = rt_ref.shape[1] // n
    shp = (1, n, n)
    ti = lax.broadcasted_iota(jnp.int32, shp, 1)
    tj = lax.broadcasted_iota(jnp.int32, shp, 2)
    strict, incl = tj < ti, tj <= ti
    same = (ti // INV_BLOCK) == (tj // INV_BLOCK)
    nn = lambda a, b, p: _bmm(a, b, 2, 1, p)
    nt = lambda a, b, p: _bmm(a, b, 2, 2, p)
    tn = lambda a, b, p: _bmm(a, b, 1, 1, p)
    p_gen, p_chain = passes

    per_chunk = lambda ref: ref[...].reshape(N_HEADS * nc, n, HEAD_DIM)
    rt, at, bt, kt, v, bh, kh = (per_chunk(r) for r in (rt_ref, at_ref, bt_ref, kt_ref, v_ref, bh_ref, kh_ref))
    g_end = per_chunk(gt_ref)[:, 0:1, :]
    big = nt(jnp.concatenate([at, rt], axis=1), jnp.concatenate([bt, kt], axis=1), p_gen)
    a_ab = jnp.where(strict, big[:, :n, :n], 0.0)
    a_ak = jnp.where(strict, big[:, :n, n:], 0.0)
    a_rb = jnp.where(incl, big[:, n:, :n], 0.0)
    a_rk = jnp.where(incl, big[:, n:, n:], 0.0)
    av = nn(jnp.concatenate([a_ak, a_rk], axis=1), v, p_gen)
    nd = jnp.where(same, a_ab, 0.0)
    no = jnp.where(same, 0.0, a_ab)
    nd2 = nn(nd, nd, p_gen)
    nd4 = nn(nd2, nd2, p_gen)
    nd8 = nn(nd4, nd4, p_gen)
    y = jnp.concatenate([no, at, av[:, :n]], axis=2)
    for pw in (nd8, nd4, nd2, nd):
        y = y + nn(pw, y, p_gen)
    m = y[:, :, :n]
    z = y[:, :, n:]
    m2 = nn(m, m, p_gen)
    z = z + nn(m2, z, p_gen)
    z = z + nn(m, z, p_gen)
    qo = jnp.concatenate([rt, av[:, n:]], axis=2) + nn(a_rb, z, p_gen)
    pg = tn(z, bh, p_gen)
    gm = pg[:, n:] + tn(v, kh, p_gen)

    by_chunk = lambda x: x.reshape((N_HEADS, nc) + x.shape[1:])
    qo, pmat, gm, g_end = by_chunk(qo), by_chunk(pg[:, :n]), by_chunk(gm), by_chunk(g_end)
    s = s_scr[...]
    for i in range(nc):
        o_ref[:, i * n:(i + 1) * n, :] = nt(qo[:, i, :, :HEAD_DIM], s, p_chain) + qo[:, i, :, HEAD_DIM:]
        s = s * g_end[:, i] + nn(s, pmat[:, i], p_chain) + gm[:, i]
    s_scr[...] = s

    @pl.when(c == pl.num_programs(1) - 1)
    def _():
        st_ref[0] = s


def _rwkv_prompt_kernel(pr_ref, pk_ref, pv_ref, pl_ref, sp_ref, vec_ref, wup_ref, e_ref, s0_ref,
                        ya_ref, st_ref, carry, s_scr, o_scr, *head_scr, passes):
    g, bonus = _rwkv_prep(pr_ref, pk_ref, pv_ref, pl_ref, sp_ref, vec_ref, wup_ref, e_ref, head_scr, carry)
    _rwkv_chunks(head_scr, s0_ref, o_scr, st_ref, s_scr, passes)
    o = jnp.concatenate([o_scr[h] for h in range(N_HEADS)], axis=-1)
    e = e_ref[...]
    vec = vec_ref[...]
    mu = _dot_exact_rhs(o, e) * (1.0 / HEAD_DIM)
    d = o - mu
    var = _dot_exact_rhs(d * d, e) * (1.0 / HEAD_DIM)
    on = d * lax.rsqrt(var + GN_EPS) * vec[V_LNG:V_LNG + 1, :] + vec[V_LNB:V_LNB + 1, :]
    ya_ref[0] = ((on + bonus) * g).astype(ya_ref.dtype)


def _rwkv_prompt(proj, nb, t, shift_prev, s0, vec, wup, e, passes):
    tt = RWKV_CHUNKS_PER_STEP * RWKV_CHUNK
    nt_ = t // tt
    gw, nh, hd = GROUP_WIDTH, N_HEADS, HEAD_DIM
    pspec = lambda cb: pl.BlockSpec((tt, gw), lambda b, i: (b * nt_ + i, cb))
    const = lambda shape: pl.BlockSpec(shape, lambda b, i: (0,) * len(shape))
    sspec = pl.BlockSpec((1, nh, hd, hd), lambda b, i: (b, 0, 0, 0))
    head_major = pltpu.VMEM((nh, tt, hd), F32)
    return pl.pallas_call(
        functools.partial(_rwkv_prompt_kernel, passes=passes),
        grid=(nb, nt_),
        in_specs=[pspec(CB_R), pspec(CB_RK), pspec(CB_RV),
                  pl.BlockSpec((tt, LANES), lambda b, i: (b * nt_ + i, CB_L * (gw // LANES))),
                  pl.BlockSpec((1, 1, 3 * gw + LANES), lambda b, i: (b, 0, 0)),
                  const(vec.shape), const(wup.shape), const(e.shape), sspec],
        out_specs=[pl.BlockSpec((1, tt, gw), lambda b, i: (b, i, 0)), sspec],
        out_shape=[jax.ShapeDtypeStruct((nb, t, gw), BF16), jax.ShapeDtypeStruct((nb, nh, hd, hd), F32)],
        scratch_shapes=[pltpu.VMEM((1, 3 * gw + LANES), F32), pltpu.VMEM((nh, hd, hd), F32), head_major]
                       + [head_major] * 8,
        compiler_params=_cp("arbitrary", "arbitrary"),
        name="rwkv_prompt",
    )(proj, proj, proj, proj, shift_prev, vec, wup, e, s0)


def _alibi_slope(h):
    return 2.0 ** (-8.0 * (h + 1) / N_HEADS)


LOG2E = 1.4426950408889634
MASKED = -1e30


def _eye(n):
    return (lax.broadcasted_iota(jnp.int32, (n, n), 0) == lax.broadcasted_iota(jnp.int32, (n, n), 1)).astype(BF16)


def _moba_prompt_kernel(q_ref, k_ref, v_ref, selm_ref, slope_ref, o_ref, kh_scr, vt_scr, km_scr, selt_scr, m_scr,
                        acc_scr, *, n_blocks):
    qi = pl.program_id(1)
    blk, nh, hd = MOBA_BLOCK, N_HEADS, HEAD_DIM
    nt_dims = (((1,), (1,)), ((), ()))

    @pl.when(qi == 0)
    def _():
        k = k_ref[...]
        v = v_ref[...]
        one_row = (lax.broadcasted_iota(jnp.int32, (hd, blk), 0) == 0).astype(BF16)
        eye_hd = _eye(hd)
        for h in range(nh):
            kh_scr[h] = k[:, h * hd:(h + 1) * hd].astype(BF16)
            vt = lax.dot_general(eye_hd, v[:, h * hd:(h + 1) * hd].astype(BF16), nt_dims,
                                 preferred_element_type=F32).astype(BF16)
            for n in range(n_blocks):
                vt_scr[n, h] = jnp.concatenate([vt[:, n * blk:(n + 1) * blk], one_row], axis=0)
        rows = [jnp.mean(k[n * blk:(n + 1) * blk, :], axis=0, keepdims=True) for n in range(n_blocks)]
        rows += [jnp.zeros((1, k.shape[1]), F32)] * (selm_ref.shape[0] - n_blocks)
        kmean = jnp.concatenate(rows, axis=0)
        tn_dims = (((0,), (0,)), ((), ()))
        spread = sum(lax.dot_general(p, selm_ref[...], tn_dims, preferred_element_type=F32)
                     for p in _split3(kmean))
        rh = lax.broadcasted_iota(jnp.int32, spread.shape, 0) // hd
        ch = lax.broadcasted_iota(jnp.int32, spread.shape, 1) % nh
        for i, p in enumerate(_split3(jnp.where(rh == ch, spread, 0.0))):
            km_scr[i] = p

    own = qi
    q = q_ref[...]
    q1, q2, q3 = _split3(q * (hd ** -0.5))
    k1, k2, k3 = km_scr[0], km_scr[1], km_scr[2]
    gate = (_dot(q1, k1) + _dot(q1, k2) + _dot(q2, k1)) + (_dot(q1, k3) + _dot(q2, k2) + _dot(q3, k1))
    lane_blk = lax.broadcasted_iota(jnp.int32, gate.shape, 1) // nh
    valid = lane_blk < own
    gate = jnp.where(valid, gate, -jnp.inf)
    rank = jnp.zeros_like(gate)
    for d in range(1, n_blocks):
        earlier = pltpu.roll(gate, nh * d, axis=1)
        later = pltpu.roll(gate, LANES - nh * d, axis=1)
        rank = rank + jnp.where(earlier >= gate, 1.0, 0.0) + jnp.where(later > gate, 1.0, 0.0)
    sel = jnp.where(valid & (rank < min(MOBA_TOPK, n_blocks - 1)), 1.0, 0.0)
    selt_scr[...] = lax.dot_general(_eye(LANES), sel.astype(BF16), nt_dims, preferred_element_type=F32)

    qa = q * ((hd ** -0.5) * LOG2E)
    qh = jnp.stack([qa[:, h * hd:(h + 1) * hd].astype(BF16) for h in range(nh)])
    ahead_i = (lax.broadcasted_iota(jnp.int32, (blk, blk), 1) - lax.broadcasted_iota(jnp.int32, (blk, blk), 0))
    causal = ahead_i >= 0
    ahead = ahead_i.astype(F32)
    slope2 = slope_ref[...] * LOG2E
    bnt = (((2,), (2,)), ((0,), (0,)))
    bnn = (((2,), (1,)), ((0,), (0,)))

    def scores(first_key):
        k_blk = kh_scr[:, pl.ds(first_key, blk), :]
        s = lax.dot_general(k_blk, qh, bnt, preferred_element_type=F32)
        return [s[h] - (_alibi_slope(h) * LOG2E) * ahead for h in range(nh)]

    tm = jnp.stack([jnp.where(causal, sh, MASKED) for sh in scores(pl.multiple_of(own * blk, blk))])
    m = jnp.max(tm, axis=1, keepdims=True)
    p = jnp.exp2(tm - m)
    m_scr[...] = m
    acc_scr[...] = lax.dot_general(vt_scr[own], p.astype(BF16), bnn, preferred_element_type=F32)

    def past_block(n, carry):
        shift = slope2 * ((own - n) * blk).astype(F32)
        sh = scores(pl.multiple_of(n * blk, blk))
        tm = jnp.stack([jnp.where(selt_scr[pl.ds(n * nh + h, 1), :] > 0.5, sh[h], MASKED) for h in range(nh)])
        m_old = m_scr[...]
        m_new = jnp.maximum(m_old, jnp.max(tm, axis=1, keepdims=True) - shift)
        p = jnp.exp2(tm - (m_new + shift))
        m_scr[...] = m_new
        acc_scr[...] = (jnp.exp2(m_old - m_new) * acc_scr[...]
                        + lax.dot_general(vt_scr[n], p.astype(BF16), bnn, preferred_element_type=F32))
        return carry

    lax.fori_loop(0, own, past_block, 0)
    outs = []
    eye_hd = _eye(hd)
    for h in range(nh):
        acc = acc_scr[h]
        out_t = (acc[0:hd, :] / acc[hd:hd + 1, :]).astype(BF16)
        outs.append(lax.dot_general(out_t, eye_hd, (((0,), (0,)), ((), ())), preferred_element_type=F32))
    o_ref[...] = jnp.concatenate(outs, axis=-1).astype(o_ref.dtype)


def _moba_prompt(proj, nb, t):
    blk, gw, nh, hd = MOBA_BLOCK, GROUP_WIDTH, N_HEADS, HEAD_DIM
    n_blocks = t // blk
    assert n_blocks * nh <= LANES // 2
    selm = (jnp.arange(8)[:, None] == jnp.arange(LANES)[None, :] // nh).astype(BF16)
    slopes = jnp.broadcast_to(jnp.array([_alibi_slope(h) for h in range(nh)], F32)[:, None, None], (nh, 1, blk))
    return pl.pallas_call(
        functools.partial(_moba_prompt_kernel, n_blocks=n_blocks),
        grid=(nb, n_blocks),
        in_specs=[pl.BlockSpec((blk, gw), lambda b, i: (b * n_blocks + i, CB_Q)),
                  pl.BlockSpec((t, gw), lambda b, i: (b, CB_K)),
                  pl.BlockSpec((t, gw), lambda b, i: (b, CB_V)),
                  pl.BlockSpec((8, LANES), lambda b, i: (0, 0)),
                  pl.BlockSpec((nh, 1, blk), lambda b, i: (0, 0, 0))],
        out_specs=pl.BlockSpec((blk, gw), lambda b, i: (b * n_blocks + i, 0)),
        out_shape=jax.ShapeDtypeStruct((nb * t, gw), BF16),
        scratch_shapes=[pltpu.VMEM((nh, t, hd), BF16), pltpu.VMEM((n_blocks, nh, 2 * hd, blk), BF16),
                        pltpu.VMEM((3, gw, LANES), BF16), pltpu.VMEM((LANES, blk), F32),
                        pltpu.VMEM((nh, 1, blk), F32), pltpu.VMEM((nh, 2 * hd, blk), F32)],
        compiler_params=_cp("arbitrary", "arbitrary"),
        name="moba_prompt",
    )(proj, proj, proj, selm, slopes)


def _sgu_prompt_kernel(pu_ref, pv_ref, w_ref, bias_ref, ln_ref, o_ref):
    u = jax.nn.gelu(pu_ref[...])
    v = _layer_norm(jax.nn.gelu(pv_ref[...]), ln_ref[0:1, :], ln_ref[1:2, :]).astype(BF16)
    cl, hd = SG_CHUNK, HEAD_DIM
    ri = lax.broadcasted_iota(jnp.int32, (cl, cl), 0)
    ci = lax.broadcasted_iota(jnp.int32, (cl, cl), 1)
    ws = [jnp.where(ci <= ri, w_ref[h], 0.0).astype(BF16) for h in range(N_HEADS)]
    bias = bias_ref[...]
    for c in range(u.shape[0] // cl):
        vc = v[c * cl:(c + 1) * cl, :]
        mixed = jnp.concatenate([_dot(ws[h], vc[:, h * hd:(h + 1) * hd]) for h in range(N_HEADS)], axis=-1)
        o_ref[c * cl:(c + 1) * cl, :] = (u[c * cl:(c + 1) * cl, :] * (mixed + bias)).astype(o_ref.dtype)


def _sgu_prompt(proj, nrows, w, bias_exp, ln):
    tt, gw = 256, GROUP_WIDTH
    return pl.pallas_call(
        _sgu_prompt_kernel,
        grid=(nrows // tt,),
        in_specs=[pl.BlockSpec((tt, gw), lambda i: (i, CB_SU)),
                  pl.BlockSpec((tt, gw), lambda i: (i, CB_SV)),
                  pl.BlockSpec(w.shape, lambda i: (0, 0, 0)),
                  pl.BlockSpec(bias_exp.shape, lambda i: (0, 0)),
                  pl.BlockSpec(ln.shape, lambda i: (0, 0))],
        out_specs=pl.BlockSpec((tt, gw), lambda i: (i, 0)),
        out_shape=jax.ShapeDtypeStruct((nrows, gw), BF16),
        compiler_params=_cp("arbitrary"),
        name="sgu_prompt",
    )(proj, proj, w, bias_exp, ln)


def _conv_prompt_kernel(pa_ref, pg_ref, prev_ref, w_ref, vec_ref, o_ref, hist_ref, zf):
    ti = pl.program_id(1)
    tt = pa_ref.shape[0]
    halo = CONV_HALO

    @pl.when(ti == 0)
    def _():
        zf[0:halo, :] = prev_ref[0]

    z = pa_ref[...] * jax.nn.sigmoid(pg_ref[...])
    zf[halo:halo + tt, :] = z
    lead = halo - (CONV_WIDTH - 1)
    acc = jnp.zeros_like(z)
    for i in range(CONV_WIDTH):
        acc = acc + w_ref[i:i + 1, :] * zf[pl.ds(lead + i, tt), :]
    y = _layer_norm(acc + vec_ref[0:1, :], vec_ref[1:2, :], vec_ref[2:3, :])
    o_ref[...] = (y * jax.nn.sigmoid(y)).astype(o_ref.dtype)
    tail = zf[pl.ds(tt, halo), :]
    hist_ref[0] = tail
    zf[0:halo, :] = tail


def _conv_prompt(proj, nb, t, prev, w, vec):
    tt, gw = 256, GROUP_WIDTH
    nt_ = t // tt
    return pl.pallas_call(
        _conv_prompt_kernel,
        grid=(nb, nt_),
        in_specs=[pl.BlockSpec((tt, gw), lambda b, i: (b * nt_ + i, CB_CA)),
                  pl.BlockSpec((tt, gw), lambda b, i: (b * nt_ + i, CB_CG)),
                  pl.BlockSpec((1, CONV_HALO, gw), lambda b, i: (b, 0, 0)),
                  pl.BlockSpec(w.shape, lambda b, i: (0, 0)),
                  pl.BlockSpec(vec.shape, lambda b, i: (0, 0))],
        out_specs=[pl.BlockSpec((tt, gw), lambda b, i: (b * nt_ + i, 0)),
                   pl.BlockSpec((1, CONV_HALO, gw), lambda b, i: (b, 0, 0))],
        out_shape=[jax.ShapeDtypeStruct((nb * t, gw), BF16),
                   jax.ShapeDtypeStruct((nb, CONV_HALO, gw), F32)],
        scratch_shapes=[pltpu.VMEM((CONV_HALO + tt, gw), F32)],
        compiler_params=_cp("arbitrary", "arbitrary"),
        name="conv_prompt",
    )(proj, proj, prev, w, vec)


def _sample_elem_kernel(pr_ref, pk_ref, pv_ref, pl_ref, sp_ref, vec_ref, wup_ref, e_ref,
                        su_ref, sv_ref, sgp_ref, ca_ref, cg_ref, cprev_ref, cw_ref, cvec_ref,
                        r_ref, w_ref, k_ref, v_ref, kk_ref, b_ref, g_ref, bonus_ref,
                        yc_ref, sgv_ref, yd_ref, cnew_ref):
    gw = GROUP_WIDTH
    sp = sp_ref[...]
    r, lw, kmod, v, kk, b, g, bonus = _rwkv_elem(
        pr_ref[...], pk_ref[...], pv_ref[...], pl_ref[...],
        sp[:, 0:gw], sp[:, gw:2 * gw], sp[:, 2 * gw:3 * gw], sp[:, 3 * gw:3 * gw + LANES],
        vec_ref[...], wup_ref, e_ref[...], precise=True)
    for ref, val in ((r_ref, r), (w_ref, jnp.exp(lw)), (k_ref, kmod), (v_ref, v), (kk_ref, kk),
                     (b_ref, b), (g_ref, g), (bonus_ref, bonus)):
        ref[...] = val
    u = jax.nn.gelu(su_ref[...])
    sv = _layer_norm(jax.nn.gelu(sv_ref[...]), sgp_ref[0:1, :], sgp_ref[1:2, :])
    sgv_ref[...] = sv
    yc_ref[...] = (u * (sgp_ref[2:3, :] * sv + sgp_ref[3:4, :])).astype(yc_ref.dtype)
    z = ca_ref[...] * jax.nn.sigmoid(cg_ref[...])
    nprev = CONV_WIDTH - 1
    acc = cw_ref[nprev:nprev + 1, :] * z
    for i in range(nprev):
        acc = acc + cw_ref[i:i + 1, :] * cprev_ref[i]
        if i > 0:
            cnew_ref[i - 1] = cprev_ref[i]
    cnew_ref[nprev - 1] = z
    y = _layer_norm(acc + cvec_ref[0:1, :], cvec_ref[1:2, :], cvec_ref[2:3, :])
    yd_ref[...] = (y * jax.nn.sigmoid(y)).astype(yd_ref.dtype)


def _sample_wkv_kernel(r_ref, w_ref, k_ref, v_ref, kk_ref, b_ref, g_ref, bonus_ref, ln_ref, s_ref,
                       ya_ref, st_ref):
    hd = HEAD_DIM
    r, w, k, v, kk, b, g, bonus = (x[0] for x in (r_ref, w_ref, k_ref, v_ref, kk_ref, b_ref, g_ref, bonus_ref))
    s = s_ref[0]
    eye = (lax.broadcasted_iota(jnp.int32, (hd, hd), 0) == lax.broadcasted_iota(jnp.int32, (hd, hd), 1))
    eye = eye.astype(F32)[None]
    v_col = jnp.sum(eye * v, axis=2, keepdims=True)
    sa = jnp.sum(s * (-kk), axis=2, keepdims=True)
    s_new = s * w + sa * b + v_col * k
    st_ref[0] = s_new
    o_col = jnp.sum(s_new * r, axis=2, keepdims=True)
    o = jnp.sum(o_col * eye, axis=1, keepdims=True)
    mu = jnp.mean(o, axis=-1, keepdims=True)
    d = o - mu
    var = jnp.mean(d * d, axis=-1, keepdims=True)
    on = d * lax.rsqrt(var + GN_EPS) * ln_ref[0] + ln_ref[1]
    ya_ref[0] = (on + bonus) * g


def _sample_mixers(proj, shift_prev, s0, conv_prev_t, vec, wup, e, sgp, cw, cvec):
    nb = proj.shape[0]
    gw, nh, hd = GROUP_WIDTH, N_HEADS, HEAD_DIM
    col = lambda cb: pl.BlockSpec((nb, gw), lambda i: (0, cb))
    full = lambda x: pl.BlockSpec(x.shape, lambda i: (0,) * x.ndim)
    row_f32 = jax.ShapeDtypeStruct((nb, gw), F32)
    outs = pl.pallas_call(
        _sample_elem_kernel,
        grid=(1,),
        in_specs=[col(CB_R), col(CB_RK), col(CB_RV),
                  pl.BlockSpec((nb, LANES), lambda i: (0, CB_L * (gw // LANES))),
                  full(shift_prev), full(vec), full(wup), full(e),
                  col(CB_SU), col(CB_SV), full(sgp), col(CB_CA), col(CB_CG),
                  full(conv_prev_t), full(cw), full(cvec)],
        out_specs=[pl.BlockSpec((nb, gw), lambda i: (0, 0))] * 11 + [full(conv_prev_t)],
        out_shape=[row_f32] * 11 + [jax.ShapeDtypeStruct(conv_prev_t.shape, F32)],
        compiler_params=_cp("arbitrary"),
        name="sample_elem",
    )(proj, proj, proj, proj, shift_prev, vec, wup, e, proj, proj, sgp, proj, proj, conv_prev_t, cw, cvec)
    heads = [x.reshape(nb, nh, 1, hd) for x in outs[:8]]
    yc, sgv, yd, conv_new_t = outs[8:]
    ln = jnp.stack([vec[V_LNG].reshape(nh, 1, hd), vec[V_LNB].reshape(nh, 1, hd)])
    hspec = pl.BlockSpec((1, nh, 1, hd), lambda b: (b, 0, 0, 0))
    sspec = pl.BlockSpec((1, nh, hd, hd), lambda b: (b, 0, 0, 0))
    ya, s_new = pl.pallas_call(
        _sample_wkv_kernel,
        grid=(nb,),
        in_specs=[hspec] * 8 + [pl.BlockSpec(ln.shape, lambda b: (0, 0, 0, 0)), sspec],
        out_specs=[hspec, sspec],
        out_shape=[jax.ShapeDtypeStruct((nb, nh, 1, hd), F32), jax.ShapeDtypeStruct(s0.shape, F32)],
        compiler_params=_cp("arbitrary"),
        name="sample_wkv",
    )(*heads, ln, s0)
    return ya.reshape(nb, gw), s_new, yc, sgv, yd, conv_new_t


PAGES_PER_STEP = 32


def _moba_scores_kernel(pt_ref, q_ref, *refs):
    pages, o_ref = refs[:-1], refs[-1]
    ps = pages[0].shape[4]
    qb = jnp.broadcast_to(q_ref[0] * (HEAD_DIM ** -0.5), pages[0].shape[2:])
    for i, p_ref in enumerate(pages):
        rows = [jnp.sum(p_ref[0, 0, h] * qb[h], axis=0, keepdims=True) for h in range(N_HEADS)]
        o_ref[0, :, i * ps:(i + 1) * ps] = jnp.concatenate(rows, axis=0)


def _moba_select_kernel(s_ref, q_ref, kn_ref, bi_ref, slope_ref, p_ref, p0_ref, *, n_past_blocks):
    s = s_ref[...]
    length = s.shape[1]
    bi = bi_ref[...]
    s0 = jnp.sum(q_ref[...] * (HEAD_DIM ** -0.5) * kn_ref[...], axis=1, keepdims=True)
    gate = sum(_dot(p, bi) for p in _split3(s)) * (1.0 / MOBA_BLOCK)
    lane = lax.broadcasted_iota(jnp.int32, gate.shape, 1)
    gate = jnp.where(lane < n_past_blocks, gate, -jnp.inf)
    sel = jnp.zeros(gate.shape, jnp.bool_)
    for _ in range(min(MOBA_TOPK, n_past_blocks)):
        m = jnp.max(gate, axis=1, keepdims=True)
        idx = jnp.min(jnp.where(gate == m, lane, LANES), axis=1, keepdims=True)
        pick = lane == idx
        sel = sel | pick
        gate = jnp.where(pick, -jnp.inf, gate)
    picked = lax.dot_general(sel.astype(BF16), bi, (((1,), (1,)), ((), ())),
                             preferred_element_type=F32) > 0.5
    pos = lax.broadcasted_iota(jnp.int32, s.shape, 1)
    dist = (length - pos).astype(F32)
    logit = jnp.where(picked, s - slope_ref[:, 0:1] * dist, -jnp.inf)
    m = jnp.maximum(jnp.max(logit, axis=1, keepdims=True), s0)
    p = jnp.exp(logit - m)
    p0 = jnp.exp(s0 - m)
    inv = 1.0 / (jnp.sum(p, axis=1, keepdims=True) + p0)
    p_ref[...] = p * inv
    p0_ref[...] = jnp.broadcast_to(p0 * inv, p0_ref.shape)


def _moba_pv_kernel(pt_ref, p_ref, p0_ref, vn_ref, *refs, n_steps):
    pages, o_ref, acc = refs[:-2], refs[-2], refs[-1]
    j = pl.program_id(1)
    ps = pages[0].shape[4]

    @pl.when(j == 0)
    def _():
        acc[...] = jnp.zeros_like(acc)

    for h in range(N_HEADS):
        tot = acc[h]
        for i, v_ref in enumerate(pages):
            tot = tot + v_ref[0, 0, h] * p_ref[0, h:h + 1, i * ps:(i + 1) * ps]
        acc[h] = tot

    @pl.when(j == n_steps - 1)
    def _():
        for h in range(N_HEADS):
            o_ref[0, h] = (jnp.sum(acc[h], axis=1, keepdims=True)
                           + p0_ref[0, h:h + 1, 0:1] * vn_ref[0, h])


def _moba_sample(proj, cache_k, cache_v, layer, page_table):
    nb, n_pages = page_table.shape
    ps, gw, nh, hd = cache_k.shape[2], GROUP_WIDTH, N_HEADS, HEAD_DIM
    length = n_pages * ps
    assert length % MOBA_BLOCK == 0
    n_past_blocks = length // MOBA_BLOCK
    pp = min(PAGES_PER_STEP, n_pages)
    n_steps = n_pages // pp
    q, kn, vn = (proj[:, cb * gw:(cb + 1) * gw] for cb in (CB_Q, CB_K, CB_V))
    cache_kt = cache_k.transpose(0, 1, 3, 4, 2)
    cache_vt = cache_v.transpose(0, 1, 3, 4, 2)
    col4 = lambda x: x.reshape(nb, nh, hd, 1)
    col_spec = pl.BlockSpec((1, nh, hd, 1), lambda b, j, pt: (b, 0, 0, 0))

    def page_spec(i):
        return pl.BlockSpec((1, 1, nh, hd, ps), lambda b, j, pt: (layer, pt[b, j * pp + i], 0, 0, 0))

    scores = pl.pallas_call(
        _moba_scores_kernel,
        grid_spec=pltpu.PrefetchScalarGridSpec(
            num_scalar_prefetch=1,
            grid=(nb, n_steps),
            in_specs=[col_spec] + [page_spec(i) for i in range(pp)],
            out_specs=pl.BlockSpec((1, nh, pp * ps), lambda b, j, pt: (b, 0, j)),
        ),
        out_shape=jax.ShapeDtypeStruct((nb, nh, length), F32),
        compiler_params=_cp("arbitrary", "arbitrary"),
        name="moba_scores",
    )(page_table, col4(q), *([cache_kt] * pp))

    bi = (jnp.arange(length)[:, None] // MOBA_BLOCK == jnp.arange(LANES)[None, :]).astype(BF16)
    rows = nb * nh
    tr = min(rows, 128)
    slopes = jnp.broadcast_to(
        jnp.tile(jnp.array([_alibi_slope(h) for h in range(nh)], F32), nb)[:, None], (rows, LANES))
    p, p0 = pl.pallas_call(
        functools.partial(_moba_select_kernel, n_past_blocks=n_past_blocks),
        grid=(rows // tr,),
        in_specs=[pl.BlockSpec((tr, length), lambda i: (i, 0)),
                  pl.BlockSpec((tr, hd), lambda i: (i, 0)),
                  pl.BlockSpec((tr, hd), lambda i: (i, 0)),
                  pl.BlockSpec((length, LANES), lambda i: (0, 0)),
                  pl.BlockSpec((tr, LANES), lambda i: (i, 0))],
        out_specs=[pl.BlockSpec((tr, length), lambda i: (i, 0)),
                   pl.BlockSpec((tr, LANES), lambda i: (i, 0))],
        out_shape=[jax.ShapeDtypeStruct((rows, length), F32), jax.ShapeDtypeStruct((rows, LANES), F32)],
        compiler_params=_cp("arbitrary"),
        name="moba_select",
    )(scores.reshape(rows, length), q.reshape(rows, hd), kn.reshape(rows, hd), bi, slopes)

    out = pl.pallas_call(
        functools.partial(_moba_pv_kernel, n_steps=n_steps),
        grid_spec=pltpu.PrefetchScalarGridSpec(
            num_scalar_prefetch=1,
            grid=(nb, n_steps),
            in_specs=[pl.BlockSpec((1, nh, pp * ps), lambda b, j, pt: (b, 0, j)),
                      pl.BlockSpec((1, nh, LANES), lambda b, j, pt: (b, 0, 0)),
                      col_spec]
                     + [page_spec(i) for i in range(pp)],
            out_specs=col_spec,
            scratch_shapes=[pltpu.VMEM((nh, hd, ps), F32)],
        ),
        out_shape=jax.ShapeDtypeStruct((nb, nh, hd, 1), F32),
        compiler_params=_cp("arbitrary", "arbitrary"),
        name="moba_pv",
    )(page_table, p.reshape(nb, nh, length), p0.reshape(nb, nh, LANES), col4(vn), *([cache_vt] * pp))
    return out.reshape(nb, gw)


RWKV_PASSES = (1, 1)


def _shift_layout(shift):
    n = shift.shape[0]
    pad = jnp.zeros((n, LANES - 3 * TM_LORA), F32)
    return jnp.concatenate([shift, pad], axis=1)


def _layer_params(l, w_in, w_out, w_ffn_in, w_ffn_out, tm_mu, tm_w0, tm_w_up, tm_a0, tm_a_up, tm_g_up,
                  tm_k_k, tm_k_a, tm_r_k, tm_ln_g, tm_ln_b, sg_ws, sg_bs, sg_ln_g, sg_ln_b,
                  cv_w, cv_b, cv_ln_g, cv_ln_b):
    gw, lr, hd = GROUP_WIDTH, TM_LORA, HEAD_DIM
    wi = w_in[l]
    w_in_f32 = jnp.concatenate(
        [wi[:, A_PROJ:], wi[:, :A_PROJ], jnp.zeros((wi.shape[0], IN_PAD - IN_WIDTH), F32)], axis=1)
    mu = tm_mu[l]
    mu_l = jnp.concatenate([mu[3 * gw:], jnp.zeros((gw - 3 * lr,), F32)])
    vec = jnp.stack([mu[0:gw], mu[gw:2 * gw], mu[2 * gw:3 * gw], tm_w0[l], tm_a0[l], tm_k_k[l], tm_k_a[l],
                     tm_r_k[l].reshape(gw), tm_ln_g[l], tm_ln_b[l], mu_l]
                    + [jnp.zeros((gw,), F32)] * 5)
    wup = jnp.zeros((3, LANES, gw), F32)
    wup = wup.at[0, 0:lr].set(tm_w_up[l]).at[1, lr:2 * lr].set(tm_a_up[l]).at[2, 2 * lr:3 * lr].set(tm_g_up[l])
    sg_bias = jnp.repeat(sg_bs[l].T, hd, axis=1)
    sg_ln = jnp.stack([sg_ln_g[l], sg_ln_b[l]] + [jnp.zeros((gw,), F32)] * 6)
    sgp = jnp.stack([sg_ln_g[l], sg_ln_b[l], jnp.repeat(sg_ws[l][:, 0, 0], hd), jnp.repeat(sg_bs[l][:, 0], hd)]
                    + [jnp.zeros((gw,), F32)] * 4)
    cw = jnp.concatenate([cv_w[l], jnp.zeros((1, gw), F32)], axis=0)
    cvec = jnp.stack([cv_b[l], cv_ln_g[l], cv_ln_b[l]] + [jnp.zeros((gw,), F32)] * 5)
    return dict(vec=vec, wup=wup, sg_w=sg_ws[l], sg_bias=sg_bias, sg_ln=sg_ln, sgp=sgp, cw=cw, cvec=cvec,
                w_in_f32=w_in_f32[None], dense_f32=(w_out, w_ffn_in, w_ffn_out))


def _prompt_dense_tail(x, ys, mod, p, g_ffn, layer):
    ng, t, d = x.shape
    m = ng * t
    w_out, w_ffn_in, w_ffn_out = p['dense_f32']
    x1 = _mm(ys, w_out, layer, 1024, 512, res=x.reshape(m, d), gate=mod, gate_col=2, rows_per_group=t)
    h2 = _norm_mod(x1.reshape(ng, t, d), g_ffn, mod, 3, 4, 512).reshape(m, d)
    hid = _mm_swiglu(h2, w_ffn_in, layer, 1024, 512)
    x2 = _mm_rows([hid], w_ffn_out[layer].astype(BF16), 512, 512, res=x1, gate=mod, gate_col=5, rows_per_group=t)
    return x2.reshape(ng, t, d)


def _prompt_layer(x, mod, p, g_mix, g_ffn, e, layer):
    nb, t, d = x.shape
    m = nb * t
    gw, nh, hd = GROUP_WIDTH, N_HEADS, HEAD_DIM
    h = _norm_mod(x, g_mix, mod, 0, 1, 512).reshape(m, d)
    proj = _mm([h], p['w_in_f32'], 0, 1024, 512)
    shift0 = jnp.zeros((nb, 1, 3 * gw + LANES), F32)
    s0 = jnp.zeros((nb, nh, hd, hd), F32)
    ya, wkv = _rwkv_prompt(proj, nb, t, shift0, s0, p['vec'], p['wup'], e, RWKV_PASSES)
    yb = _moba_prompt(proj, nb, t)
    yc = _sgu_prompt(proj, m, p['sg_w'], p['sg_bias'], p['sg_ln'])
    yd, hist = _conv_prompt(proj, nb, t, jnp.zeros((nb, CONV_HALO, gw), F32), p['cw'], p['cvec'])
    x2 = _prompt_dense_tail(x, [ya.reshape(m, gw), yb, yc, yd], mod, p, g_ffn, layer)
    proj3 = proj.reshape(nb, t, IN_PAD)
    last = proj3[:, t - 1]
    shift = jnp.concatenate([last[:, CB_R * gw:CB_L * gw], last[:, CB_L * gw:CB_L * gw + 3 * TM_LORA]], axis=1)
    new_k = proj3[:, :, CB_K * gw:(CB_K + 1) * gw].reshape(nb, t, nh, hd)
    new_v = proj3[:, :, CB_V * gw:(CB_V + 1) * gw].reshape(nb, t, nh, hd)
    return x2, (shift, wkv, new_k, new_v, hist[:, CONV_HALO - (CONV_WIDTH - 1):])


def _sample_layer(x, mod, p, g_mix, g_ffn, e, layer, shift_prev, wkv_prev, conv_prev, cache_k, cache_v,
                  page_table):
    nb, t, d = x.shape
    gw, nh, hd = GROUP_WIDTH, N_HEADS, HEAD_DIM
    xg = x.reshape(1, nb, d)
    h = _norm_mod(xg, g_mix, mod, 0, 1, nb, dtype=F32).reshape(nb, d)
    proj = _mm_precise([h], p['w_in_f32'], 0, 512)
    ya, wkv, yc, sgv, yd, conv_new_t = _sample_mixers(
        proj, _shift_layout(shift_prev), wkv_prev, conv_prev.transpose(1, 0, 2),
        p['vec'], p['wup'], e, p['sgp'], p['cw'], p['cvec'])
    yb = _moba_sample(proj, cache_k, cache_v, layer, page_table)
    w_out, w_ffn_in, w_ffn_out = p['dense_f32']
    x1 = _mm_precise([ya, yb, yc, yd], w_out, layer, 512, mode="gated", res=x.reshape(nb, d), gate=mod, gate_col=2)
    h2 = _norm_mod(x1.reshape(1, nb, d), g_ffn, mod, 3, 4, nb, dtype=F32).reshape(nb, d)
    hid = _mm_precise([h2], w_ffn_in, layer, 512, mode="swiglu")
    x2 = _mm_precise([hid], w_ffn_out, layer, 256, mode="gated", res=x1, gate=mod, gate_col=5)
    shift = jnp.concatenate([proj[:, CB_R * gw:CB_L * gw], proj[:, CB_L * gw:CB_L * gw + 3 * TM_LORA]], axis=1)
    new_k = proj[:, CB_K * gw:(CB_K + 1) * gw].reshape(nb, 1, nh, hd)
    new_v = proj[:, CB_V * gw:(CB_V + 1) * gw].reshape(nb, 1, nh, hd)
    return x2.reshape(nb, t, d), (shift, wkv, new_k, new_v, sgv.reshape(nb, 1, gw), conv_new_t.transpose(1, 0, 2))


def kernel(x_prompt, x_sample, c_prompt, c_sample, cache_k_b, cache_v_b, state_a_wkv, state_a_shift, state_d_conv, page_table, w_ada, b_ada, g_norm_mix, w_in, w_out, tm_mu, tm_w0, tm_w_up, tm_a0, tm_a_up, tm_g_up, tm_k_k, tm_k_a, tm_r_k, tm_ln_g, tm_ln_b, sg_ws, sg_bs, sg_ln_g, sg_ln_b, cv_w, cv_b, cv_ln_g, cv_ln_b, g_norm_ffn, w_ffn_in, w_ffn_out, g_norm_final):
    depth = w_in.shape[0]
    bp, ns = x_prompt.shape[0], x_sample.shape[0]
    gw = GROUP_WIDTH
    rows = bp + ns
    rows_pad = -(-rows // 8) * 8
    c_all = jnp.concatenate([c_prompt, c_sample, jnp.zeros((rows_pad - rows, c_prompt.shape[1]), F32)], axis=0)
    mod = _ada(c_all, w_ada, b_ada)
    e = _head_ones()
    xp, xs = x_prompt, x_sample
    st_p, st_s = [], []
    for l in range(depth):
        p = _layer_params(l, w_in, w_out, w_ffn_in, w_ffn_out, tm_mu, tm_w0, tm_w_up, tm_a0, tm_a_up, tm_g_up,
                          tm_k_k, tm_k_a, tm_r_k, tm_ln_g, tm_ln_b, sg_ws, sg_bs, sg_ln_g, sg_ln_b,
                          cv_w, cv_b, cv_ln_g, cv_ln_b)
        mod_p = mod[l, :bp][:, None, :]
        mod_s = mod[l, bp:rows][None]
        xp, sp = _prompt_layer(xp, mod_p, p, g_norm_mix[l], g_norm_ffn[l], e, l)
        xs, ss = _sample_layer(xs, mod_s, p, g_norm_mix[l], g_norm_ffn[l], e, l, state_a_shift[l],
                               state_a_wkv[l], state_d_conv[l], cache_k_b, cache_v_b, page_table)
        st_p.append(sp)
        st_s.append(ss)
    y_prompt = _norm(xp, g_norm_final, 512)
    y_sample = _norm(xs.reshape(1, ns, -1), g_norm_final, ns).reshape(xs.shape)
    stack = lambda sts, i: jnp.stack([s[i] for s in sts])
    return (y_prompt, y_sample, stack(st_p, 2), stack(st_p, 3), stack(st_s, 2), stack(st_s, 3),
            stack(st_p, 1), stack(st_s, 1), stack(st_p, 0), stack(st_s, 0), stack(st_p, 4), stack(st_s, 5),
            stack(st_s, 4))
```

```python
import functools

import jax
import jax.numpy as jnp
from jax import lax
from jax.experimental import pallas as pl
from jax.experimental.pallas import tpu as pltpu

F32, BF16 = jnp.float32, jnp.bfloat16

D_MODEL = 2048
GROUP_WIDTH = 512
N_HEADS = 8
HEAD_DIM = 64
TM_LORA = 32
A_PROJ = 3 * GROUP_WIDTH + 3 * TM_LORA
IN_WIDTH = A_PROJ + 7 * GROUP_WIDTH
IN_PAD = 11 * GROUP_WIDTH
MOBA_BLOCK = 256
MOBA_TOPK = 3
SG_CHUNK = 128
CONV_WIDTH = 31
CONV_HALO = 32
RMS_EPS = 1e-6
LN_EPS = 1e-5
GN_EPS = 64e-5
RWKV_CHUNK = 64
RWKV_CHUNKS_PER_STEP = 4
INV_BLOCK = 16
LANES = 128

CB_Q, CB_K, CB_V, CB_SU, CB_SV, CB_CA, CB_CG, CB_R, CB_RK, CB_RV, CB_L = range(11)
(V_MU_R, V_MU_K, V_MU_V, V_W0, V_A0, V_KK, V_KA, V_RK, V_LNG, V_LNB, V_MU_L) = range(11)

VMEM_LIMIT = 56 * 1024 * 1024


def _cp(*sem):
    return pltpu.CompilerParams(dimension_semantics=sem, vmem_limit_bytes=VMEM_LIMIT)


def _dot(a, b):
    return jnp.dot(a, b, preferred_element_type=F32)


def _split2(x):
    hi = x.astype(BF16)
    return hi, (x - hi.astype(F32)).astype(BF16)


def _split3(x):
    h1 = x.astype(BF16)
    r1 = x - h1.astype(F32)
    h2 = r1.astype(BF16)
    return h1, h2, (r1 - h2.astype(F32)).astype(BF16)


def _dot_exact_rhs(x, e):
    hi, lo = _split2(x)
    return _dot(hi, e) + _dot(lo, e)


def _head_ones():
    r = lax.broadcasted_iota(jnp.int32, (GROUP_WIDTH, GROUP_WIDTH), 0) // HEAD_DIM
    c = lax.broadcasted_iota(jnp.int32, (GROUP_WIDTH, GROUP_WIDTH), 1) // HEAD_DIM
    return (r == c).astype(BF16)


def _layer_norm(x, g, b):
    mu = jnp.mean(x, axis=-1, keepdims=True)
    d = x - mu
    var = jnp.mean(d * d, axis=-1, keepdims=True)
    return d * lax.rsqrt(var + LN_EPS) * g + b


def _softplus(x):
    return jnp.maximum(x, 0.0) + jnp.log1p(jnp.exp(-jnp.abs(x)))


def _dot3(x, w):
    rows = x.shape[0]
    x_hi = x.astype(BF16).astype(F32)
    xs = jnp.concatenate([x_hi, x - x_hi], axis=0).astype(BF16)
    wh, wl = _split2(w)
    r = _dot(xs, wh)
    return r[:rows] + r[rows:] + _dot(x_hi.astype(BF16), wl)


def _ada_kernel(c_ref, w_ref, b_ref, o_ref):
    c = c_ref[...]
    o_ref[0] = _dot3(c * jax.nn.sigmoid(c), w_ref[0]) + b_ref[0]


def _ada(c_all, w_ada, b_ada):
    depth, d, n = w_ada.shape
    r = c_all.shape[0]
    tn = 512
    return pl.pallas_call(
        _ada_kernel,
        grid=(depth, n // tn),
        in_specs=[pl.BlockSpec((r, d), lambda l, j: (0, 0)),
                  pl.BlockSpec((1, d, tn), lambda l, j: (l, 0, j)),
                  pl.BlockSpec((1, 1, tn), lambda l, j: (l, 0, j))],
        out_specs=pl.BlockSpec((1, r, tn), lambda l, j: (l, 0, j)),
        out_shape=jax.ShapeDtypeStruct((depth, r, n), F32),
        compiler_params=_cp("arbitrary", "arbitrary"),
        name="ada",
    )(c_all, w_ada, b_ada.reshape(depth, 1, n))


def _norm_mod_kernel(x_ref, g_ref, sc_ref, sh_ref, o_ref):
    x = x_ref[0]
    h = x * lax.rsqrt(jnp.mean(x * x, axis=-1, keepdims=True) + RMS_EPS) * g_ref[...]
    o_ref[0] = (h * (1.0 + sc_ref[0]) + sh_ref[0]).astype(o_ref.dtype)


def _norm_mod(x, g, mod, sh_col, sc_col, tt, dtype=BF16):
    ng, t, d = x.shape
    r = mod.shape[1]
    return pl.pallas_call(
        _norm_mod_kernel,
        grid=(ng, t // tt),
        in_specs=[pl.BlockSpec((1, tt, d), lambda b, i: (b, i, 0)),
                  pl.BlockSpec((1, d), lambda b, i: (0, 0)),
                  pl.BlockSpec((1, r, d), lambda b, i: (b, 0, sc_col)),
                  pl.BlockSpec((1, r, d), lambda b, i: (b, 0, sh_col))],
        out_specs=pl.BlockSpec((1, tt, d), lambda b, i: (b, i, 0)),
        out_shape=jax.ShapeDtypeStruct((ng, t, d), dtype),
        compiler_params=_cp("arbitrary", "arbitrary"),
        name="norm_mod",
    )(x, g.reshape(1, d), mod, mod)


def _norm_kernel(x_ref, g_ref, o_ref):
    x = x_ref[0]
    o_ref[0] = x * lax.rsqrt(jnp.mean(x * x, axis=-1, keepdims=True) + RMS_EPS) * g_ref[...]


def _norm(x, g, tt):
    ng, t, d = x.shape
    return pl.pallas_call(
        _norm_kernel,
        grid=(ng, t // tt),
        in_specs=[pl.BlockSpec((1, tt, d), lambda b, i: (b, i, 0)),
                  pl.BlockSpec((1, d), lambda b, i: (0, 0))],
        out_specs=pl.BlockSpec((1, tt, d), lambda b, i: (b, i, 0)),
        out_shape=jax.ShapeDtypeStruct((ng, t, d), F32),
        compiler_params=_cp("arbitrary", "arbitrary"),
        name="norm_final",
    )(x, g.reshape(1, d))


def _mm_kernel(*refs, n_x, gated):
    xs, w_ref = refs[:n_x], refs[n_x]
    o_ref, wb = refs[-2], refs[-1]

    @pl.when(pl.program_id(1) == 0)
    def _():
        wb[...] = w_ref[0].astype(BF16)

    acc = None
    k0 = 0
    for x_ref in xs:
        k = x_ref.shape[1]
        part = _dot(x_ref[...], wb[k0:k0 + k, :])
        acc = part if acc is None else acc + part
        k0 += k
    if gated:
        res_ref, gt_ref = refs[n_x + 1], refs[n_x + 2]
        o_ref[...] = res_ref[...] + gt_ref[0] * acc
    else:
        o_ref[...] = acc


def _mm(xs, w, layer, tm, tn, res=None, gate=None, gate_col=0, rows_per_group=None):
    m = xs[0].shape[0]
    kdim, n = w.shape[1], w.shape[2]
    gated = res is not None
    in_specs = [pl.BlockSpec((tm, x.shape[1]), lambda j, i: (i, 0)) for x in xs]
    in_specs.append(pl.BlockSpec((1, kdim, tn), lambda j, i: (layer, 0, j)))
    args = list(xs) + [w]
    if gated:
        r = gate.shape[1]
        tiles_per_group = rows_per_group // tm
        nj = n // tn
        in_specs.append(pl.BlockSpec((tm, tn), lambda j, i: (i, j)))
        in_specs.append(pl.BlockSpec((1, r, tn), lambda j, i: (i // tiles_per_group, 0, gate_col * nj + j)))
        args += [res, gate]
    return pl.pallas_call(
        functools.partial(_mm_kernel, n_x=len(xs), gated=gated),
        grid=(n // tn, m // tm),
        in_specs=in_specs,
        out_specs=pl.BlockSpec((tm, tn), lambda j, i: (i, j)),
        out_shape=jax.ShapeDtypeStruct((m, n), F32),
        scratch_shapes=[pltpu.VMEM((kdim, tn), BF16)],
        compiler_params=_cp("arbitrary", "arbitrary"),
        name="mm_gated" if gated else "mm",
    )(*args)


def _mm_rows_kernel(*refs, n_x, gated):
    xs, w_ref = refs[:n_x], refs[n_x]
    o_ref = refs[-1]
    acc = None
    k0 = 0
    for x_ref in xs:
        k = x_ref.shape[1]
        part = _dot(x_ref[...], w_ref[k0:k0 + k, :])
        acc = part if acc is None else acc + part
        k0 += k
    if gated:
        res_ref, gt_ref = refs[n_x + 1], refs[n_x + 2]
        o_ref[...] = res_ref[...] + gt_ref[0] * acc
    else:
        o_ref[...] = acc


def _mm_rows(xs, w, tm, tn, res=None, gate=None, gate_col=0, rows_per_group=None):
    m = xs[0].shape[0]
    n = w.shape[1]
    gated = res is not None
    in_specs = [pl.BlockSpec((tm, x.shape[1]), lambda i, j: (i, 0)) for x in xs]
    in_specs.append(pl.BlockSpec((w.shape[0], tn), lambda i, j: (0, j)))
    args = list(xs) + [w]
    if gated:
        r = gate.shape[1]
        tiles_per_group = rows_per_group // tm
        nj = n // tn
        in_specs.append(pl.BlockSpec((tm, tn), lambda i, j: (i, j)))
        in_specs.append(pl.BlockSpec((1, r, tn), lambda i, j: (i // tiles_per_group, 0, gate_col * nj + j)))
        args += [res, gate]
    return pl.pallas_call(
        functools.partial(_mm_rows_kernel, n_x=len(xs), gated=gated),
        grid=(m // tm, n // tn),
        in_specs=in_specs,
        out_specs=pl.BlockSpec((tm, tn), lambda i, j: (i, j)),
        out_shape=jax.ShapeDtypeStruct((m, n), F32),
        compiler_params=_cp("arbitrary", "arbitrary"),
        name="mm_rows_gated" if gated else "mm_rows",
    )(*args)


def _mix_out_kernel(*refs, n_x):
    xs, (w_ref, res_ref, gt_ref, g_ref, sc_ref, sh_ref, x1_ref, h_ref) = refs[:n_x], refs[n_x:]
    acc = None
    k0 = 0
    for x_ref in xs:
        k = x_ref.shape[1]
        part = _dot(x_ref[...], w_ref[k0:k0 + k, :])
        acc = part if acc is None else acc + part
        k0 += k
    x1 = res_ref[...] + gt_ref[0] * acc
    x1_ref[...] = x1
    h = x1 * lax.rsqrt(jnp.mean(x1 * x1, axis=-1, keepdims=True) + RMS_EPS) * g_ref[...]
    h_ref[...] = (h * (1.0 + sc_ref[0]) + sh_ref[0]).astype(h_ref.dtype)


def _mix_out(xs, w, res, mod, g, tm, rows_per_group):
    m, d = res.shape
    tiles_per_group = rows_per_group // tm
    row = lambda i: (i, 0)
    modspec = lambda col: pl.BlockSpec((1, 1, d), lambda i: (i // tiles_per_group, 0, col))
    return pl.pallas_call(
        functools.partial(_mix_out_kernel, n_x=len(xs)),
        grid=(m // tm,),
        in_specs=[pl.BlockSpec((tm, x.shape[1]), row) for x in xs]
                 + [pl.BlockSpec(w.shape, lambda i: (0, 0)), pl.BlockSpec((tm, d), row), modspec(2),
                    pl.BlockSpec((1, d), lambda i: (0, 0)), modspec(4), modspec(3)],
        out_specs=[pl.BlockSpec((tm, d), row), pl.BlockSpec((tm, d), row)],
        out_shape=[jax.ShapeDtypeStruct((m, d), F32), jax.ShapeDtypeStruct((m, d), BF16)],
        compiler_params=_cp("arbitrary"),
        name="mix_out",
    )(*xs, w, res, mod, g.reshape(1, d), mod, mod)


def _mm_precise_kernel(*refs, n_x, mode):
    xs = refs[:n_x]
    o_ref = refs[-1]
    if mode == "swiglu":
        x = xs[0][...]
        g = _dot3(x, refs[n_x][0])
        o_ref[...] = g * jax.nn.sigmoid(g) * _dot3(x, refs[n_x + 1][0])
        return
    w_ref = refs[n_x]
    acc = None
    k0 = 0
    for x_ref in xs:
        k = x_ref.shape[1]
        part = _dot3(x_ref[...], w_ref[0, k0:k0 + k, :])
        acc = part if acc is None else acc + part
        k0 += k
    if mode == "gated":
        res_ref, gt_ref = refs[n_x + 1], refs[n_x + 2]
        o_ref[...] = res_ref[...] + gt_ref[0] * acc
    else:
        o_ref[...] = acc


def _mm_precise(xs, w, layer, tn, mode="plain", res=None, gate=None, gate_col=0):
    m = xs[0].shape[0]
    kdim = w.shape[1]
    n = w.shape[2] // 2 if mode == "swiglu" else w.shape[2]
    nj = n // tn
    in_specs = [pl.BlockSpec((m, x.shape[1]), lambda j: (0, 0)) for x in xs]
    in_specs.append(pl.BlockSpec((1, kdim, tn), lambda j: (layer, 0, j)))
    args = list(xs) + [w]
    if mode == "swiglu":
        in_specs.append(pl.BlockSpec((1, kdim, tn), lambda j: (layer, 0, nj + j)))
        args.append(w)
    if mode == "gated":
        in_specs.append(pl.BlockSpec((m, tn), lambda j: (0, j)))
        in_specs.append(pl.BlockSpec((1, m, tn), lambda j: (0, 0, gate_col * nj + j)))
        args += [res, gate]
    return pl.pallas_call(
        functools.partial(_mm_precise_kernel, n_x=len(xs), mode=mode),
        grid=(nj,),
        in_specs=in_specs,
        out_specs=pl.BlockSpec((m, tn), lambda j: (0, j)),
        out_shape=jax.ShapeDtypeStruct((m, n), F32),
        compiler_params=_cp("arbitrary"),
        name="mm_precise_" + mode,
    )(*args)


def _swiglu_kernel(x_ref, wg_ref, wu_ref, o_ref, wgb, wub):
    @pl.when(pl.program_id(1) == 0)
    def _():
        wgb[...] = wg_ref[0].astype(BF16)
        wub[...] = wu_ref[0].astype(BF16)

    x = x_ref[...]
    g = _dot(x, wgb[...])
    u = _dot(x, wub[...])
    o_ref[...] = (g * jax.nn.sigmoid(g) * u).astype(o_ref.dtype)


def _mm_swiglu(x, w, layer, tm, tn):
    m, k = x.shape
    nh = w.shape[2] // 2
    nj = nh // tn
    return pl.pallas_call(
        _swiglu_kernel,
        grid=(nj, m // tm),
        in_specs=[pl.BlockSpec((tm, k), lambda j, i: (i, 0)),
                  pl.BlockSpec((1, k, tn), lambda j, i: (layer, 0, j)),
                  pl.BlockSpec((1, k, tn), lambda j, i: (layer, 0, nj + j))],
        out_specs=pl.BlockSpec((tm, tn), lambda j, i: (i, j)),
        out_shape=jax.ShapeDtypeStruct((m, nh), BF16),
        scratch_shapes=[pltpu.VMEM((k, tn), BF16), pltpu.VMEM((k, tn), BF16)],
        compiler_params=_cp("arbitrary", "arbitrary"),
        name="mm_swiglu",
    )(x, w, w)


def _rwkv_elem(pr, pk, pv, plo, qr, qk, qv, qlo, vec, wup_ref, e, precise=False):
    row = lambda i: vec[i:i + 1, :]
    xr = pr + (qr - pr) * row(V_MU_R)
    xk = pk + (qk - pk) * row(V_MU_K)
    xv = pv + (qv - pv) * row(V_MU_V)
    xl = plo + (qlo - plo) * vec[V_MU_L:V_MU_L + 1, 0:LANES]
    up = _dot3 if precise else (lambda x, w: _dot(x.astype(BF16), w.astype(BF16)))
    lw = up(jnp.tanh(xl), wup_ref[0])
    la = up(xl, wup_ref[1])
    g = up(jax.nn.sigmoid(xl), wup_ref[2])
    w_log = -_softplus(-(row(V_W0) + lw)) - 0.5
    log_decay = -jnp.exp(w_log)
    a = jax.nn.sigmoid(row(V_A0) + la)
    kk = xk * row(V_KK)
    kk = kk / jnp.maximum(jnp.sqrt(_dot_exact_rhs(kk * kk, e)), 1e-12)
    kmod = xk * (1.0 + (a - 1.0) * row(V_KA))
    bonus = _dot_exact_rhs(xr * kmod * row(V_RK), e) * xv
    return xr, log_decay, kmod, xv, kk, kk * a, g, bonus


def _rwkv_prep(pr_ref, pk_ref, pv_ref, pl_ref, sp_ref, vec_ref, wup_ref, e_ref, head_refs, carry):
    rt_ref, at_ref, bt_ref, kt_ref, v_ref, bh_ref, kh_ref, gt_ref = head_refs
    ti = pl.program_id(1)

    @pl.when(ti == 0)
    def _():
        carry[...] = sp_ref[0]

    tt = pr_ref.shape[0]
    first = lax.broadcasted_iota(jnp.int32, (tt, 1), 0) == 0
    gw = GROUP_WIDTH
    cur = (pr_ref[...], pk_ref[...], pv_ref[...], pl_ref[...])
    offs = (0, gw, 2 * gw, 3 * gw)
    prev = []
    for p, o in zip(cur, offs):
        w = p.shape[1]
        prev.append(jnp.where(first, carry[:, o:o + w], pltpu.roll(p, 1, axis=0)))
        carry[:, o:o + w] = p[tt - 1:tt, :]
    vec = vec_ref[...]
    e = e_ref[...]
    r, lw, kmod, v, kk, b, g, bonus = _rwkv_elem(*cur, *prev, vec, wup_ref, e)

    ri = lax.broadcasted_iota(jnp.int32, (tt, tt), 0)
    ci = lax.broadcasted_iota(jnp.int32, (tt, tt), 1)
    same = (ri // RWKV_CHUNK) == (ci // RWKV_CHUNK)
    tri = (same & (ci <= ri)).astype(BF16)
    blk = same.astype(BF16)
    parts = _split3(lw)
    cum = sum(_dot(tri, p) for p in parts)
    tot = sum(_dot(blk, p) for p in parts)
    e_out = jnp.exp(-cum)
    to_end = jnp.exp(tot - cum)
    outs = ((rt_ref, r * jnp.exp(cum)), (at_ref, -kk * jnp.exp(cum - lw)), (bt_ref, b * e_out),
            (kt_ref, kmod * e_out), (v_ref, v), (bh_ref, b * to_end), (kh_ref, kmod * to_end),
            (gt_ref, jnp.exp(tot)))
    for ref, val in outs:
        for h in range(N_HEADS):
            ref[h] = val[:, h * HEAD_DIM:(h + 1) * HEAD_DIM]
    return g, bonus


def _bmm(a, b, ca, cb, passes):
    dn = (((ca,), (cb,)), ((0,), (0,)))
    d = lambda x, y: lax.dot_general(x, y, dn, preferred_element_type=F32)
    if passes == 1:
        return d(a.astype(BF16), b.astype(BF16))
    ah, al = _split2(a)
    bh, bl = _split2(b)
    return d(ah, bh) + d(ah, bl) + d(al, bh)


def _rwkv_chunks(head_refs, s0_ref, o_ref, st_ref, s_scr, passes):
    rt_ref, at_ref, bt_ref, kt_ref, v_ref, bh_ref, kh_ref, gt_ref = head_refs
    c = pl.program_id(1)

    @pl.when(c == 0)
    def _():
        s_scr[...] = s0_ref[0]

    n = RWKV_CHUNK
    nc = rt_ref.shape[1] // n
    shp = (1, n, n)
    ti = lax.broadcasted_iota(jnp.int32, shp, 1)
    tj = lax.broadcasted_iota(jnp.int32, shp, 2)
    strict, incl = tj < ti, tj <= ti
    same = (ti // INV_BLOCK) == (tj // INV_BLOCK)
    nn = lambda a, b, p: _bmm(a, b, 2, 1, p)
    nt = lambda a, b, p: _bmm(a, b, 2, 2, p)
    tn = lambda a, b, p: _bmm(a, b, 1, 1, p)
    p_gen, p_chain = passes

    per_chunk = lambda ref: ref[...].reshape(N_HEADS * nc, n, HEAD_DIM)
    rt, at, bt, kt, v, bh, kh = (per_chunk(r) for r in (rt_ref, at_ref, bt_ref, kt_ref, v_ref, bh_ref, kh_ref))
    g_end = per_chunk(gt_ref)[:, 0:1, :]
    big = nt(jnp.concatenate([at, rt], axis=1), jnp.concatenate([bt, kt], axis=1), p_gen)
    a_ab = jnp.where(strict, big[:, :n, :n], 0.0)
    a_ak = jnp.where(strict, big[:, :n, n:], 0.0)
    a_rb = jnp.where(incl, big[:, n:, :n], 0.0)
    a_rk = jnp.where(incl, big[:, n:, n:], 0.0)
    av = nn(jnp.concatenate([a_ak, a_rk], axis=1), v, p_gen)
    nd = jnp.where(same, a_ab, 0.0)
    no = jnp.where(same, 0.0, a_ab)
    nd2 = nn(nd, nd, p_gen)
    nd4 = nn(nd2, nd2, p_gen)
    nd8 = nn(nd4, nd4, p_gen)
    y = jnp.concatenate([no, at, av[:, :n]], axis=2)
    for pw in (nd8, nd4, nd2, nd):
        y = y + nn(pw, y, p_gen)
    m = y[:, :, :n]
    z = y[:, :, n:]
    m2 = nn(m, m, p_gen)
    z = z + nn(m2, z, p_gen)
    z = z + nn(m, z, p_gen)
    qo = jnp.concatenate([rt, av[:, n:]], axis=2) + nn(a_rb, z, p_gen)
    pg = tn(z, bh, p_gen)
    gm = pg[:, n:] + tn(v, kh, p_gen)

    by_chunk = lambda x: x.reshape((N_HEADS, nc) + x.shape[1:])
    qo, pmat, gm, g_end = by_chunk(qo), by_chunk(pg[:, :n]), by_chunk(gm), by_chunk(g_end)
    s = s_scr[...]
    for i in range(nc):
        o_ref[:, i * n:(i + 1) * n, :] = nt(qo[:, i, :, :HEAD_DIM], s, p_chain) + qo[:, i, :, HEAD_DIM:]
        s = s * g_end[:, i] + nn(s, pmat[:, i], p_chain) + gm[:, i]
    s_scr[...] = s

    @pl.when(c == pl.num_programs(1) - 1)
    def _():
        st_ref[0] = s


def _rwkv_prompt_kernel(pr_ref, pk_ref, pv_ref, pl_ref, sp_ref, vec_ref, wup_ref, e_ref, s0_ref,
                        ya_ref, st_ref, carry, s_scr, o_scr, *head_scr, passes):
    g, bonus = _rwkv_prep(pr_ref, pk_ref, pv_ref, pl_ref, sp_ref, vec_ref, wup_ref, e_ref, head_scr, carry)
    _rwkv_chunks(head_scr, s0_ref, o_scr, st_ref, s_scr, passes)
    o = jnp.concatenate([o_scr[h] for h in range(N_HEADS)], axis=-1)
    e = e_ref[...]
    vec = vec_ref[...]
    mu = _dot_exact_rhs(o, e) * (1.0 / HEAD_DIM)
    d = o - mu
    var = _dot_exact_rhs(d * d, e) * (1.0 / HEAD_DIM)
    on = d * lax.rsqrt(var + GN_EPS) * vec[V_LNG:V_LNG + 1, :] + vec[V_LNB:V_LNB + 1, :]
    ya_ref[0] = ((on + bonus) * g).astype(ya_ref.dtype)


def _rwkv_prompt(proj, nb, t, shift_prev, s0, vec, wup, e, passes):
    tt = RWKV_CHUNKS_PER_STEP * RWKV_CHUNK
    nt_ = t // tt
    gw, nh, hd = GROUP_WIDTH, N_HEADS, HEAD_DIM
    pspec = lambda cb: pl.BlockSpec((tt, gw), lambda b, i: (b * nt_ + i, cb))
    const = lambda shape: pl.BlockSpec(shape, lambda b, i: (0,) * len(shape))
    sspec = pl.BlockSpec((1, nh, hd, hd), lambda b, i: (b, 0, 0, 0))
    head_major = pltpu.VMEM((nh, tt, hd), F32)
    return pl.pallas_call(
        functools.partial(_rwkv_prompt_kernel, passes=passes),
        grid=(nb, nt_),
        in_specs=[pspec(CB_R), pspec(CB_RK), pspec(CB_RV),
                  pl.BlockSpec((tt, LANES), lambda b, i: (b * nt_ + i, CB_L * (gw // LANES))),
                  pl.BlockSpec((1, 1, 3 * gw + LANES), lambda b, i: (b, 0, 0)),
                  const(vec.shape), const(wup.shape), const(e.shape), sspec],
        out_specs=[pl.BlockSpec((1, tt, gw), lambda b, i: (b, i, 0)), sspec],
        out_shape=[jax.ShapeDtypeStruct((nb, t, gw), BF16), jax.ShapeDtypeStruct((nb, nh, hd, hd), F32)],
        scratch_shapes=[pltpu.VMEM((1, 3 * gw + LANES), F32), pltpu.VMEM((nh, hd, hd), F32), head_major]
                       + [head_major] * 8,
        compiler_params=_cp("arbitrary", "arbitrary"),
        name="rwkv_prompt",
    )(proj, proj, proj, proj, shift_prev, vec, wup, e, s0)


def _alibi_slope(h):
    return 2.0 ** (-8.0 * (h + 1) / N_HEADS)


LOG2E = 1.4426950408889634
MASKED = -1e30


def _eye(n):
    return (lax.broadcasted_iota(jnp.int32, (n, n), 0) == lax.broadcasted_iota(jnp.int32, (n, n), 1)).astype(BF16)


def _moba_prompt_kernel(q_ref, k_ref, v_ref, selm_ref, slope_ref, o_ref, kh_scr, vt_scr, km_scr, selt_scr, m_scr,
                        acc_scr, *, n_blocks):
    qi = pl.program_id(1)
    blk, nh, hd = MOBA_BLOCK, N_HEADS, HEAD_DIM
    nt_dims = (((1,), (1,)), ((), ()))

    @pl.when(qi == 0)
    def _():
        k = k_ref[...]
        v = v_ref[...]
        one_row = (lax.broadcasted_iota(jnp.int32, (hd, blk), 0) == 0).astype(BF16)
        eye_hd = _eye(hd)
        for h in range(nh):
            kh_scr[h] = k[:, h * hd:(h + 1) * hd].astype(BF16)
            vt = lax.dot_general(eye_hd, v[:, h * hd:(h + 1) * hd].astype(BF16), nt_dims,
                                 preferred_element_type=F32).astype(BF16)
            for n in range(n_blocks):
                vt_scr[n, h] = jnp.concatenate([vt[:, n * blk:(n + 1) * blk], one_row], axis=0)
        rows = [jnp.mean(k[n * blk:(n + 1) * blk, :], axis=0, keepdims=True) for n in range(n_blocks)]
        rows += [jnp.zeros((1, k.shape[1]), F32)] * (selm_ref.shape[0] - n_blocks)
        kmean = jnp.concatenate(rows, axis=0)
        tn_dims = (((0,), (0,)), ((), ()))
        spread = sum(lax.dot_general(p, selm_ref[...], tn_dims, preferred_element_type=F32)
                     for p in _split3(kmean))
        rh = lax.broadcasted_iota(jnp.int32, spread.shape, 0) // hd
        ch = lax.broadcasted_iota(jnp.int32, spread.shape, 1) % nh
        for i, p in enumerate(_split3(jnp.where(rh == ch, spread, 0.0))):
            km_scr[i] = p

    own = qi
    q = q_ref[...]
    q1, q2, q3 = _split3(q * (hd ** -0.5))
    k1, k2, k3 = km_scr[0], km_scr[1], km_scr[2]
    gate = (_dot(q1, k1) + _dot(q1, k2) + _dot(q2, k1)) + (_dot(q1, k3) + _dot(q2, k2) + _dot(q3, k1))
    lane_blk = lax.broadcasted_iota(jnp.int32, gate.shape, 1) // nh
    valid = lane_blk < own
    gate = jnp.where(valid, gate, -jnp.inf)
    rank = jnp.zeros_like(gate)
    for d in range(1, n_blocks):
        earlier = pltpu.roll(gate, nh * d, axis=1)
        later = pltpu.roll(gate, LANES - nh * d, axis=1)
        rank = rank + jnp.where(earlier >= gate, 1.0, 0.0) + jnp.where(later > gate, 1.0, 0.0)
    sel = jnp.where(valid & (rank < min(MOBA_TOPK, n_blocks - 1)), 1.0, 0.0)
    selt_scr[...] = lax.dot_general(_eye(LANES), sel.astype(BF16), nt_dims, preferred_element_type=F32)

    qa = q * ((hd ** -0.5) * LOG2E)
    qh = jnp.stack([qa[:, h * hd:(h + 1) * hd].astype(BF16) for h in range(nh)])
    ahead_i = (lax.broadcasted_iota(jnp.int32, (blk, blk), 1) - lax.broadcasted_iota(jnp.int32, (blk, blk), 0))
    causal = ahead_i >= 0
    ahead = ahead_i.astype(F32)
    slope2 = slope_ref[...] * LOG2E
    bnt = (((2,), (2,)), ((0,), (0,)))
    bnn = (((2,), (1,)), ((0,), (0,)))

    def scores(first_key):
        k_blk = kh_scr[:, pl.ds(first_key, blk), :]
        s = lax.dot_general(k_blk, qh, bnt, preferred_element_type=F32)
        return [s[h] - (_alibi_slope(h) * LOG2E) * ahead for h in range(nh)]

    tm = jnp.stack([jnp.where(causal, sh, MASKED) for sh in scores(pl.multiple_of(own * blk, blk))])
    m = jnp.max(tm, axis=1, keepdims=True)
    p = jnp.exp2(tm - m)
    m_scr[...] = m
    acc_scr[...] = lax.dot_general(vt_scr[own], p.astype(BF16), bnn, preferred_element_type=F32)

    def past_block(n, carry):
        shift = slope2 * ((own - n) * blk).astype(F32)
        sh = scores(pl.multiple_of(n * blk, blk))
        tm = jnp.stack([jnp.where(selt_scr[pl.ds(n * nh + h, 1), :] > 0.5, sh[h], MASKED) for h in range(nh)])
        m_old = m_scr[...]
        m_new = jnp.maximum(m_old, jnp.max(tm, axis=1, keepdims=True) - shift)
        p = jnp.exp2(tm - (m_new + shift))
        m_scr[...] = m_new
        acc_scr[...] = (jnp.exp2(m_old - m_new) * acc_scr[...]
                        + lax.dot_general(vt_scr[n], p.astype(BF16), bnn, preferred_element_type=F32))
        return carry

    lax.fori_loop(0, own, past_block, 0)
    outs = []
    eye_hd = _eye(hd)
    for h in range(nh):
        acc = acc_scr[h]
        out_t = (acc[0:hd, :] / acc[hd:hd + 1, :]).astype(BF16)
        outs.append(lax.dot_general(out_t, eye_hd, (((0,), (0,)), ((), ())), preferred_element_type=F32))
    o_ref[...] = jnp.concatenate(outs, axis=-1).astype(o_ref.dtype)


def _moba_prompt(proj, nb, t):
    blk, gw, nh, hd = MOBA_BLOCK, GROUP_WIDTH, N_HEADS, HEAD_DIM
    n_blocks = t // blk
    assert n_blocks * nh <= LANES // 2
    selm = (jnp.arange(8)[:, None] == jnp.arange(LANES)[None, :] // nh).astype(BF16)
    slopes = jnp.broadcast_to(jnp.array([_alibi_slope(h) for h in range(nh)], F32)[:, None, None], (nh, 1, blk))
    return pl.pallas_call(
        functools.partial(_moba_prompt_kernel, n_blocks=n_blocks),
        grid=(nb, n_blocks),
        in_specs=[pl.BlockSpec((blk, gw), lambda b, i: (b * n_blocks + i, CB_Q)),
                  pl.BlockSpec((t, gw), lambda b, i: (b, CB_K)),
                  pl.BlockSpec((t, gw), lambda b, i: (b, CB_V)),
                  pl.BlockSpec((8, LANES), lambda b, i: (0, 0)),
                  pl.BlockSpec((nh, 1, blk), lambda b, i: (0, 0, 0))],
        out_specs=pl.BlockSpec((blk, gw), lambda b, i: (b * n_blocks + i, 0)),
        out_shape=jax.ShapeDtypeStruct((nb * t, gw), BF16),
        scratch_shapes=[pltpu.VMEM((nh, t, hd), BF16), pltpu.VMEM((n_blocks, nh, 2 * hd, blk), BF16),
                        pltpu.VMEM((3, gw, LANES), BF16), pltpu.VMEM((LANES, blk), F32),
                        pltpu.VMEM((nh, 1, blk), F32), pltpu.VMEM((nh, 2 * hd, blk), F32)],
        compiler_params=_cp("arbitrary", "arbitrary"),
        name="moba_prompt",
    )(proj, proj, proj, selm, slopes)


def _sgu_prompt_kernel(pu_ref, pv_ref, w_ref, bias_ref, ln_ref, o_ref):
    u = jax.nn.gelu(pu_ref[...])
    v = _layer_norm(jax.nn.gelu(pv_ref[...]), ln_ref[0:1, :], ln_ref[1:2, :]).astype(BF16)
    cl, hd = SG_CHUNK, HEAD_DIM
    ri = lax.broadcasted_iota(jnp.int32, (cl, cl), 0)
    ci = lax.broadcasted_iota(jnp.int32, (cl, cl), 1)
    ws = [jnp.where(ci <= ri, w_ref[h], 0.0).astype(BF16) for h in range(N_HEADS)]
    bias = bias_ref[...]
    for c in range(u.shape[0] // cl):
        vc = v[c * cl:(c + 1) * cl, :]
        mixed = jnp.concatenate([_dot(ws[h], vc[:, h * hd:(h + 1) * hd]) for h in range(N_HEADS)], axis=-1)
        o_ref[c * cl:(c + 1) * cl, :] = (u[c * cl:(c + 1) * cl, :] * (mixed + bias)).astype(o_ref.dtype)


def _sgu_prompt(proj, nrows, w, bias_exp, ln):
    tt, gw = 256, GROUP_WIDTH
    return pl.pallas_call(
        _sgu_prompt_kernel,
        grid=(nrows // tt,),
        in_specs=[pl.BlockSpec((tt, gw), lambda i: (i, CB_SU)),
                  pl.BlockSpec((tt, gw), lambda i: (i, CB_SV)),
                  pl.BlockSpec(w.shape, lambda i: (0, 0, 0)),
                  pl.BlockSpec(bias_exp.shape, lambda i: (0, 0)),
                  pl.BlockSpec(ln.shape, lambda i: (0, 0))],
        out_specs=pl.BlockSpec((tt, gw), lambda i: (i, 0)),
        out_shape=jax.ShapeDtypeStruct((nrows, gw), BF16),
        compiler_params=_cp("arbitrary"),
        name="sgu_prompt",
    )(proj, proj, w, bias_exp, ln)


def _conv_prompt_kernel(pa_ref, pg_ref, prev_ref, w_ref, vec_ref, o_ref, hist_ref, zf):
    ti = pl.program_id(1)
    tt = pa_ref.shape[0]
    halo = CONV_HALO

    @pl.when(ti == 0)
    def _():
        zf[0:halo, :] = prev_ref[0]

    z = pa_ref[...] * jax.nn.sigmoid(pg_ref[...])
    zf[halo:halo + tt, :] = z
    lead = halo - (CONV_WIDTH - 1)
    acc = jnp.zeros_like(z)
    for i in range(CONV_WIDTH):
        acc = acc + w_ref[i:i + 1, :] * zf[pl.ds(lead + i, tt), :]
    y = _layer_norm(acc + vec_ref[0:1, :], vec_ref[1:2, :], vec_ref[2:3, :])
    o_ref[...] = (y * jax.nn.sigmoid(y)).astype(o_ref.dtype)
    tail = zf[pl.ds(tt, halo), :]
    hist_ref[0] = tail
    zf[0:halo, :] = tail


def _conv_prompt(proj, nb, t, prev, w, vec):
    tt, gw = 256, GROUP_WIDTH
    nt_ = t // tt
    return pl.pallas_call(
        _conv_prompt_kernel,
        grid=(nb, nt_),
        in_specs=[pl.BlockSpec((tt, gw), lambda b, i: (b * nt_ + i, CB_CA)),
                  pl.BlockSpec((tt, gw), lambda b, i: (b * nt_ + i, CB_CG)),
                  pl.BlockSpec((1, CONV_HALO, gw), lambda b, i: (b, 0, 0)),
                  pl.BlockSpec(w.shape, lambda b, i: (0, 0)),
                  pl.BlockSpec(vec.shape, lambda b, i: (0, 0))],
        out_specs=[pl.BlockSpec((tt, gw), lambda b, i: (b * nt_ + i, 0)),
                   pl.BlockSpec((1, CONV_HALO, gw), lambda b, i: (b, 0, 0))],
        out_shape=[jax.ShapeDtypeStruct((nb * t, gw), BF16),
                   jax.ShapeDtypeStruct((nb, CONV_HALO, gw), F32)],
        scratch_shapes=[pltpu.VMEM((CONV_HALO + tt, gw), F32)],
        compiler_params=_cp("arbitrary", "arbitrary"),
        name="conv_prompt",
    )(proj, proj, prev, w, vec)


def _sample_elem_kernel(pr_ref, pk_ref, pv_ref, pl_ref, sp_ref, vec_ref, wup_ref, e_ref,
                        su_ref, sv_ref, sgp_ref, ca_ref, cg_ref, cprev_ref, cw_ref, cvec_ref,
                        r_ref, w_ref, k_ref, v_ref, kk_ref, b_ref, g_ref, bonus_ref,
                        yc_ref, sgv_ref, yd_ref, cnew_ref):
    gw = GROUP_WIDTH
    sp = sp_ref[...]
    r, lw, kmod, v, kk, b, g, bonus = _rwkv_elem(
        pr_ref[...], pk_ref[...], pv_ref[...], pl_ref[...],
        sp[:, 0:gw], sp[:, gw:2 * gw], sp[:, 2 * gw:3 * gw], sp[:, 3 * gw:3 * gw + LANES],
        vec_ref[...], wup_ref, e_ref[...], precise=True)
    for ref, val in ((r_ref, r), (w_ref, jnp.exp(lw)), (k_ref, kmod), (v_ref, v), (kk_ref, kk),
                     (b_ref, b), (g_ref, g), (bonus_ref, bonus)):
        ref[...] = val
    u = jax.nn.gelu(su_ref[...])
    sv = _layer_norm(jax.nn.gelu(sv_ref[...]), sgp_ref[0:1, :], sgp_ref[1:2, :])
    sgv_ref[...] = sv
    yc_ref[...] = (u * (sgp_ref[2:3, :] * sv + sgp_ref[3:4, :])).astype(yc_ref.dtype)
    z = ca_ref[...] * jax.nn.sigmoid(cg_ref[...])
    nprev = CONV_WIDTH - 1
    acc = cw_ref[nprev:nprev + 1, :] * z
    for i in range(nprev):
        acc = acc + cw_ref[i:i + 1, :] * cprev_ref[i]
        if i > 0:
            cnew_ref[i - 1] = cprev_ref[i]
    cnew_ref[nprev - 1] = z
    y = _layer_norm(acc + cvec_ref[0:1, :], cvec_ref[1:2, :], cvec_ref[2:3, :])
    yd_ref[...] = (y * jax.nn.sigmoid(y)).astype(yd_ref.dtype)


def _sample_wkv_kernel(r_ref, w_ref, k_ref, v_ref, kk_ref, b_ref, g_ref, bonus_ref, ln_ref, s_ref,
                       ya_ref, st_ref):
    hd = HEAD_DIM
    r, w, k, v, kk, b, g, bonus = (x[0] for x in (r_ref, w_ref, k_ref, v_ref, kk_ref, b_ref, g_ref, bonus_ref))
    s = s_ref[0]
    eye = (lax.broadcasted_iota(jnp.int32, (hd, hd), 0) == lax.broadcasted_iota(jnp.int32, (hd, hd), 1))
    eye = eye.astype(F32)[None]
    v_col = jnp.sum(eye * v, axis=2, keepdims=True)
    sa = jnp.sum(s * (-kk), axis=2, keepdims=True)
    s_new = s * w + sa * b + v_col * k
    st_ref[0] = s_new
    o_col = jnp.sum(s_new * r, axis=2, keepdims=True)
    o = jnp.sum(o_col * eye, axis=1, keepdims=True)
    mu = jnp.mean(o, axis=-1, keepdims=True)
    d = o - mu
    var = jnp.mean(d * d, axis=-1, keepdims=True)
    on = d * lax.rsqrt(var + GN_EPS) * ln_ref[0] + ln_ref[1]
    ya_ref[0] = (on + bonus) * g


def _sample_mixers(proj, shift_prev, s0, conv_prev_t, vec, wup, e, sgp, cw, cvec):
    nb = proj.shape[0]
    gw, nh, hd = GROUP_WIDTH, N_HEADS, HEAD_DIM
    col = lambda cb: pl.BlockSpec((nb, gw), lambda i: (0, cb))
    full = lambda x: pl.BlockSpec(x.shape, lambda i: (0,) * x.ndim)
    row_f32 = jax.ShapeDtypeStruct((nb, gw), F32)
    outs = pl.pallas_call(
        _sample_elem_kernel,
        grid=(1,),
        in_specs=[col(CB_R), col(CB_RK), col(CB_RV),
                  pl.BlockSpec((nb, LANES), lambda i: (0, CB_L * (gw // LANES))),
                  full(shift_prev), full(vec), full(wup), full(e),
                  col(CB_SU), col(CB_SV), full(sgp), col(CB_CA), col(CB_CG),
                  full(conv_prev_t), full(cw), full(cvec)],
        out_specs=[pl.BlockSpec((nb, gw), lambda i: (0, 0))] * 11 + [full(conv_prev_t)],
        out_shape=[row_f32] * 11 + [jax.ShapeDtypeStruct(conv_prev_t.shape, F32)],
        compiler_params=_cp("arbitrary"),
        name="sample_elem",
    )(proj, proj, proj, proj, shift_prev, vec, wup, e, proj, proj, sgp, proj, proj, conv_prev_t, cw, cvec)
    heads = [x.reshape(nb, nh, 1, hd) for x in outs[:8]]
    yc, sgv, yd, conv_new_t = outs[8:]
    ln = jnp.stack([vec[V_LNG].reshape(nh, 1, hd), vec[V_LNB].reshape(nh, 1, hd)])
    hspec = pl.BlockSpec((1, nh, 1, hd), lambda b: (b, 0, 0, 0))
    sspec = pl.BlockSpec((1, nh, hd, hd), lambda b: (b, 0, 0, 0))
    ya, s_new = pl.pallas_call(
        _sample_wkv_kernel,
        grid=(nb,),
        in_specs=[hspec] * 8 + [pl.BlockSpec(ln.shape, lambda b: (0, 0, 0, 0)), sspec],
        out_specs=[hspec, sspec],
        out_shape=[jax.ShapeDtypeStruct((nb, nh, 1, hd), F32), jax.ShapeDtypeStruct(s0.shape, F32)],
        compiler_params=_cp("arbitrary"),
        name="sample_wkv",
    )(*heads, ln, s0)
    return ya.reshape(nb, gw), s_new, yc, sgv, yd, conv_new_t


PAGES_PER_STEP = 32


def _moba_scores_kernel(pt_ref, q_ref, *refs):
    pages, o_ref = refs[:-1], refs[-1]
    ps = pages[0].shape[4]
    qb = jnp.broadcast_to(q_ref[0] * (HEAD_DIM ** -0.5), pages[0].shape[2:])
    for i, p_ref in enumerate(pages):
        rows = [jnp.sum(p_ref[0, 0, h] * qb[h], axis=0, keepdims=True) for h in range(N_HEADS)]
        o_ref[0, :, i * ps:(i + 1) * ps] = jnp.concatenate(rows, axis=0)


def _moba_select_kernel(s_ref, q_ref, kn_ref, bi_ref, slope_ref, p_ref, p0_ref, *, n_past_blocks):
    s = s_ref[...]
    length = s.shape[1]
    bi = bi_ref[...]
    s0 = jnp.sum(q_ref[...] * (HEAD_DIM ** -0.5) * kn_ref[...], axis=1, keepdims=True)
    gate = sum(_dot(p, bi) for p in _split3(s)) * (1.0 / MOBA_BLOCK)
    lane = lax.broadcasted_iota(jnp.int32, gate.shape, 1)
    gate = jnp.where(lane < n_past_blocks, gate, -jnp.inf)
    sel = jnp.zeros(gate.shape, jnp.bool_)
    for _ in range(min(MOBA_TOPK, n_past_blocks)):
        m = jnp.max(gate, axis=1, keepdims=True)
        idx = jnp.min(jnp.where(gate == m, lane, LANES), axis=1, keepdims=True)
        pick = lane == idx
        sel = sel | pick
        gate = jnp.where(pick, -jnp.inf, gate)
    picked = lax.dot_general(sel.astype(BF16), bi, (((1,), (1,)), ((), ())),
                             preferred_element_type=F32) > 0.5
    pos = lax.broadcasted_iota(jnp.int32, s.shape, 1)
    dist = (length - pos).astype(F32)
    logit = jnp.where(picked, s - slope_ref[:, 0:1] * dist, -jnp.inf)
    m = jnp.maximum(jnp.max(logit, axis=1, keepdims=True), s0)
    p = jnp.exp(logit - m)
    p0 = jnp.exp(s0 - m)
    inv = 1.0 / (jnp.sum(p, axis=1, keepdims=True) + p0)
    p_ref[...] = p * inv
    p0_ref[...] = jnp.broadcast_to(p0 * inv, p0_ref.shape)


def _moba_pv_kernel(pt_ref, p_ref, p0_ref, vn_ref, *refs, n_steps):
    pages, o_ref, acc = refs[:-2], refs[-2], refs[-1]
    j = pl.program_id(1)
    ps = pages[0].shape[4]

    @pl.when(j == 0)
    def _():
        acc[...] = jnp.zeros_like(acc)

    for h in range(N_HEADS):
        tot = acc[h]
        for i, v_ref in enumerate(pages):
            tot = tot + v_ref[0, 0, h] * p_ref[0, h:h + 1, i * ps:(i + 1) * ps]
        acc[h] = tot

    @pl.when(j == n_steps - 1)
    def _():
        for h in range(N_HEADS):
            o_ref[0, h] = (jnp.sum(acc[h], axis=1, keepdims=True)
                           + p0_ref[0, h:h + 1, 0:1] * vn_ref[0, h])


def _moba_sample(proj, cache_k, cache_v, layer, page_table):
    nb, n_pages = page_table.shape
    ps, gw, nh, hd = cache_k.shape[2], GROUP_WIDTH, N_HEADS, HEAD_DIM
    length = n_pages * ps
    assert length % MOBA_BLOCK == 0
    n_past_blocks = length // MOBA_BLOCK
    pp = min(PAGES_PER_STEP, n_pages)
    n_steps = n_pages // pp
    q, kn, vn = (proj[:, cb * gw:(cb + 1) * gw] for cb in (CB_Q, CB_K, CB_V))
    cache_kt = cache_k.transpose(0, 1, 3, 4, 2)
    cache_vt = cache_v.transpose(0, 1, 3, 4, 2)
    col4 = lambda x: x.reshape(nb, nh, hd, 1)
    col_spec = pl.BlockSpec((1, nh, hd, 1), lambda b, j, pt: (b, 0, 0, 0))

    def page_spec(i):
        return pl.BlockSpec((1, 1, nh, hd, ps), lambda b, j, pt: (layer, pt[b, j * pp + i], 0, 0, 0))

    scores = pl.pallas_call(
        _moba_scores_kernel,
        grid_spec=pltpu.PrefetchScalarGridSpec(
            num_scalar_prefetch=1,
            grid=(nb, n_steps),
            in_specs=[col_spec] + [page_spec(i) for i in range(pp)],
            out_specs=pl.BlockSpec((1, nh, pp * ps), lambda b, j, pt: (b, 0, j)),
        ),
        out_shape=jax.ShapeDtypeStruct((nb, nh, length), F32),
        compiler_params=_cp("arbitrary", "arbitrary"),
        name="moba_scores",
    )(page_table, col4(q), *([cache_kt] * pp))

    bi = (jnp.arange(length)[:, None] // MOBA_BLOCK == jnp.arange(LANES)[None, :]).astype(BF16)
    rows = nb * nh
    tr = min(rows, 128)
    slopes = jnp.broadcast_to(
        jnp.tile(jnp.array([_alibi_slope(h) for h in range(nh)], F32), nb)[:, None], (rows, LANES))
    p, p0 = pl.pallas_call(
        functools.partial(_moba_select_kernel, n_past_blocks=n_past_blocks),
        grid=(rows // tr,),
        in_specs=[pl.BlockSpec((tr, length), lambda i: (i, 0)),
                  pl.BlockSpec((tr, hd), lambda i: (i, 0)),
                  pl.BlockSpec((tr, hd), lambda i: (i, 0)),
                  pl.BlockSpec((length, LANES), lambda i: (0, 0)),
                  pl.BlockSpec((tr, LANES), lambda i: (i, 0))],
        out_specs=[pl.BlockSpec((tr, length), lambda i: (i, 0)),
                   pl.BlockSpec((tr, LANES), lambda i: (i, 0))],
        out_shape=[jax.ShapeDtypeStruct((rows, length), F32), jax.ShapeDtypeStruct((rows, LANES), F32)],
        compiler_params=_cp("arbitrary"),
        name="moba_select",
    )(scores.reshape(rows, length), q.reshape(rows, hd), kn.reshape(rows, hd), bi, slopes)

    out = pl.pallas_call(
        functools.partial(_moba_pv_kernel, n_steps=n_steps),
        grid_spec=pltpu.PrefetchScalarGridSpec(
            num_scalar_prefetch=1,
            grid=(nb, n_steps),
            in_specs=[pl.BlockSpec((1, nh, pp * ps), lambda b, j, pt: (b, 0, j)),
                      pl.BlockSpec((1, nh, LANES), lambda b, j, pt: (b, 0, 0)),
                      col_spec]
                     + [page_spec(i) for i in range(pp)],
            out_specs=col_spec,
            scratch_shapes=[pltpu.VMEM((nh, hd, ps), F32)],
        ),
        out_shape=jax.ShapeDtypeStruct((nb, nh, hd, 1), F32),
        compiler_params=_cp("arbitrary", "arbitrary"),
        name="moba_pv",
    )(page_table, p.reshape(nb, nh, length), p0.reshape(nb, nh, LANES), col4(vn), *([cache_vt] * pp))
    return out.reshape(nb, gw)


RWKV_PASSES = (1, 1)


def _shift_layout(shift):
    n = shift.shape[0]
    pad = jnp.zeros((n, LANES - 3 * TM_LORA), F32)
    return jnp.concatenate([shift, pad], axis=1)


def _layer_params(l, w_in, w_out, w_ffn_in, w_ffn_out, tm_mu, tm_w0, tm_w_up, tm_a0, tm_a_up, tm_g_up,
                  tm_k_k, tm_k_a, tm_r_k, tm_ln_g, tm_ln_b, sg_ws, sg_bs, sg_ln_g, sg_ln_b,
                  cv_w, cv_b, cv_ln_g, cv_ln_b):
    gw, lr, hd = GROUP_WIDTH, TM_LORA, HEAD_DIM
    wi = w_in[l]
    w_in_f32 = jnp.concatenate(
        [wi[:, A_PROJ:], wi[:, :A_PROJ], jnp.zeros((wi.shape[0], IN_PAD - IN_WIDTH), F32)], axis=1)
    mu = tm_mu[l]
    mu_l = jnp.concatenate([mu[3 * gw:], jnp.zeros((gw - 3 * lr,), F32)])
    vec = jnp.stack([mu[0:gw], mu[gw:2 * gw], mu[2 * gw:3 * gw], tm_w0[l], tm_a0[l], tm_k_k[l], tm_k_a[l],
                     tm_r_k[l].reshape(gw), tm_ln_g[l], tm_ln_b[l], mu_l]
                    + [jnp.zeros((gw,), F32)] * 5)
    wup = jnp.zeros((3, LANES, gw), F32)
    wup = wup.at[0, 0:lr].set(tm_w_up[l]).at[1, lr:2 * lr].set(tm_a_up[l]).at[2, 2 * lr:3 * lr].set(tm_g_up[l])
    sg_bias = jnp.repeat(sg_bs[l].T, hd, axis=1)
    sg_ln = jnp.stack([sg_ln_g[l], sg_ln_b[l]] + [jnp.zeros((gw,), F32)] * 6)
    sgp = jnp.stack([sg_ln_g[l], sg_ln_b[l], jnp.repeat(sg_ws[l][:, 0, 0], hd), jnp.repeat(sg_bs[l][:, 0], hd)]
                    + [jnp.zeros((gw,), F32)] * 4)
    cw = jnp.concatenate([cv_w[l], jnp.zeros((1, gw), F32)], axis=0)
    cvec = jnp.stack([cv_b[l], cv_ln_g[l], cv_ln_b[l]] + [jnp.zeros((gw,), F32)] * 5)
    return dict(vec=vec, wup=wup, sg_w=sg_ws[l], sg_bias=sg_bias, sg_ln=sg_ln, sgp=sgp, cw=cw, cvec=cvec,
                w_in_f32=w_in_f32[None], dense_f32=(w_out, w_ffn_in, w_ffn_out))


def _prompt_dense_tail(x, ys, mod, p, g_ffn, layer):
    ng, t, d = x.shape
    m = ng * t
    w_out, w_ffn_in, w_ffn_out = p['dense_f32']
    x1, h2 = _mix_out(ys, w_out[layer].astype(BF16), x.reshape(m, d), mod, g_ffn, 512, t)
    hid = _mm_swiglu(h2, w_ffn_in, layer, 1024, 512)
    x2 = _mm_rows([hid], w_ffn_out[layer].astype(BF16), 512, 512, res=x1, gate=mod, gate_col=5, rows_per_group=t)
    return x2.reshape(ng, t, d)


def _prompt_layer(x, mod, p, g_mix, g_ffn, e, layer):
    nb, t, d = x.shape
    m = nb * t
    gw, nh, hd = GROUP_WIDTH, N_HEADS, HEAD_DIM
    h = _norm_mod(x, g_mix, mod, 0, 1, 512).reshape(m, d)
    proj = _mm([h], p['w_in_f32'], 0, 1024, 512)
    shift0 = jnp.zeros((nb, 1, 3 * gw + LANES), F32)
    s0 = jnp.zeros((nb, nh, hd, hd), F32)
    ya, wkv = _rwkv_prompt(proj, nb, t, shift0, s0, p['vec'], p['wup'], e, RWKV_PASSES)
    yb = _moba_prompt(proj, nb, t)
    yc = _sgu_prompt(proj, m, p['sg_w'], p['sg_bias'], p['sg_ln'])
    yd, hist = _conv_prompt(proj, nb, t, jnp.zeros((nb, CONV_HALO, gw), F32), p['cw'], p['cvec'])
    x2 = _prompt_dense_tail(x, [ya.reshape(m, gw), yb, yc, yd], mod, p, g_ffn, layer)
    proj3 = proj.reshape(nb, t, IN_PAD)
    last = proj3[:, t - 1]
    shift = jnp.concatenate([last[:, CB_R * gw:CB_L * gw], last[:, CB_L * gw:CB_L * gw + 3 * TM_LORA]], axis=1)
    new_k = proj3[:, :, CB_K * gw:(CB_K + 1) * gw].reshape(nb, t, nh, hd)
    new_v = proj3[:, :, CB_V * gw:(CB_V + 1) * gw].reshape(nb, t, nh, hd)
    return x2, (shift, wkv, new_k, new_v, hist[:, CONV_HALO - (CONV_WIDTH - 1):])


def _sample_layer(x, mod, p, g_mix, g_ffn, e, layer, shift_prev, wkv_prev, conv_prev, cache_k, cache_v,
                  page_table):
    nb, t, d = x.shape
    gw, nh, hd = GROUP_WIDTH, N_HEADS, HEAD_DIM
    xg = x.reshape(1, nb, d)
    h = _norm_mod(xg, g_mix, mod, 0, 1, nb, dtype=F32).reshape(nb, d)
    proj = _mm_precise([h], p['w_in_f32'], 0, 512)
    ya, wkv, yc, sgv, yd, conv_new_t = _sample_mixers(
        proj, _shift_layout(shift_prev), wkv_prev, conv_prev.transpose(1, 0, 2),
        p['vec'], p['wup'], e, p['sgp'], p['cw'], p['cvec'])
    yb = _moba_sample(proj, cache_k, cache_v, layer, page_table)
    w_out, w_ffn_in, w_ffn_out = p['dense_f32']
    x1 = _mm_precise([ya, yb, yc, yd], w_out, layer, 512, mode="gated", res=x.reshape(nb, d), gate=mod, gate_col=2)
    h2 = _norm_mod(x1.reshape(1, nb, d), g_ffn, mod, 3, 4, nb, dtype=F32).reshape(nb, d)
    hid = _mm_precise([h2], w_ffn_in, layer, 512, mode="swiglu")
    x2 = _mm_precise([hid], w_ffn_out, layer, 256, mode="gated", res=x1, gate=mod, gate_col=5)
    shift = jnp.concatenate([proj[:, CB_R * gw:CB_L * gw], proj[:, CB_L * gw:CB_L * gw + 3 * TM_LORA]], axis=1)
    new_k = proj[:, CB_K * gw:(CB_K + 1) * gw].reshape(nb, 1, nh, hd)
    new_v = proj[:, CB_V * gw:(CB_V + 1) * gw].reshape(nb, 1, nh, hd)
    return x2.reshape(nb, t, d), (shift, wkv, new_k, new_v, sgv.reshape(nb, 1, gw), conv_new_t.transpose(1, 0, 2))


def kernel(x_prompt, x_sample, c_prompt, c_sample, cache_k_b, cache_v_b, state_a_wkv, state_a_shift, state_d_conv, page_table, w_ada, b_ada, g_norm_mix, w_in, w_out, tm_mu, tm_w0, tm_w_up, tm_a0, tm_a_up, tm_g_up, tm_k_k, tm_k_a, tm_r_k, tm_ln_g, tm_ln_b, sg_ws, sg_bs, sg_ln_g, sg_ln_b, cv_w, cv_b, cv_ln_g, cv_ln_b, g_norm_ffn, w_ffn_in, w_ffn_out, g_norm_final):
    depth = w_in.shape[0]
    bp, ns = x_prompt.shape[0], x_sample.shape[0]
    gw = GROUP_WIDTH
    rows = bp + ns
    rows_pad = -(-rows // 8) * 8
    c_all = jnp.concatenate([c_prompt, c_sample, jnp.zeros((rows_pad - rows, c_prompt.shape[1]), F32)], axis=0)
    mod = _ada(c_all, w_ada, b_ada)
    e = _head_ones()
    xp, xs = x_prompt, x_sample
    st_p, st_s = [], []
    for l in range(depth):
        p = _layer_params(l, w_in, w_out, w_ffn_in, w_ffn_out, tm_mu, tm_w0, tm_w_up, tm_a0, tm_a_up, tm_g_up,
                          tm_k_k, tm_k_a, tm_r_k, tm_ln_g, tm_ln_b, sg_ws, sg_bs, sg_ln_g, sg_ln_b,
                          cv_w, cv_b, cv_ln_g, cv_ln_b)
        mod_p = mod[l, :bp][:, None, :]
        mod_s = mod[l, bp:rows][None]
        xp, sp = _prompt_layer(xp, mod_p, p, g_norm_mix[l], g_norm_ffn[l], e, l)
        xs, ss = _sample_layer(xs, mod_s, p, g_norm_mix[l], g_norm_ffn[l], e, l, state_a_shift[l],
                               state_a_wkv[l], state_d_conv[l], cache_k_b, cache_v_b, page_table)
        st_p.append(sp)
        st_s.append(ss)
    y_prompt = _norm(xp, g_norm_final, 512)
    y_sample = _norm(xs.reshape(1, ns, -1), g_norm_final, ns).reshape(xs.shape)
    stack = lambda sts, i: jnp.stack([s[i] for s in sts])
    return (y_prompt, y_sample, stack(st_p, 2), stack(st_p, 3), stack(st_s, 2), stack(st_s, 3),
            stack(st_p, 1), stack(st_s, 1), stack(st_p, 0), stack(st_s, 0), stack(st_p, 4), stack(st_s, 5),
            stack(st_s, 4))
```

```python
import functools

import jax
import jax.numpy as jnp
from jax import lax
from jax.experimental import pallas as pl
from jax.experimental.pallas import tpu as pltpu

F32, BF16 = jnp.float32, jnp.bfloat16

D_MODEL = 2048
GROUP_WIDTH = 512
N_HEADS = 8
HEAD_DIM = 64
TM_LORA = 32
A_PROJ = 3 * GROUP_WIDTH + 3 * TM_LORA
IN_WIDTH = A_PROJ + 7 * GROUP_WIDTH
IN_PAD = 11 * GROUP_WIDTH
MOBA_BLOCK = 256
MOBA_TOPK = 3
SG_CHUNK = 128
CONV_WIDTH = 31
CONV_HALO = 32
RMS_EPS = 1e-6
LN_EPS = 1e-5
GN_EPS = 64e-5
RWKV_CHUNK = 64
RWKV_CHUNKS_PER_STEP = 4
INV_BLOCK = 16
LANES = 128

CB_Q, CB_K, CB_V, CB_SU, CB_SV, CB_CA, CB_CG, CB_R, CB_RK, CB_RV, CB_L = range(11)
(V_MU_R, V_MU_K, V_MU_V, V_W0, V_A0, V_KK, V_KA, V_RK, V_LNG, V_LNB, V_MU_L) = range(11)

VMEM_LIMIT = 56 * 1024 * 1024


def _cp(*sem):
    return pltpu.CompilerParams(dimension_semantics=sem, vmem_limit_bytes=VMEM_LIMIT)


def _dot(a, b):
    return jnp.dot(a, b, preferred_element_type=F32)


def _dot_nt(a, b):
    return lax.dot_general(a, b, (((1,), (1,)), ((), ())), preferred_element_type=F32)


def _split2(x):
    hi = x.astype(BF16)
    return hi, (x - hi.astype(F32)).astype(BF16)


def _split3(x):
    h1 = x.astype(BF16)
    r1 = x - h1.astype(F32)
    h2 = r1.astype(BF16)
    return h1, h2, (r1 - h2.astype(F32)).astype(BF16)


def _dot_exact_rhs(x, e):
    hi, lo = _split2(x)
    return _dot(hi, e) + _dot(lo, e)


def _head_ones():
    r = lax.broadcasted_iota(jnp.int32, (GROUP_WIDTH, GROUP_WIDTH), 0) // HEAD_DIM
    c = lax.broadcasted_iota(jnp.int32, (GROUP_WIDTH, GROUP_WIDTH), 1) // HEAD_DIM
    return (r == c).astype(BF16)


def _layer_norm(x, g, b):
    mu = jnp.mean(x, axis=-1, keepdims=True)
    d = x - mu
    var = jnp.mean(d * d, axis=-1, keepdims=True)
    return d * lax.rsqrt(var + LN_EPS) * g + b


def _softplus(x):
    return jnp.maximum(x, 0.0) + jnp.log1p(jnp.exp(-jnp.abs(x)))


def _dot3(x, w, dot=_dot):
    rows = x.shape[0]
    x_hi = x.astype(BF16).astype(F32)
    xs = jnp.concatenate([x_hi, x - x_hi], axis=0).astype(BF16)
    wh, wl = _split2(w)
    r = dot(xs, wh)
    return r[:rows] + r[rows:] + dot(x_hi.astype(BF16), wl)


def _ada_kernel(c_ref, w_ref, b_ref, o_ref):
    c = c_ref[...]
    o_ref[0] = _dot3(c * jax.nn.sigmoid(c), w_ref[0]) + b_ref[0]


def _ada(c_all, w_ada, b_ada):
    depth, d, n = w_ada.shape
    r = c_all.shape[0]
    tn = 512
    return pl.pallas_call(
        _ada_kernel,
        grid=(depth, n // tn),
        in_specs=[pl.BlockSpec((r, d), lambda l, j: (0, 0)),
                  pl.BlockSpec((1, d, tn), lambda l, j: (l, 0, j)),
                  pl.BlockSpec((1, 1, tn), lambda l, j: (l, 0, j))],
        out_specs=pl.BlockSpec((1, r, tn), lambda l, j: (l, 0, j)),
        out_shape=jax.ShapeDtypeStruct((depth, r, n), F32),
        compiler_params=_cp("arbitrary", "arbitrary"),
        name="ada",
    )(c_all, w_ada, b_ada.reshape(depth, 1, n))


def _norm_mod_kernel(x_ref, g_ref, sc_ref, sh_ref, o_ref):
    x = x_ref[0]
    h = x * lax.rsqrt(jnp.mean(x * x, axis=-1, keepdims=True) + RMS_EPS) * g_ref[...]
    o_ref[0] = (h * (1.0 + sc_ref[0]) + sh_ref[0]).astype(o_ref.dtype)


def _norm_mod(x, g, mod, sh_col, sc_col, tt, dtype=BF16):
    ng, t, d = x.shape
    r = mod.shape[1]
    return pl.pallas_call(
        _norm_mod_kernel,
        grid=(ng, t // tt),
        in_specs=[pl.BlockSpec((1, tt, d), lambda b, i: (b, i, 0)),
                  pl.BlockSpec((1, d), lambda b, i: (0, 0)),
                  pl.BlockSpec((1, r, d), lambda b, i: (b, 0, sc_col)),
                  pl.BlockSpec((1, r, d), lambda b, i: (b, 0, sh_col))],
        out_specs=pl.BlockSpec((1, tt, d), lambda b, i: (b, i, 0)),
        out_shape=jax.ShapeDtypeStruct((ng, t, d), dtype),
        compiler_params=_cp("arbitrary", "arbitrary"),
        name="norm_mod",
    )(x, g.reshape(1, d), mod, mod)


def _norm_kernel(x_ref, g_ref, o_ref):
    x = x_ref[0]
    o_ref[0] = x * lax.rsqrt(jnp.mean(x * x, axis=-1, keepdims=True) + RMS_EPS) * g_ref[...]


def _norm(x, g, tt):
    ng, t, d = x.shape
    return pl.pallas_call(
        _norm_kernel,
        grid=(ng, t // tt),
        in_specs=[pl.BlockSpec((1, tt, d), lambda b, i: (b, i, 0)),
                  pl.BlockSpec((1, d), lambda b, i: (0, 0))],
        out_specs=pl.BlockSpec((1, tt, d), lambda b, i: (b, i, 0)),
        out_shape=jax.ShapeDtypeStruct((ng, t, d), F32),
        compiler_params=_cp("arbitrary", "arbitrary"),
        name="norm_final",
    )(x, g.reshape(1, d))


def _mm_kernel(*refs, n_x, gated, w_t):
    xs, w_ref = refs[:n_x], refs[n_x]
    o_ref, wb = refs[-2], refs[-1]

    @pl.when(pl.program_id(1) == 0)
    def _():
        wb[...] = w_ref[0].astype(BF16)

    acc = None
    k0 = 0
    for x_ref in xs:
        k = x_ref.shape[1]
        part = _dot_nt(x_ref[...], wb[:, k0:k0 + k]) if w_t else _dot(x_ref[...], wb[k0:k0 + k, :])
        acc = part if acc is None else acc + part
        k0 += k
    if gated:
        res_ref, gt_ref = refs[n_x + 1], refs[n_x + 2]
        o_ref[...] = res_ref[...] + gt_ref[0] * acc
    else:
        o_ref[...] = acc


def _mm(xs, w, layer, tm, tn, res=None, gate=None, gate_col=0, rows_per_group=None, w_t=False):
    m = xs[0].shape[0]
    kdim, n = (w.shape[2], w.shape[1]) if w_t else (w.shape[1], w.shape[2])
    gated = res is not None
    in_specs = [pl.BlockSpec((tm, x.shape[1]), lambda j, i: (i, 0)) for x in xs]
    in_specs.append(pl.BlockSpec((1, tn, kdim), lambda j, i: (layer, j, 0)) if w_t
                    else pl.BlockSpec((1, kdim, tn), lambda j, i: (layer, 0, j)))
    args = list(xs) + [w]
    if gated:
        r = gate.shape[1]
        tiles_per_group = rows_per_group // tm
        nj = n // tn
        in_specs.append(pl.BlockSpec((tm, tn), lambda j, i: (i, j)))
        in_specs.append(pl.BlockSpec((1, r, tn), lambda j, i: (i // tiles_per_group, 0, gate_col * nj + j)))
        args += [res, gate]
    return pl.pallas_call(
        functools.partial(_mm_kernel, n_x=len(xs), gated=gated, w_t=w_t),
        grid=(n // tn, m // tm),
        in_specs=in_specs,
        out_specs=pl.BlockSpec((tm, tn), lambda j, i: (i, j)),
        out_shape=jax.ShapeDtypeStruct((m, n), F32),
        scratch_shapes=[pltpu.VMEM((tn, kdim) if w_t else (kdim, tn), BF16)],
        compiler_params=_cp("arbitrary", "arbitrary"),
        name="mm_gated" if gated else "mm",
    )(*args)


def _mm_rows_kernel(*refs, n_x, gated):
    xs, w_ref = refs[:n_x], refs[n_x]
    o_ref = refs[-1]
    acc = None
    k0 = 0
    for x_ref in xs:
        k = x_ref.shape[1]
        part = _dot(x_ref[...], w_ref[k0:k0 + k, :])
        acc = part if acc is None else acc + part
        k0 += k
    if gated:
        res_ref, gt_ref = refs[n_x + 1], refs[n_x + 2]
        o_ref[...] = res_ref[...] + gt_ref[0] * acc
    else:
        o_ref[...] = acc


def _mm_rows(xs, w, tm, tn, res=None, gate=None, gate_col=0, rows_per_group=None):
    m = xs[0].shape[0]
    n = w.shape[1]
    gated = res is not None
    in_specs = [pl.BlockSpec((tm, x.shape[1]), lambda i, j: (i, 0)) for x in xs]
    in_specs.append(pl.BlockSpec((w.shape[0], tn), lambda i, j: (0, j)))
    args = list(xs) + [w]
    if gated:
        r = gate.shape[1]
        tiles_per_group = rows_per_group // tm
        nj = n // tn
        in_specs.append(pl.BlockSpec((tm, tn), lambda i, j: (i, j)))
        in_specs.append(pl.BlockSpec((1, r, tn), lambda i, j: (i // tiles_per_group, 0, gate_col * nj + j)))
        args += [res, gate]
    return pl.pallas_call(
        functools.partial(_mm_rows_kernel, n_x=len(xs), gated=gated),
        grid=(m // tm, n // tn),
        in_specs=in_specs,
        out_specs=pl.BlockSpec((tm, tn), lambda i, j: (i, j)),
        out_shape=jax.ShapeDtypeStruct((m, n), F32),
        compiler_params=_cp("arbitrary", "arbitrary"),
        name="mm_rows_gated" if gated else "mm_rows",
    )(*args)


def _mix_out_kernel(*refs, n_x):
    xs, (w_ref, res_ref, gt_ref, g_ref, sc_ref, sh_ref, x1_ref, h_ref) = refs[:n_x], refs[n_x:]
    acc = None
    k0 = 0
    for x_ref in xs:
        k = x_ref.shape[1]
        part = _dot(x_ref[...], w_ref[k0:k0 + k, :])
        acc = part if acc is None else acc + part
        k0 += k
    x1 = res_ref[...] + gt_ref[0] * acc
    x1_ref[...] = x1
    h = x1 * lax.rsqrt(jnp.mean(x1 * x1, axis=-1, keepdims=True) + RMS_EPS) * g_ref[...]
    h_ref[...] = (h * (1.0 + sc_ref[0]) + sh_ref[0]).astype(h_ref.dtype)


def _mix_out(xs, w, res, mod, g, tm, rows_per_group):
    m, d = res.shape
    tiles_per_group = rows_per_group // tm
    row = lambda i: (i, 0)
    modspec = lambda col: pl.BlockSpec((1, 1, d), lambda i: (i // tiles_per_group, 0, col))
    return pl.pallas_call(
        functools.partial(_mix_out_kernel, n_x=len(xs)),
        grid=(m // tm,),
        in_specs=[pl.BlockSpec((tm, x.shape[1]), row) for x in xs]
                 + [pl.BlockSpec(w.shape, lambda i: (0, 0)), pl.BlockSpec((tm, d), row), modspec(2),
                    pl.BlockSpec((1, d), lambda i: (0, 0)), modspec(4), modspec(3)],
        out_specs=[pl.BlockSpec((tm, d), row), pl.BlockSpec((tm, d), row)],
        out_shape=[jax.ShapeDtypeStruct((m, d), F32), jax.ShapeDtypeStruct((m, d), BF16)],
        compiler_params=_cp("arbitrary"),
        name="mix_out",
    )(*xs, w, res, mod, g.reshape(1, d), mod, mod)


def _mm_precise_kernel(*refs, n_x, mode):
    xs = refs[:n_x]
    o_ref = refs[-1]
    if mode == "swiglu":
        x = xs[0][...]
        g = _dot3(x, refs[n_x][0])
        o_ref[...] = g * jax.nn.sigmoid(g) * _dot3(x, refs[n_x + 1][0])
        return
    w_ref = refs[n_x]
    if mode == "plain_t":
        o_ref[...] = _dot3(xs[0][...], w_ref[0], _dot_nt)
        return
    acc = None
    k0 = 0
    for x_ref in xs:
        k = x_ref.shape[1]
        part = _dot3(x_ref[...], w_ref[0, k0:k0 + k, :])
        acc = part if acc is None else acc + part
        k0 += k
    if mode == "gated":
        res_ref, gt_ref = refs[n_x + 1], refs[n_x + 2]
        o_ref[...] = res_ref[...] + gt_ref[0] * acc
    else:
        o_ref[...] = acc


def _mm_precise(xs, w, layer, tn, mode="plain", res=None, gate=None, gate_col=0):
    m = xs[0].shape[0]
    kdim = w.shape[2] if mode == "plain_t" else w.shape[1]
    n = {"swiglu": w.shape[2] // 2, "plain_t": w.shape[1]}.get(mode, w.shape[2])
    nj = n // tn
    in_specs = [pl.BlockSpec((m, x.shape[1]), lambda j: (0, 0)) for x in xs]
    in_specs.append(pl.BlockSpec((1, tn, kdim), lambda j: (layer, j, 0)) if mode == "plain_t"
                    else pl.BlockSpec((1, kdim, tn), lambda j: (layer, 0, j)))
    args = list(xs) + [w]
    if mode == "swiglu":
        in_specs.append(pl.BlockSpec((1, kdim, tn), lambda j: (layer, 0, nj + j)))
        args.append(w)
    if mode == "gated":
        in_specs.append(pl.BlockSpec((m, tn), lambda j: (0, j)))
        in_specs.append(pl.BlockSpec((1, m, tn), lambda j: (0, 0, gate_col * nj + j)))
        args += [res, gate]
    return pl.pallas_call(
        functools.partial(_mm_precise_kernel, n_x=len(xs), mode=mode),
        grid=(nj,),
        in_specs=in_specs,
        out_specs=pl.BlockSpec((m, tn), lambda j: (0, j)),
        out_shape=jax.ShapeDtypeStruct((m, n), F32),
        compiler_params=_cp("arbitrary"),
        name="mm_precise_" + mode,
    )(*args)


def _swiglu_kernel(x_ref, wg_ref, wu_ref, o_ref, wgb, wub):
    @pl.when(pl.program_id(1) == 0)
    def _():
        wgb[...] = wg_ref[0].astype(BF16)
        wub[...] = wu_ref[0].astype(BF16)

    x = x_ref[...]
    g = _dot(x, wgb[...])
    u = _dot(x, wub[...])
    o_ref[...] = (g * jax.nn.sigmoid(g) * u).astype(o_ref.dtype)


def _mm_swiglu(x, w, layer, tm, tn):
    m, k = x.shape
    nh = w.shape[2] // 2
    nj = nh // tn
    return pl.pallas_call(
        _swiglu_kernel,
        grid=(nj, m // tm),
        in_specs=[pl.BlockSpec((tm, k), lambda j, i: (i, 0)),
                  pl.BlockSpec((1, k, tn), lambda j, i: (layer, 0, j)),
                  pl.BlockSpec((1, k, tn), lambda j, i: (layer, 0, nj + j))],
        out_specs=pl.BlockSpec((tm, tn), lambda j, i: (i, j)),
        out_shape=jax.ShapeDtypeStruct((m, nh), BF16),
        scratch_shapes=[pltpu.VMEM((k, tn), BF16), pltpu.VMEM((k, tn), BF16)],
        compiler_params=_cp("arbitrary", "arbitrary"),
        name="mm_swiglu",
    )(x, w, w)


def _rwkv_elem(pr, pk, pv, plo, qr, qk, qv, qlo, vec, wup_ref, e, precise=False):
    row = lambda i: vec[i:i + 1, :]
    xr = pr + (qr - pr) * row(V_MU_R)
    xk = pk + (qk - pk) * row(V_MU_K)
    xv = pv + (qv - pv) * row(V_MU_V)
    xl = plo + (qlo - plo) * vec[V_MU_L:V_MU_L + 1, 0:LANES]
    up = _dot3 if precise else (lambda x, w: _dot(x.astype(BF16), w.astype(BF16)))
    lw = up(jnp.tanh(xl), wup_ref[0])
    la = up(xl, wup_ref[1])
    g = up(jax.nn.sigmoid(xl), wup_ref[2])
    w_log = -_softplus(-(row(V_W0) + lw)) - 0.5
    log_decay = -jnp.exp(w_log)
    a = jax.nn.sigmoid(row(V_A0) + la)
    kk = xk * row(V_KK)
    kk = kk / jnp.maximum(jnp.sqrt(_dot_exact_rhs(kk * kk, e)), 1e-12)
    kmod = xk * (1.0 + (a - 1.0) * row(V_KA))
    bonus = _dot_exact_rhs(xr * kmod * row(V_RK), e) * xv
    return xr, log_decay, kmod, xv, kk, kk * a, g, bonus


def _rwkv_prep(pr_ref, pk_ref, pv_ref, pl_ref, sp_ref, vec_ref, wup_ref, e_ref, head_refs, carry):
    rt_ref, at_ref, bt_ref, kt_ref, v_ref, bh_ref, kh_ref, gt_ref = head_refs
    ti = pl.program_id(1)

    @pl.when(ti == 0)
    def _():
        carry[...] = sp_ref[0]

    tt = pr_ref.shape[0]
    first = lax.broadcasted_iota(jnp.int32, (tt, 1), 0) == 0
    gw = GROUP_WIDTH
    cur = (pr_ref[...], pk_ref[...], pv_ref[...], pl_ref[...])
    offs = (0, gw, 2 * gw, 3 * gw)
    prev = []
    for p, o in zip(cur, offs):
        w = p.shape[1]
        prev.append(jnp.where(first, carry[:, o:o + w], pltpu.roll(p, 1, axis=0)))
        carry[:, o:o + w] = p[tt - 1:tt, :]
    vec = vec_ref[...]
    e = e_ref[...]
    r, lw, kmod, v, kk, b, g, bonus = _rwkv_elem(*cur, *prev, vec, wup_ref, e)

    ri = lax.broadcasted_iota(jnp.int32, (tt, tt), 0)
    ci = lax.broadcasted_iota(jnp.int32, (tt, tt), 1)
    same = (ri // RWKV_CHUNK) == (ci // RWKV_CHUNK)
    tri = (same & (ci <= ri)).astype(BF16)
    blk = same.astype(BF16)
    parts = _split3(lw)
    cum = sum(_dot(tri, p) for p in parts)
    tot = sum(_dot(blk, p) for p in parts)
    e_out = jnp.exp(-cum)
    to_end = jnp.exp(tot - cum)
    outs = ((rt_ref, r * jnp.exp(cum)), (at_ref, -kk * jnp.exp(cum - lw)), (bt_ref, b * e_out),
            (kt_ref, kmod * e_out), (v_ref, v), (bh_ref, b * to_end), (kh_ref, kmod * to_end),
            (gt_ref, jnp.exp(tot)))
    for ref, val in outs:
        for h in range(N_HEADS):
            ref[h] = val[:, h * HEAD_DIM:(h + 1) * HEAD_DIM]
    return g, bonus


def _bmm(a, b, ca, cb, passes):
    dn = (((ca,), (cb,)), ((0,), (0,)))
    d = lambda x, y: lax.dot_general(x, y, dn, preferred_element_type=F32)
    if passes == 1:
        return d(a.astype(BF16), b.astype(BF16))
    ah, al = _split2(a)
    bh, bl = _split2(b)
    return d(ah, bh) + d(ah, bl) + d(al, bh)


def _rwkv_chunks(head_refs, s0_ref, o_ref, st_ref, s_scr, passes):
    rt_ref, at_ref, bt_ref, kt_ref, v_ref, bh_ref, kh_ref, gt_ref = head_refs
    c = pl.program_id(1)

    @pl.when(c == 0)
    def _():
        s_scr[...] = s0_ref[0]

    n = RWKV_CHUNK
    nc = rt_ref.shape[1] // n
    shp = (1, n, n)
    ti = lax.broadcasted_iota(jnp.int32, shp, 1)
    tj = lax.broadcasted_iota(jnp.int32, shp, 2)
    strict, incl = tj < ti, tj <= ti
    same = (ti // INV_BLOCK) == (tj // INV_BLOCK)
    nn = lambda a, b, p: _bmm(a, b, 2, 1, p)
    nt = lambda a, b, p: _bmm(a, b, 2, 2, p)
    tn = lambda a, b, p: _bmm(a, b, 1, 1, p)
    p_gen, p_chain = passes

    per_chunk = lambda ref: ref[...].reshape(N_HEADS * nc, n, HEAD_DIM)
    rt, at, bt, kt, v, bh, kh = (per_chunk(r) for r in (rt_ref, at_ref, bt_ref, kt_ref, v_ref, bh_ref, kh_ref))
    g_end = per_chunk(gt_ref)[:, 0:1, :]
    big = nt(jnp.concatenate([at, rt], axis=1), jnp.concatenate([bt, kt], axis=1), p_gen)
    a_ab = jnp.where(strict, big[:, :n, :n], 0.0)
    a_ak = jnp.where(strict, big[:, :n, n:], 0.0)
    a_rb = jnp.where(incl, big[:, n:, :n], 0.0)
    a_rk = jnp.where(incl, big[:, n:, n:], 0.0)
    av = nn(jnp.concatenate([a_ak, a_rk], axis=1), v, p_gen)
    nd = jnp.where(same, a_ab, 0.0)
    no = jnp.where(same, 0.0, a_ab)
    nd2 = nn(nd, nd, p_gen)
    nd4 = nn(nd2, nd2, p_gen)
    nd8 = nn(nd4, nd4, p_gen)
    y = jnp.concatenate([no, at, av[:, :n]], axis=2)
    for pw in (nd8, nd4, nd2, nd):
        y = y + nn(pw, y, p_gen)
    m = y[:, :, :n]
    z = y[:, :, n:]
    m2 = nn(m, m, p_gen)
    z = z + nn(m2, z, p_gen)
    z = z + nn(m, z, p_gen)
    qo = jnp.concatenate([rt, av[:, n:]], axis=2) + nn(a_rb, z, p_gen)
    pg = tn(z, bh, p_gen)
    gm = pg[:, n:] + tn(v, kh, p_gen)

    by_chunk = lambda x: x.reshape((N_HEADS, nc) + x.shape[1:])
    qo, pmat, gm, g_end = by_chunk(qo), by_chunk(pg[:, :n]), by_chunk(gm), by_chunk(g_end)
    s = s_scr[...]
    for i in range(nc):
        o_ref[:, i * n:(i + 1) * n, :] = nt(qo[:, i, :, :HEAD_DIM], s, p_chain) + qo[:, i, :, HEAD_DIM:]
        s = s * g_end[:, i] + nn(s, pmat[:, i], p_chain) + gm[:, i]
    s_scr[...] = s

    @pl.when(c == pl.num_programs(1) - 1)
    def _():
        st_ref[0] = s


def _rwkv_prompt_kernel(pr_ref, pk_ref, pv_ref, pl_ref, sp_ref, vec_ref, wup_ref, e_ref, s0_ref,
                        ya_ref, st_ref, carry, s_scr, o_scr, *head_scr, passes):
    g, bonus = _rwkv_prep(pr_ref, pk_ref, pv_ref, pl_ref, sp_ref, vec_ref, wup_ref, e_ref, head_scr, carry)
    _rwkv_chunks(head_scr, s0_ref, o_scr, st_ref, s_scr, passes)
    o = jnp.concatenate([o_scr[h] for h in range(N_HEADS)], axis=-1)
    e = e_ref[...]
    vec = vec_ref[...]
    mu = _dot_exact_rhs(o, e) * (1.0 / HEAD_DIM)
    d = o - mu
    var = _dot_exact_rhs(d * d, e) * (1.0 / HEAD_DIM)
    on = d * lax.rsqrt(var + GN_EPS) * vec[V_LNG:V_LNG + 1, :] + vec[V_LNB:V_LNB + 1, :]
    ya_ref[0] = ((on + bonus) * g).astype(ya_ref.dtype)


def _rwkv_prompt(proj, nb, t, shift_prev, s0, vec, wup, e, passes):
    tt = RWKV_CHUNKS_PER_STEP * RWKV_CHUNK
    nt_ = t // tt
    gw, nh, hd = GROUP_WIDTH, N_HEADS, HEAD_DIM
    pspec = lambda cb: pl.BlockSpec((tt, gw), lambda b, i: (b * nt_ + i, cb))
    const = lambda shape: pl.BlockSpec(shape, lambda b, i: (0,) * len(shape))
    sspec = pl.BlockSpec((1, nh, hd, hd), lambda b, i: (b, 0, 0, 0))
    head_major = pltpu.VMEM((nh, tt, hd), F32)
    return pl.pallas_call(
        functools.partial(_rwkv_prompt_kernel, passes=passes),
        grid=(nb, nt_),
        in_specs=[pspec(CB_R), pspec(CB_RK), pspec(CB_RV),
                  pl.BlockSpec((tt, LANES), lambda b, i: (b * nt_ + i, CB_L * (gw // LANES))),
                  pl.BlockSpec((1, 1, 3 * gw + LANES), lambda b, i: (b, 0, 0)),
                  const(vec.shape), const(wup.shape), const(e.shape), sspec],
        out_specs=[pl.BlockSpec((1, tt, gw), lambda b, i: (b, i, 0)), sspec],
        out_shape=[jax.ShapeDtypeStruct((nb, t, gw), BF16), jax.ShapeDtypeStruct((nb, nh, hd, hd), F32)],
        scratch_shapes=[pltpu.VMEM((1, 3 * gw + LANES), F32), pltpu.VMEM((nh, hd, hd), F32), head_major]
                       + [head_major] * 8,
        compiler_params=_cp("arbitrary", "arbitrary"),
        name="rwkv_prompt",
    )(proj, proj, proj, proj, shift_prev, vec, wup, e, s0)


def _alibi_slope(h):
    return 2.0 ** (-8.0 * (h + 1) / N_HEADS)


LOG2E = 1.4426950408889634
MASKED = -1e30


def _eye(n):
    return (lax.broadcasted_iota(jnp.int32, (n, n), 0) == lax.broadcasted_iota(jnp.int32, (n, n), 1)).astype(BF16)


def _moba_prompt_kernel(q_ref, k_ref, v_ref, selm_ref, slope_ref, o_ref, kh_scr, vt_scr, km_scr, selt_scr, m_scr,
                        acc_scr, *, n_blocks):
    qi = pl.program_id(1)
    blk, nh, hd = MOBA_BLOCK, N_HEADS, HEAD_DIM
    nt_dims = (((1,), (1,)), ((), ()))

    @pl.when(qi == 0)
    def _():
        k = k_ref[...]
        v = v_ref[...]
        one_row = (lax.broadcasted_iota(jnp.int32, (hd, blk), 0) == 0).astype(BF16)
        eye_hd = _eye(hd)
        for h in range(nh):
            kh_scr[h] = k[:, h * hd:(h + 1) * hd].astype(BF16)
            vt = lax.dot_general(eye_hd, v[:, h * hd:(h + 1) * hd].astype(BF16), nt_dims,
                                 preferred_element_type=F32).astype(BF16)
            for n in range(n_blocks):
                vt_scr[n, h] = jnp.concatenate([vt[:, n * blk:(n + 1) * blk], one_row], axis=0)
        rows = [jnp.mean(k[n * blk:(n + 1) * blk, :], axis=0, keepdims=True) for n in range(n_blocks)]
        rows += [jnp.zeros((1, k.shape[1]), F32)] * (selm_ref.shape[0] - n_blocks)
        kmean = jnp.concatenate(rows, axis=0)
        tn_dims = (((0,), (0,)), ((), ()))
        spread = sum(lax.dot_general(p, selm_ref[...], tn_dims, preferred_element_type=F32)
                     for p in _split3(kmean))
        rh = lax.broadcasted_iota(jnp.int32, spread.shape, 0) // hd
        ch = lax.broadcasted_iota(jnp.int32, spread.shape, 1) % nh
        for i, p in enumerate(_split3(jnp.where(rh == ch, spread, 0.0))):
            km_scr[i] = p

    own = qi
    q = q_ref[...]
    q1, q2, q3 = _split3(q * (hd ** -0.5))
    k1, k2, k3 = km_scr[0], km_scr[1], km_scr[2]
    gate = (_dot(q1, k1) + _dot(q1, k2) + _dot(q2, k1)) + (_dot(q1, k3) + _dot(q2, k2) + _dot(q3, k1))
    lane_blk = lax.broadcasted_iota(jnp.int32, gate.shape, 1) // nh
    valid = lane_blk < own
    gate = jnp.where(valid, gate, -jnp.inf)
    rank = jnp.zeros_like(gate)
    for d in range(1, n_blocks):
        earlier = pltpu.roll(gate, nh * d, axis=1)
        later = pltpu.roll(gate, LANES - nh * d, axis=1)
        rank = rank + jnp.where(earlier >= gate, 1.0, 0.0) + jnp.where(later > gate, 1.0, 0.0)
    sel = jnp.where(valid & (rank < min(MOBA_TOPK, n_blocks - 1)), 1.0, 0.0)
    selt_scr[...] = lax.dot_general(_eye(LANES), sel.astype(BF16), nt_dims, preferred_element_type=F32)

    qa = q * ((hd ** -0.5) * LOG2E)
    qh = jnp.stack([qa[:, h * hd:(h + 1) * hd].astype(BF16) for h in range(nh)])
    ahead_i = (lax.broadcasted_iota(jnp.int32, (blk, blk), 1) - lax.broadcasted_iota(jnp.int32, (blk, blk), 0))
    causal = ahead_i >= 0
    ahead = ahead_i.astype(F32)
    slope2 = slope_ref[...] * LOG2E
    bnt = (((2,), (2,)), ((0,), (0,)))
    bnn = (((2,), (1,)), ((0,), (0,)))

    def scores(first_key):
        k_blk = kh_scr[:, pl.ds(first_key, blk), :]
        s = lax.dot_general(k_blk, qh, bnt, preferred_element_type=F32)
        return [s[h] - (_alibi_slope(h) * LOG2E) * ahead for h in range(nh)]

    tm = jnp.stack([jnp.where(causal, sh, MASKED) for sh in scores(pl.multiple_of(own * blk, blk))])
    m = jnp.max(tm, axis=1, keepdims=True)
    p = jnp.exp2(tm - m)
    m_scr[...] = m
    acc_scr[...] = lax.dot_general(vt_scr[own], p.astype(BF16), bnn, preferred_element_type=F32)

    def past_block(n, carry):
        shift = slope2 * ((own - n) * blk).astype(F32)
        sh = scores(pl.multiple_of(n * blk, blk))
        tm = jnp.stack([jnp.where(selt_scr[pl.ds(n * nh + h, 1), :] > 0.5, sh[h], MASKED) for h in range(nh)])
        m_old = m_scr[...]
        m_new = jnp.maximum(m_old, jnp.max(tm, axis=1, keepdims=True) - shift)
        p = jnp.exp2(tm - (m_new + shift))
        m_scr[...] = m_new
        acc_scr[...] = (jnp.exp2(m_old - m_new) * acc_scr[...]
                        + lax.dot_general(vt_scr[n], p.astype(BF16), bnn, preferred_element_type=F32))
        return carry

    lax.fori_loop(0, own, past_block, 0)
    acc = acc_scr[...]
    out_t = (acc[:, 0:hd, :] / acc[:, hd:hd + 1, :]).astype(BF16)
    eyes = jnp.broadcast_to(_eye(hd)[None], (nh, hd, hd))
    out = lax.dot_general(out_t, eyes, (((1,), (1,)), ((0,), (0,))), preferred_element_type=F32)
    o_ref[...] = jnp.concatenate([out[h] for h in range(nh)], axis=-1).astype(o_ref.dtype)


def _moba_prompt(proj, nb, t):
    blk, gw, nh, hd = MOBA_BLOCK, GROUP_WIDTH, N_HEADS, HEAD_DIM
    n_blocks = t // blk
    assert n_blocks * nh <= LANES // 2
    selm = (jnp.arange(8)[:, None] == jnp.arange(LANES)[None, :] // nh).astype(BF16)
    slopes = jnp.broadcast_to(jnp.array([_alibi_slope(h) for h in range(nh)], F32)[:, None, None], (nh, 1, blk))
    return pl.pallas_call(
        functools.partial(_moba_prompt_kernel, n_blocks=n_blocks),
        grid=(nb, n_blocks),
        in_specs=[pl.BlockSpec((blk, gw), lambda b, i: (b * n_blocks + i, CB_Q)),
                  pl.BlockSpec((t, gw), lambda b, i: (b, CB_K)),
                  pl.BlockSpec((t, gw), lambda b, i: (b, CB_V)),
                  pl.BlockSpec((8, LANES), lambda b, i: (0, 0)),
                  pl.BlockSpec((nh, 1, blk), lambda b, i: (0, 0, 0))],
        out_specs=pl.BlockSpec((blk, gw), lambda b, i: (b * n_blocks + i, 0)),
        out_shape=jax.ShapeDtypeStruct((nb * t, gw), BF16),
        scratch_shapes=[pltpu.VMEM((nh, t, hd), BF16), pltpu.VMEM((n_blocks, nh, 2 * hd, blk), BF16),
                        pltpu.VMEM((3, gw, LANES), BF16), pltpu.VMEM((LANES, blk), F32),
                        pltpu.VMEM((nh, 1, blk), F32), pltpu.VMEM((nh, 2 * hd, blk), F32)],
        compiler_params=_cp("arbitrary", "arbitrary"),
        name="moba_prompt",
    )(proj, proj, proj, selm, slopes)


def _sgu_prompt_kernel(pu_ref, pv_ref, w_ref, bias_ref, ln_ref, o_ref):
    u = jax.nn.gelu(pu_ref[...])
    v = _layer_norm(jax.nn.gelu(pv_ref[...]), ln_ref[0:1, :], ln_ref[1:2, :]).astype(BF16)
    cl, hd = SG_CHUNK, HEAD_DIM
    ri = lax.broadcasted_iota(jnp.int32, (cl, cl), 0)
    ci = lax.broadcasted_iota(jnp.int32, (cl, cl), 1)
    ws = [jnp.where(ci <= ri, w_ref[h], 0.0).astype(BF16) for h in range(N_HEADS)]
    bias = bias_ref[...]
    for c in range(u.shape[0] // cl):
        vc = v[c * cl:(c + 1) * cl, :]
        mixed = jnp.concatenate([_dot(ws[h], vc[:, h * hd:(h + 1) * hd]) for h in range(N_HEADS)], axis=-1)
        o_ref[c * cl:(c + 1) * cl, :] = (u[c * cl:(c + 1) * cl, :] * (mixed + bias)).astype(o_ref.dtype)


def _sgu_prompt(proj, nrows, w, bias_exp, ln):
    tt, gw = 256, GROUP_WIDTH
    return pl.pallas_call(
        _sgu_prompt_kernel,
        grid=(nrows // tt,),
        in_specs=[pl.BlockSpec((tt, gw), lambda i: (i, CB_SU)),
                  pl.BlockSpec((tt, gw), lambda i: (i, CB_SV)),
                  pl.BlockSpec(w.shape, lambda i: (0, 0, 0)),
                  pl.BlockSpec(bias_exp.shape, lambda i: (0, 0)),
                  pl.BlockSpec(ln.shape, lambda i: (0, 0))],
        out_specs=pl.BlockSpec((tt, gw), lambda i: (i, 0)),
        out_shape=jax.ShapeDtypeStruct((nrows, gw), BF16),
        compiler_params=_cp("arbitrary"),
        name="sgu_prompt",
    )(proj, proj, w, bias_exp, ln)


def _conv_prompt_kernel(pa_ref, pg_ref, prev_ref, w_ref, vec_ref, o_ref, hist_ref, zf):
    ti = pl.program_id(1)
    tt = pa_ref.shape[0]
    halo = CONV_HALO

    @pl.when(ti == 0)
    def _():
        zf[0:halo, :] = prev_ref[0]

    z = pa_ref[...] * jax.nn.sigmoid(pg_ref[...])
    zf[halo:halo + tt, :] = z
    lead = halo - (CONV_WIDTH - 1)
    zall = zf[...]
    n_rows = zall.shape[0]
    sub = 8
    rotated = [zall if r == 0 else pltpu.roll(zall, n_rows - r, axis=0) for r in range(sub)]
    acc = jnp.zeros_like(z)
    for i in range(CONV_WIDTH):
        a, r = divmod(lead + i, sub)
        acc = acc + w_ref[i:i + 1, :] * rotated[r][a * sub:a * sub + tt, :]
    y = _layer_norm(acc + vec_ref[0:1, :], vec_ref[1:2, :], vec_ref[2:3, :])
    o_ref[...] = (y * jax.nn.sigmoid(y)).astype(o_ref.dtype)
    tail = zf[pl.ds(tt, halo), :]
    hist_ref[0] = tail
    zf[0:halo, :] = tail


def _conv_prompt(proj, nb, t, prev, w, vec):
    tt, gw = 256, GROUP_WIDTH
    nt_ = t // tt
    return pl.pallas_call(
        _conv_prompt_kernel,
        grid=(nb, nt_),
        in_specs=[pl.BlockSpec((tt, gw), lambda b, i: (b * nt_ + i, CB_CA)),
                  pl.BlockSpec((tt, gw), lambda b, i: (b * nt_ + i, CB_CG)),
                  pl.BlockSpec((1, CONV_HALO, gw), lambda b, i: (b, 0, 0)),
                  pl.BlockSpec(w.shape, lambda b, i: (0, 0)),
                  pl.BlockSpec(vec.shape, lambda b, i: (0, 0))],
        out_specs=[pl.BlockSpec((tt, gw), lambda b, i: (b * nt_ + i, 0)),
                   pl.BlockSpec((1, CONV_HALO, gw), lambda b, i: (b, 0, 0))],
        out_shape=[jax.ShapeDtypeStruct((nb * t, gw), BF16),
                   jax.ShapeDtypeStruct((nb, CONV_HALO, gw), F32)],
        scratch_shapes=[pltpu.VMEM((CONV_HALO + tt, gw), F32)],
        compiler_params=_cp("arbitrary", "arbitrary"),
        name="conv_prompt",
    )(proj, proj, prev, w, vec)


def _sample_elem_kernel(pr_ref, pk_ref, pv_ref, pl_ref, sp_ref, vec_ref, wup_ref, e_ref,
                        su_ref, sv_ref, sgp_ref, ca_ref, cg_ref, cprev_ref, cw_ref, cvec_ref,
                        r_ref, w_ref, k_ref, v_ref, kk_ref, b_ref, g_ref, bonus_ref,
                        yc_ref, sgv_ref, yd_ref, cnew_ref):
    gw = GROUP_WIDTH
    sp = sp_ref[...]
    r, lw, kmod, v, kk, b, g, bonus = _rwkv_elem(
        pr_ref[...], pk_ref[...], pv_ref[...], pl_ref[...],
        sp[:, 0:gw], sp[:, gw:2 * gw], sp[:, 2 * gw:3 * gw], sp[:, 3 * gw:3 * gw + LANES],
        vec_ref[...], wup_ref, e_ref[...], precise=True)
    for ref, val in ((r_ref, r), (w_ref, jnp.exp(lw)), (k_ref, kmod), (v_ref, v), (kk_ref, kk),
                     (b_ref, b), (g_ref, g), (bonus_ref, bonus)):
        ref[...] = val
    u = jax.nn.gelu(su_ref[...])
    sv = _layer_norm(jax.nn.gelu(sv_ref[...]), sgp_ref[0:1, :], sgp_ref[1:2, :])
    sgv_ref[...] = sv
    yc_ref[...] = (u * (sgp_ref[2:3, :] * sv + sgp_ref[3:4, :])).astype(yc_ref.dtype)
    z = ca_ref[...] * jax.nn.sigmoid(cg_ref[...])
    nprev = CONV_WIDTH - 1
    acc = cw_ref[nprev:nprev + 1, :] * z
    for i in range(nprev):
        acc = acc + cw_ref[i:i + 1, :] * cprev_ref[i]
        if i > 0:
            cnew_ref[i - 1] = cprev_ref[i]
    cnew_ref[nprev - 1] = z
    y = _layer_norm(acc + cvec_ref[0:1, :], cvec_ref[1:2, :], cvec_ref[2:3, :])
    yd_ref[...] = (y * jax.nn.sigmoid(y)).astype(yd_ref.dtype)


def _sample_wkv_kernel(r_ref, w_ref, k_ref, v_ref, kk_ref, b_ref, g_ref, bonus_ref, ln_ref, s_ref,
                       ya_ref, st_ref):
    hd = HEAD_DIM
    r, w, k, v, kk, b, g, bonus = (x[0] for x in (r_ref, w_ref, k_ref, v_ref, kk_ref, b_ref, g_ref, bonus_ref))
    s = s_ref[0]
    eye = (lax.broadcasted_iota(jnp.int32, (hd, hd), 0) == lax.broadcasted_iota(jnp.int32, (hd, hd), 1))
    eye = eye.astype(F32)[None]
    v_col = jnp.sum(eye * v, axis=2, keepdims=True)
    sa = jnp.sum(s * (-kk), axis=2, keepdims=True)
    s_new = s * w + sa * b + v_col * k
    st_ref[0] = s_new
    o_col = jnp.sum(s_new * r, axis=2, keepdims=True)
    o = jnp.sum(o_col * eye, axis=1, keepdims=True)
    mu = jnp.mean(o, axis=-1, keepdims=True)
    d = o - mu
    var = jnp.mean(d * d, axis=-1, keepdims=True)
    on = d * lax.rsqrt(var + GN_EPS) * ln_ref[0] + ln_ref[1]
    ya_ref[0] = (on + bonus) * g


def _sample_mixers(proj, shift_prev, s0, conv_prev_t, vec, wup, e, sgp, cw, cvec):
    nb = proj.shape[0]
    gw, nh, hd = GROUP_WIDTH, N_HEADS, HEAD_DIM
    col = lambda cb: pl.BlockSpec((nb, gw), lambda i: (0, cb))
    full = lambda x: pl.BlockSpec(x.shape, lambda i: (0,) * x.ndim)
    row_f32 = jax.ShapeDtypeStruct((nb, gw), F32)
    outs = pl.pallas_call(
        _sample_elem_kernel,
        grid=(1,),
        in_specs=[col(CB_R), col(CB_RK), col(CB_RV),
                  pl.BlockSpec((nb, LANES), lambda i: (0, CB_L * (gw // LANES))),
                  full(shift_prev), full(vec), full(wup), full(e),
                  col(CB_SU), col(CB_SV), full(sgp), col(CB_CA), col(CB_CG),
                  full(conv_prev_t), full(cw), full(cvec)],
        out_specs=[pl.BlockSpec((nb, gw), lambda i: (0, 0))] * 11 + [full(conv_prev_t)],
        out_shape=[row_f32] * 11 + [jax.ShapeDtypeStruct(conv_prev_t.shape, F32)],
        compiler_params=_cp("arbitrary"),
        name="sample_elem",
    )(proj, proj, proj, proj, shift_prev, vec, wup, e, proj, proj, sgp, proj, proj, conv_prev_t, cw, cvec)
    heads = [x.reshape(nb, nh, 1, hd) for x in outs[:8]]
    yc, sgv, yd, conv_new_t = outs[8:]
    ln = jnp.stack([vec[V_LNG].reshape(nh, 1, hd), vec[V_LNB].reshape(nh, 1, hd)])
    hspec = pl.BlockSpec((1, nh, 1, hd), lambda b: (b, 0, 0, 0))
    sspec = pl.BlockSpec((1, nh, hd, hd), lambda b: (b, 0, 0, 0))
    ya, s_new = pl.pallas_call(
        _sample_wkv_kernel,
        grid=(nb,),
        in_specs=[hspec] * 8 + [pl.BlockSpec(ln.shape, lambda b: (0, 0, 0, 0)), sspec],
        out_specs=[hspec, sspec],
        out_shape=[jax.ShapeDtypeStruct((nb, nh, 1, hd), F32), jax.ShapeDtypeStruct(s0.shape, F32)],
        compiler_params=_cp("arbitrary"),
        name="sample_wkv",
    )(*heads, ln, s0)
    return ya.reshape(nb, gw), s_new, yc, sgv, yd, conv_new_t


PAGES_PER_STEP = 32


def _moba_scores_kernel(pt_ref, q_ref, *refs):
    pages, o_ref = refs[:-1], refs[-1]
    ps = pages[0].shape[4]
    qb = jnp.broadcast_to(q_ref[0] * (HEAD_DIM ** -0.5), pages[0].shape[2:])
    for i, p_ref in enumerate(pages):
        rows = [jnp.sum(p_ref[0, 0, h] * qb[h], axis=0, keepdims=True) for h in range(N_HEADS)]
        o_ref[0, :, i * ps:(i + 1) * ps] = jnp.concatenate(rows, axis=0)


def _moba_select_kernel(s_ref, q_ref, kn_ref, bi_ref, slope_ref, p_ref, p0_ref, *, n_past_blocks):
    s = s_ref[...]
    length = s.shape[1]
    bi = bi_ref[...]
    s0 = jnp.sum(q_ref[...] * (HEAD_DIM ** -0.5) * kn_ref[...], axis=1, keepdims=True)
    gate = sum(_dot(p, bi) for p in _split3(s)) * (1.0 / MOBA_BLOCK)
    lane = lax.broadcasted_iota(jnp.int32, gate.shape, 1)
    gate = jnp.where(lane < n_past_blocks, gate, -jnp.inf)
    sel = jnp.zeros(gate.shape, jnp.bool_)
    for _ in range(min(MOBA_TOPK, n_past_blocks)):
        m = jnp.max(gate, axis=1, keepdims=True)
        idx = jnp.min(jnp.where(gate == m, lane, LANES), axis=1, keepdims=True)
        pick = lane == idx
        sel = sel | pick
        gate = jnp.where(pick, -jnp.inf, gate)
    picked = lax.dot_general(sel.astype(BF16), bi, (((1,), (1,)), ((), ())),
                             preferred_element_type=F32) > 0.5
    pos = lax.broadcasted_iota(jnp.int32, s.shape, 1)
    dist = (length - pos).astype(F32)
    logit = jnp.where(picked, s - slope_ref[:, 0:1] * dist, -jnp.inf)
    m = jnp.maximum(jnp.max(logit, axis=1, keepdims=True), s0)
    p = jnp.exp(logit - m)
    p0 = jnp.exp(s0 - m)
    inv = 1.0 / (jnp.sum(p, axis=1, keepdims=True) + p0)
    p_ref[...] = p * inv
    p0_ref[...] = jnp.broadcast_to(p0 * inv, p0_ref.shape)


def _moba_pv_kernel(pt_ref, p_ref, p0_ref, vn_ref, *refs, n_steps):
    pages, o_ref, acc = refs[:-2], refs[-2], refs[-1]
    j = pl.program_id(1)
    ps = pages[0].shape[4]

    @pl.when(j == 0)
    def _():
        acc[...] = jnp.zeros_like(acc)

    for h in range(N_HEADS):
        tot = acc[h]
        for i, v_ref in enumerate(pages):
            tot = tot + v_ref[0, 0, h] * p_ref[0, h:h + 1, i * ps:(i + 1) * ps]
        acc[h] = tot

    @pl.when(j == n_steps - 1)
    def _():
        for h in range(N_HEADS):
            o_ref[0, h] = (jnp.sum(acc[h], axis=1, keepdims=True)
                           + p0_ref[0, h:h + 1, 0:1] * vn_ref[0, h])


def _moba_sample(proj, cache_k, cache_v, layer, page_table):
    nb, n_pages = page_table.shape
    ps, gw, nh, hd = cache_k.shape[2], GROUP_WIDTH, N_HEADS, HEAD_DIM
    length = n_pages * ps
    assert length % MOBA_BLOCK == 0
    n_past_blocks = length // MOBA_BLOCK
    pp = min(PAGES_PER_STEP, n_pages)
    n_steps = n_pages // pp
    q, kn, vn = (proj[:, cb * gw:(cb + 1) * gw] for cb in (CB_Q, CB_K, CB_V))
    cache_kt = cache_k.transpose(0, 1, 3, 4, 2)
    cache_vt = cache_v.transpose(0, 1, 3, 4, 2)
    col4 = lambda x: x.reshape(nb, nh, hd, 1)
    col_spec = pl.BlockSpec((1, nh, hd, 1), lambda b, j, pt: (b, 0, 0, 0))

    def page_spec(i):
        return pl.BlockSpec((1, 1, nh, hd, ps), lambda b, j, pt: (layer, pt[b, j * pp + i], 0, 0, 0))

    scores = pl.pallas_call(
        _moba_scores_kernel,
        grid_spec=pltpu.PrefetchScalarGridSpec(
            num_scalar_prefetch=1,
            grid=(nb, n_steps),
            in_specs=[col_spec] + [page_spec(i) for i in range(pp)],
            out_specs=pl.BlockSpec((1, nh, pp * ps), lambda b, j, pt: (b, 0, j)),
        ),
        out_shape=jax.ShapeDtypeStruct((nb, nh, length), F32),
        compiler_params=_cp("arbitrary", "arbitrary"),
        name="moba_scores",
    )(page_table, col4(q), *([cache_kt] * pp))

    bi = (jnp.arange(length)[:, None] // MOBA_BLOCK == jnp.arange(LANES)[None, :]).astype(BF16)
    rows = nb * nh
    tr = min(rows, 128)
    slopes = jnp.broadcast_to(
        jnp.tile(jnp.array([_alibi_slope(h) for h in range(nh)], F32), nb)[:, None], (rows, LANES))
    p, p0 = pl.pallas_call(
        functools.partial(_moba_select_kernel, n_past_blocks=n_past_blocks),
        grid=(rows // tr,),
        in_specs=[pl.BlockSpec((tr, length), lambda i: (i, 0)),
                  pl.BlockSpec((tr, hd), lambda i: (i, 0)),
                  pl.BlockSpec((tr, hd), lambda i: (i, 0)),
                  pl.BlockSpec((length, LANES), lambda i: (0, 0)),
                  pl.BlockSpec((tr, LANES), lambda i: (i, 0))],
        out_specs=[pl.BlockSpec((tr, length), lambda i: (i, 0)),
                   pl.BlockSpec((tr, LANES), lambda i: (i, 0))],
        out_shape=[jax.ShapeDtypeStruct((rows, length), F32), jax.ShapeDtypeStruct((rows, LANES), F32)],
        compiler_params=_cp("arbitrary"),
        name="moba_select",
    )(scores.reshape(rows, length), q.reshape(rows, hd), kn.reshape(rows, hd), bi, slopes)

    out = pl.pallas_call(
        functools.partial(_moba_pv_kernel, n_steps=n_steps),
        grid_spec=pltpu.PrefetchScalarGridSpec(
            num_scalar_prefetch=1,
            grid=(nb, n_steps),
            in_specs=[pl.BlockSpec((1, nh, pp * ps), lambda b, j, pt: (b, 0, j)),
                      pl.BlockSpec((1, nh, LANES), lambda b, j, pt: (b, 0, 0)),
                      col_spec]
                     + [page_spec(i) for i in range(pp)],
            out_specs=col_spec,
            scratch_shapes=[pltpu.VMEM((nh, hd, ps), F32)],
        ),
        out_shape=jax.ShapeDtypeStruct((nb, nh, hd, 1), F32),
        compiler_params=_cp("arbitrary", "arbitrary"),
        name="moba_pv",
    )(page_table, p.reshape(nb, nh, length), p0.reshape(nb, nh, LANES), col4(vn), *([cache_vt] * pp))
    return out.reshape(nb, gw)


RWKV_PASSES = (1, 1)


def _shift_layout(shift):
    n = shift.shape[0]
    pad = jnp.zeros((n, LANES - 3 * TM_LORA), F32)
    return jnp.concatenate([shift, pad], axis=1)


def _layer_params(l, w_in, w_out, w_ffn_in, w_ffn_out, tm_mu, tm_w0, tm_w_up, tm_a0, tm_a_up, tm_g_up,
                  tm_k_k, tm_k_a, tm_r_k, tm_ln_g, tm_ln_b, sg_ws, sg_bs, sg_ln_g, sg_ln_b,
                  cv_w, cv_b, cv_ln_g, cv_ln_b):
    gw, lr, hd = GROUP_WIDTH, TM_LORA, HEAD_DIM
    wt = w_in[l].T
    w_in_t = jnp.concatenate([wt[A_PROJ:], wt[:A_PROJ], jnp.zeros((IN_PAD - IN_WIDTH, wt.shape[1]), F32)], axis=0)
    mu = tm_mu[l]
    mu_l = jnp.concatenate([mu[3 * gw:], jnp.zeros((gw - 3 * lr,), F32)])
    vec = jnp.stack([mu[0:gw], mu[gw:2 * gw], mu[2 * gw:3 * gw], tm_w0[l], tm_a0[l], tm_k_k[l], tm_k_a[l],
                     tm_r_k[l].reshape(gw), tm_ln_g[l], tm_ln_b[l], mu_l]
                    + [jnp.zeros((gw,), F32)] * 5)
    wup = jnp.zeros((3, LANES, gw), F32)
    wup = wup.at[0, 0:lr].set(tm_w_up[l]).at[1, lr:2 * lr].set(tm_a_up[l]).at[2, 2 * lr:3 * lr].set(tm_g_up[l])
    sg_bias = jnp.repeat(sg_bs[l].T, hd, axis=1)
    sg_ln = jnp.stack([sg_ln_g[l], sg_ln_b[l]] + [jnp.zeros((gw,), F32)] * 6)
    sgp = jnp.stack([sg_ln_g[l], sg_ln_b[l], jnp.repeat(sg_ws[l][:, 0, 0], hd), jnp.repeat(sg_bs[l][:, 0], hd)]
                    + [jnp.zeros((gw,), F32)] * 4)
    cw = jnp.concatenate([cv_w[l], jnp.zeros((1, gw), F32)], axis=0)
    cvec = jnp.stack([cv_b[l], cv_ln_g[l], cv_ln_b[l]] + [jnp.zeros((gw,), F32)] * 5)
    return dict(vec=vec, wup=wup, sg_w=sg_ws[l], sg_bias=sg_bias, sg_ln=sg_ln, sgp=sgp, cw=cw, cvec=cvec,
                w_in_t=w_in_t[None], dense_f32=(w_out, w_ffn_in, w_ffn_out))


def _prompt_dense_tail(x, ys, mod, p, g_ffn, layer):
    ng, t, d = x.shape
    m = ng * t
    w_out, w_ffn_in, w_ffn_out = p['dense_f32']
    x1, h2 = _mix_out(ys, w_out[layer].astype(BF16), x.reshape(m, d), mod, g_ffn, 512, t)
    hid = _mm_swiglu(h2, w_ffn_in, layer, 1024, 512)
    x2 = _mm_rows([hid], w_ffn_out[layer].astype(BF16), 512, 512, res=x1, gate=mod, gate_col=5, rows_per_group=t)
    return x2.reshape(ng, t, d)


def _prompt_layer(x, mod, p, g_mix, g_ffn, e, layer):
    nb, t, d = x.shape
    m = nb * t
    gw, nh, hd = GROUP_WIDTH, N_HEADS, HEAD_DIM
    h = _norm_mod(x, g_mix, mod, 0, 1, 512).reshape(m, d)
    proj = _mm([h], p['w_in_t'], 0, 1024, 512, w_t=True)
    shift0 = jnp.zeros((nb, 1, 3 * gw + LANES), F32)
    s0 = jnp.zeros((nb, nh, hd, hd), F32)
    ya, wkv = _rwkv_prompt(proj, nb, t, shift0, s0, p['vec'], p['wup'], e, RWKV_PASSES)
    yb = _moba_prompt(proj, nb, t)
    yc = _sgu_prompt(proj, m, p['sg_w'], p['sg_bias'], p['sg_ln'])
    yd, hist = _conv_prompt(proj, nb, t, jnp.zeros((nb, CONV_HALO, gw), F32), p['cw'], p['cvec'])
    x2 = _prompt_dense_tail(x, [ya.reshape(m, gw), yb, yc, yd], mod, p, g_ffn, layer)
    proj3 = proj.reshape(nb, t, IN_PAD)
    last = proj3[:, t - 1]
    shift = jnp.concatenate([last[:, CB_R * gw:CB_L * gw], last[:, CB_L * gw:CB_L * gw + 3 * TM_LORA]], axis=1)
    new_k = proj3[:, :, CB_K * gw:(CB_K + 1) * gw].reshape(nb, t, nh, hd)
    new_v = proj3[:, :, CB_V * gw:(CB_V + 1) * gw].reshape(nb, t, nh, hd)
    return x2, (shift, wkv, new_k, new_v, hist[:, CONV_HALO - (CONV_WIDTH - 1):])


def _sample_layer(x, mod, p, g_mix, g_ffn, e, layer, shift_prev, wkv_prev, conv_prev, cache_k, cache_v,
                  page_table):
    nb, t, d = x.shape
    gw, nh, hd = GROUP_WIDTH, N_HEADS, HEAD_DIM
    xg = x.reshape(1, nb, d)
    h = _norm_mod(xg, g_mix, mod, 0, 1, nb, dtype=F32).reshape(nb, d)
    proj = _mm_precise([h], p['w_in_t'], 0, 512, mode="plain_t")
    ya, wkv, yc, sgv, yd, conv_new_t = _sample_mixers(
        proj, _shift_layout(shift_prev), wkv_prev, conv_prev.transpose(1, 0, 2),
        p['vec'], p['wup'], e, p['sgp'], p['cw'], p['cvec'])
    yb = _moba_sample(proj, cache_k, cache_v, layer, page_table)
    w_out, w_ffn_in, w_ffn_out = p['dense_f32']
    x1 = _mm_precise([ya, yb, yc, yd], w_out, layer, 512, mode="gated", res=x.reshape(nb, d), gate=mod, gate_col=2)
    h2 = _norm_mod(x1.reshape(1, nb, d), g_ffn, mod, 3, 4, nb, dtype=F32).reshape(nb, d)
    hid = _mm_precise([h2], w_ffn_in, layer, 512, mode="swiglu")
    x2 = _mm_precise([hid], w_ffn_out, layer, 256, mode="gated", res=x1, gate=mod, gate_col=5)
    shift = jnp.concatenate([proj[:, CB_R * gw:CB_L * gw], proj[:, CB_L * gw:CB_L * gw + 3 * TM_LORA]], axis=1)
    new_k = proj[:, CB_K * gw:(CB_K + 1) * gw].reshape(nb, 1, nh, hd)
    new_v = proj[:, CB_V * gw:(CB_V + 1) * gw].reshape(nb, 1, nh, hd)
    return x2.reshape(nb, t, d), (shift, wkv, new_k, new_v, sgv.reshape(nb, 1, gw), conv_new_t.transpose(1, 0, 2))


def kernel(x_prompt, x_sample, c_prompt, c_sample, cache_k_b, cache_v_b, state_a_wkv, state_a_shift, state_d_conv, page_table, w_ada, b_ada, g_norm_mix, w_in, w_out, tm_mu, tm_w0, tm_w_up, tm_a0, tm_a_up, tm_g_up, tm_k_k, tm_k_a, tm_r_k, tm_ln_g, tm_ln_b, sg_ws, sg_bs, sg_ln_g, sg_ln_b, cv_w, cv_b, cv_ln_g, cv_ln_b, g_norm_ffn, w_ffn_in, w_ffn_out, g_norm_final):
    depth = w_in.shape[0]
    bp, ns = x_prompt.shape[0], x_sample.shape[0]
    gw = GROUP_WIDTH
    rows = bp + ns
    rows_pad = -(-rows // 8) * 8
    c_all = jnp.concatenate([c_prompt, c_sample, jnp.zeros((rows_pad - rows, c_prompt.shape[1]), F32)], axis=0)
    mod = _ada(c_all, w_ada, b_ada)
    e = _head_ones()
    xp, xs = x_prompt, x_sample
    st_p, st_s = [], []
    for l in range(depth):
        p = _layer_params(l, w_in, w_out, w_ffn_in, w_ffn_out, tm_mu, tm_w0, tm_w_up, tm_a0, tm_a_up, tm_g_up,
                          tm_k_k, tm_k_a, tm_r_k, tm_ln_g, tm_ln_b, sg_ws, sg_bs, sg_ln_g, sg_ln_b,
                          cv_w, cv_b, cv_ln_g, cv_ln_b)
        mod_p = mod[l, :bp][:, None, :]
        mod_s = mod[l, bp:rows][None]
        xp, sp = _prompt_layer(xp, mod_p, p, g_norm_mix[l], g_norm_ffn[l], e, l)
        xs, ss = _sample_layer(xs, mod_s, p, g_norm_mix[l], g_norm_ffn[l], e, l, state_a_shift[l],
                               state_a_wkv[l], state_d_conv[l], cache_k_b, cache_v_b, page_table)
        st_p.append(sp)
        st_s.append(ss)
    y_prompt = _norm(xp, g_norm_final, 512)
    y_sample = _norm(xs.reshape(1, ns, -1), g_norm_final, ns).reshape(xs.shape)
    stack = lambda sts, i: jnp.stack([s[i] for s in sts])
    return (y_prompt, y_sample, stack(st_p, 2), stack(st_p, 3), stack(st_s, 2), stack(st_s, 3),
            stack(st_p, 1), stack(st_s, 1), stack(st_p, 0), stack(st_s, 0), stack(st_p, 4), stack(st_s, 5),
            stack(st_s, 4))
```

```python
import functools

import jax
import jax.numpy as jnp
from jax import lax
from jax.experimental import pallas as pl
from jax.experimental.pallas import tpu as pltpu

F32, BF16 = jnp.float32, jnp.bfloat16

D_MODEL = 2048
GROUP_WIDTH = 512
N_HEADS = 8
HEAD_DIM = 64
TM_LORA = 32
A_PROJ = 3 * GROUP_WIDTH + 3 * TM_LORA
IN_WIDTH = A_PROJ + 7 * GROUP_WIDTH
IN_PAD = 11 * GROUP_WIDTH
MOBA_BLOCK = 256
MOBA_TOPK = 3
SG_CHUNK = 128
CONV_WIDTH = 31
CONV_HALO = 32
RMS_EPS = 1e-6
LN_EPS = 1e-5
GN_EPS = 64e-5
RWKV_CHUNK = 64
RWKV_CHUNKS_PER_STEP = 4
INV_BLOCK = 16
LANES = 128

CB_Q, CB_K, CB_V, CB_SU, CB_SV, CB_CA, CB_CG, CB_R, CB_RK, CB_RV, CB_L = range(11)
(V_MU_R, V_MU_K, V_MU_V, V_W0, V_A0, V_KK, V_KA, V_RK, V_LNG, V_LNB, V_MU_L) = range(11)

VMEM_LIMIT = 56 * 1024 * 1024


def _cp(*sem):
    return pltpu.CompilerParams(dimension_semantics=sem, vmem_limit_bytes=VMEM_LIMIT)


def _dot(a, b):
    return jnp.dot(a, b, preferred_element_type=F32)


def _dot_nt(a, b):
    return lax.dot_general(a, b, (((1,), (1,)), ((), ())), preferred_element_type=F32)


def _split2(x):
    hi = x.astype(BF16)
    return hi, (x - hi.astype(F32)).astype(BF16)


def _split3(x):
    h1 = x.astype(BF16)
    r1 = x - h1.astype(F32)
    h2 = r1.astype(BF16)
    return h1, h2, (r1 - h2.astype(F32)).astype(BF16)


def _dot_exact_rhs(x, e):
    hi, lo = _split2(x)
    return _dot(hi, e) + _dot(lo, e)


def _head_ones():
    r = lax.broadcasted_iota(jnp.int32, (GROUP_WIDTH, GROUP_WIDTH), 0) // HEAD_DIM
    c = lax.broadcasted_iota(jnp.int32, (GROUP_WIDTH, GROUP_WIDTH), 1) // HEAD_DIM
    return (r == c).astype(BF16)


def _layer_norm(x, g, b):
    mu = jnp.mean(x, axis=-1, keepdims=True)
    d = x - mu
    var = jnp.mean(d * d, axis=-1, keepdims=True)
    return d * lax.rsqrt(var + LN_EPS) * g + b


def _softplus(x):
    return jnp.maximum(x, 0.0) + jnp.log1p(jnp.exp(-jnp.abs(x)))


def _dot3(x, w, dot=_dot):
    rows = x.shape[0]
    x_hi = x.astype(BF16).astype(F32)
    xs = jnp.concatenate([x_hi, x - x_hi], axis=0).astype(BF16)
    wh, wl = _split2(w)
    r = dot(xs, wh)
    return r[:rows] + r[rows:] + dot(x_hi.astype(BF16), wl)


def _ada_kernel(c_ref, w_ref, b_ref, o_ref):
    c = c_ref[...]
    o_ref[0] = _dot3(c * jax.nn.sigmoid(c), w_ref[0]) + b_ref[0]


def _ada(c_all, w_ada, b_ada):
    depth, d, n = w_ada.shape
    r = c_all.shape[0]
    tn = 512
    return pl.pallas_call(
        _ada_kernel,
        grid=(depth, n // tn),
        in_specs=[pl.BlockSpec((r, d), lambda l, j: (0, 0)),
                  pl.BlockSpec((1, d, tn), lambda l, j: (l, 0, j)),
                  pl.BlockSpec((1, 1, tn), lambda l, j: (l, 0, j))],
        out_specs=pl.BlockSpec((1, r, tn), lambda l, j: (l, 0, j)),
        out_shape=jax.ShapeDtypeStruct((depth, r, n), F32),
        compiler_params=_cp("arbitrary", "arbitrary"),
        name="ada",
    )(c_all, w_ada, b_ada.reshape(depth, 1, n))


def _norm_mod_kernel(x_ref, g_ref, sc_ref, sh_ref, o_ref):
    x = x_ref[0]
    h = x * lax.rsqrt(jnp.mean(x * x, axis=-1, keepdims=True) + RMS_EPS) * g_ref[...]
    o_ref[0] = (h * (1.0 + sc_ref[0]) + sh_ref[0]).astype(o_ref.dtype)


def _norm_mod(x, g, mod, sh_col, sc_col, tt, dtype=BF16):
    ng, t, d = x.shape
    r = mod.shape[1]
    return pl.pallas_call(
        _norm_mod_kernel,
        grid=(ng, t // tt),
        in_specs=[pl.BlockSpec((1, tt, d), lambda b, i: (b, i, 0)),
                  pl.BlockSpec((1, d), lambda b, i: (0, 0)),
                  pl.BlockSpec((1, r, d), lambda b, i: (b, 0, sc_col)),
                  pl.BlockSpec((1, r, d), lambda b, i: (b, 0, sh_col))],
        out_specs=pl.BlockSpec((1, tt, d), lambda b, i: (b, i, 0)),
        out_shape=jax.ShapeDtypeStruct((ng, t, d), dtype),
        compiler_params=_cp("arbitrary", "arbitrary"),
        name="norm_mod",
    )(x, g.reshape(1, d), mod, mod)


def _norm_kernel(x_ref, g_ref, o_ref):
    x = x_ref[0]
    o_ref[0] = x * lax.rsqrt(jnp.mean(x * x, axis=-1, keepdims=True) + RMS_EPS) * g_ref[...]


def _norm(x, g, tt):
    ng, t, d = x.shape
    return pl.pallas_call(
        _norm_kernel,
        grid=(ng, t // tt),
        in_specs=[pl.BlockSpec((1, tt, d), lambda b, i: (b, i, 0)),
                  pl.BlockSpec((1, d), lambda b, i: (0, 0))],
        out_specs=pl.BlockSpec((1, tt, d), lambda b, i: (b, i, 0)),
        out_shape=jax.ShapeDtypeStruct((ng, t, d), F32),
        compiler_params=_cp("arbitrary", "arbitrary"),
        name="norm_final",
    )(x, g.reshape(1, d))


def _mm_kernel(*refs, n_x, gated, w_t):
    xs, w_ref = refs[:n_x], refs[n_x]
    o_ref, wb = refs[-2], refs[-1]

    @pl.when(pl.program_id(1) == 0)
    def _():
        wb[...] = w_ref[0].astype(BF16)

    acc = None
    k0 = 0
    for x_ref in xs:
        k = x_ref.shape[1]
        part = _dot_nt(x_ref[...], wb[:, k0:k0 + k]) if w_t else _dot(x_ref[...], wb[k0:k0 + k, :])
        acc = part if acc is None else acc + part
        k0 += k
    if gated:
        res_ref, gt_ref = refs[n_x + 1], refs[n_x + 2]
        o_ref[...] = res_ref[...] + gt_ref[0] * acc
    else:
        o_ref[...] = acc


def _mm(xs, w, layer, tm, tn, res=None, gate=None, gate_col=0, rows_per_group=None, w_t=False):
    m = xs[0].shape[0]
    kdim, n = (w.shape[2], w.shape[1]) if w_t else (w.shape[1], w.shape[2])
    gated = res is not None
    in_specs = [pl.BlockSpec((tm, x.shape[1]), lambda j, i: (i, 0)) for x in xs]
    in_specs.append(pl.BlockSpec((1, tn, kdim), lambda j, i: (layer, j, 0)) if w_t
                    else pl.BlockSpec((1, kdim, tn), lambda j, i: (layer, 0, j)))
    args = list(xs) + [w]
    if gated:
        r = gate.shape[1]
        tiles_per_group = rows_per_group // tm
        nj = n // tn
        in_specs.append(pl.BlockSpec((tm, tn), lambda j, i: (i, j)))
        in_specs.append(pl.BlockSpec((1, r, tn), lambda j, i: (i // tiles_per_group, 0, gate_col * nj + j)))
        args += [res, gate]
    return pl.pallas_call(
        functools.partial(_mm_kernel, n_x=len(xs), gated=gated, w_t=w_t),
        grid=(n // tn, m // tm),
        in_specs=in_specs,
        out_specs=pl.BlockSpec((tm, tn), lambda j, i: (i, j)),
        out_shape=jax.ShapeDtypeStruct((m, n), F32),
        scratch_shapes=[pltpu.VMEM((tn, kdim) if w_t else (kdim, tn), BF16)],
        compiler_params=_cp("arbitrary", "arbitrary"),
        name="mm_gated" if gated else "mm",
    )(*args)


def _mm_rows_kernel(*refs, n_x, gated):
    xs, w_ref = refs[:n_x], refs[n_x]
    o_ref = refs[-1]
    acc = None
    k0 = 0
    for x_ref in xs:
        k = x_ref.shape[1]
        part = _dot(x_ref[...], w_ref[k0:k0 + k, :])
        acc = part if acc is None else acc + part
        k0 += k
    if gated:
        res_ref, gt_ref = refs[n_x + 1], refs[n_x + 2]
        o_ref[...] = res_ref[...] + gt_ref[0] * acc
    else:
        o_ref[...] = acc


def _mm_rows(xs, w, tm, tn, res=None, gate=None, gate_col=0, rows_per_group=None):
    m = xs[0].shape[0]
    n = w.shape[1]
    gated = res is not None
    in_specs = [pl.BlockSpec((tm, x.shape[1]), lambda i, j: (i, 0)) for x in xs]
    in_specs.append(pl.BlockSpec((w.shape[0], tn), lambda i, j: (0, j)))
    args = list(xs) + [w]
    if gated:
        r = gate.shape[1]
        tiles_per_group = rows_per_group // tm
        nj = n // tn
        in_specs.append(pl.BlockSpec((tm, tn), lambda i, j: (i, j)))
        in_specs.append(pl.BlockSpec((1, r, tn), lambda i, j: (i // tiles_per_group, 0, gate_col * nj + j)))
        args += [res, gate]
    return pl.pallas_call(
        functools.partial(_mm_rows_kernel, n_x=len(xs), gated=gated),
        grid=(m // tm, n // tn),
        in_specs=in_specs,
        out_specs=pl.BlockSpec((tm, tn), lambda i, j: (i, j)),
        out_shape=jax.ShapeDtypeStruct((m, n), F32),
        compiler_params=_cp("arbitrary", "arbitrary"),
        name="mm_rows_gated" if gated else "mm_rows",
    )(*args)


def _mix_out_kernel(*refs, n_x):
    xs, (w_ref, res_ref, gt_ref, g_ref, sc_ref, sh_ref, x1_ref, h_ref) = refs[:n_x], refs[n_x:]
    acc = None
    k0 = 0
    for x_ref in xs:
        k = x_ref.shape[1]
        part = _dot(x_ref[...], w_ref[k0:k0 + k, :])
        acc = part if acc is None else acc + part
        k0 += k
    x1 = res_ref[...] + gt_ref[0] * acc
    x1_ref[...] = x1
    h = x1 * lax.rsqrt(jnp.mean(x1 * x1, axis=-1, keepdims=True) + RMS_EPS) * g_ref[...]
    h_ref[...] = (h * (1.0 + sc_ref[0]) + sh_ref[0]).astype(h_ref.dtype)


def _mix_out(xs, w, res, mod, g, tm, rows_per_group):
    m, d = res.shape
    tiles_per_group = rows_per_group // tm
    row = lambda i: (i, 0)
    modspec = lambda col: pl.BlockSpec((1, 1, d), lambda i: (i // tiles_per_group, 0, col))
    return pl.pallas_call(
        functools.partial(_mix_out_kernel, n_x=len(xs)),
        grid=(m // tm,),
        in_specs=[pl.BlockSpec((tm, x.shape[1]), row) for x in xs]
                 + [pl.BlockSpec(w.shape, lambda i: (0, 0)), pl.BlockSpec((tm, d), row), modspec(2),
                    pl.BlockSpec((1, d), lambda i: (0, 0)), modspec(4), modspec(3)],
        out_specs=[pl.BlockSpec((tm, d), row), pl.BlockSpec((tm, d), row)],
        out_shape=[jax.ShapeDtypeStruct((m, d), F32), jax.ShapeDtypeStruct((m, d), BF16)],
        compiler_params=_cp("arbitrary"),
        name="mix_out",
    )(*xs, w, res, mod, g.reshape(1, d), mod, mod)


def _mm_precise_kernel(*refs, n_x, mode):
    xs = refs[:n_x]
    o_ref = refs[-1]
    if mode == "swiglu":
        x = xs[0][...]
        g = _dot3(x, refs[n_x][0])
        o_ref[...] = g * jax.nn.sigmoid(g) * _dot3(x, refs[n_x + 1][0])
        return
    w_ref = refs[n_x]
    if mode == "plain_t":
        o_ref[...] = _dot3(xs[0][...], w_ref[0], _dot_nt)
        return
    acc = None
    k0 = 0
    for x_ref in xs:
        k = x_ref.shape[1]
        part = _dot3(x_ref[...], w_ref[0, k0:k0 + k, :])
        acc = part if acc is None else acc + part
        k0 += k
    if mode == "gated":
        res_ref, gt_ref = refs[n_x + 1], refs[n_x + 2]
        o_ref[...] = res_ref[...] + gt_ref[0] * acc
    else:
        o_ref[...] = acc


def _mm_precise(xs, w, layer, tn, mode="plain", res=None, gate=None, gate_col=0):
    m = xs[0].shape[0]
    kdim = w.shape[2] if mode == "plain_t" else w.shape[1]
    n = {"swiglu": w.shape[2] // 2, "plain_t": w.shape[1]}.get(mode, w.shape[2])
    nj = n // tn
    in_specs = [pl.BlockSpec((m, x.shape[1]), lambda j: (0, 0)) for x in xs]
    in_specs.append(pl.BlockSpec((1, tn, kdim), lambda j: (layer, j, 0)) if mode == "plain_t"
                    else pl.BlockSpec((1, kdim, tn), lambda j: (layer, 0, j)))
    args = list(xs) + [w]
    if mode == "swiglu":
        in_specs.append(pl.BlockSpec((1, kdim, tn), lambda j: (layer, 0, nj + j)))
        args.append(w)
    if mode == "gated":
        in_specs.append(pl.BlockSpec((m, tn), lambda j: (0, j)))
        in_specs.append(pl.BlockSpec((1, m, tn), lambda j: (0, 0, gate_col * nj + j)))
        args += [res, gate]
    return pl.pallas_call(
        functools.partial(_mm_precise_kernel, n_x=len(xs), mode=mode),
        grid=(nj,),
        in_specs=in_specs,
        out_specs=pl.BlockSpec((m, tn), lambda j: (0, j)),
        out_shape=jax.ShapeDtypeStruct((m, n), F32),
        compiler_params=_cp("arbitrary"),
        name="mm_precise_" + mode,
    )(*args)


def _swiglu_kernel(x_ref, wg_ref, wu_ref, o_ref, wgb, wub):
    @pl.when(pl.program_id(1) == 0)
    def _():
        wgb[...] = wg_ref[0].astype(BF16)
        wub[...] = wu_ref[0].astype(BF16)

    x = x_ref[...]
    g = _dot(x, wgb[...])
    u = _dot(x, wub[...])
    o_ref[...] = (g * jax.nn.sigmoid(g) * u).astype(o_ref.dtype)


def _mm_swiglu(x, w, layer, tm, tn):
    m, k = x.shape
    nh = w.shape[2] // 2
    nj = nh // tn
    return pl.pallas_call(
        _swiglu_kernel,
        grid=(nj, m // tm),
        in_specs=[pl.BlockSpec((tm, k), lambda j, i: (i, 0)),
                  pl.BlockSpec((1, k, tn), lambda j, i: (layer, 0, j)),
                  pl.BlockSpec((1, k, tn), lambda j, i: (layer, 0, nj + j))],
        out_specs=pl.BlockSpec((tm, tn), lambda j, i: (i, j)),
        out_shape=jax.ShapeDtypeStruct((m, nh), BF16),
        scratch_shapes=[pltpu.VMEM((k, tn), BF16), pltpu.VMEM((k, tn), BF16)],
        compiler_params=_cp("arbitrary", "arbitrary"),
        name="mm_swiglu",
    )(x, w, w)


def _rwkv_elem(pr, pk, pv, plo, qr, qk, qv, qlo, vec, wup_ref, e, precise=False):
    row = lambda i: vec[i:i + 1, :]
    xr = pr + (qr - pr) * row(V_MU_R)
    xk = pk + (qk - pk) * row(V_MU_K)
    xv = pv + (qv - pv) * row(V_MU_V)
    xl = plo + (qlo - plo) * vec[V_MU_L:V_MU_L + 1, 0:LANES]
    up = _dot3 if precise else (lambda x, w: _dot(x.astype(BF16), w.astype(BF16)))
    lw = up(jnp.tanh(xl), wup_ref[0])
    la = up(xl, wup_ref[1])
    g = up(jax.nn.sigmoid(xl), wup_ref[2])
    w_log = -_softplus(-(row(V_W0) + lw)) - 0.5
    log_decay = -jnp.exp(w_log)
    a = jax.nn.sigmoid(row(V_A0) + la)
    kk = xk * row(V_KK)
    kk = kk / jnp.maximum(jnp.sqrt(_dot_exact_rhs(kk * kk, e)), 1e-12)
    kmod = xk * (1.0 + (a - 1.0) * row(V_KA))
    bonus = _dot_exact_rhs(xr * kmod * row(V_RK), e) * xv
    return xr, log_decay, kmod, xv, kk, kk * a, g, bonus


def _rwkv_prep(pr_ref, pk_ref, pv_ref, pl_ref, sp_ref, vec_ref, wup_ref, e_ref, head_refs, carry):
    rt_ref, at_ref, bt_ref, kt_ref, v_ref, bh_ref, kh_ref, gt_ref = head_refs
    ti = pl.program_id(1)

    @pl.when(ti == 0)
    def _():
        carry[...] = sp_ref[0]

    tt = pr_ref.shape[0]
    first = lax.broadcasted_iota(jnp.int32, (tt, 1), 0) == 0
    gw = GROUP_WIDTH
    cur = (pr_ref[...], pk_ref[...], pv_ref[...], pl_ref[...])
    offs = (0, gw, 2 * gw, 3 * gw)
    prev = []
    for p, o in zip(cur, offs):
        w = p.shape[1]
        prev.append(jnp.where(first, carry[:, o:o + w], pltpu.roll(p, 1, axis=0)))
        carry[:, o:o + w] = p[tt - 1:tt, :]
    vec = vec_ref[...]
    e = e_ref[...]
    r, lw, kmod, v, kk, b, g, bonus = _rwkv_elem(*cur, *prev, vec, wup_ref, e)

    ri = lax.broadcasted_iota(jnp.int32, (tt, tt), 0)
    ci = lax.broadcasted_iota(jnp.int32, (tt, tt), 1)
    same = (ri // RWKV_CHUNK) == (ci // RWKV_CHUNK)
    tri = (same & (ci <= ri)).astype(BF16)
    blk = same.astype(BF16)
    parts = _split3(lw)
    cum = sum(_dot(tri, p) for p in parts)
    tot = sum(_dot(blk, p) for p in parts)
    e_out = jnp.exp(-cum)
    to_end = jnp.exp(tot - cum)
    outs = ((rt_ref, r * jnp.exp(cum)), (at_ref, -kk * jnp.exp(cum - lw)), (bt_ref, b * e_out),
            (kt_ref, kmod * e_out), (v_ref, v), (bh_ref, b * to_end), (kh_ref, kmod * to_end),
            (gt_ref, jnp.exp(tot)))
    for ref, val in outs:
        for h in range(N_HEADS):
            ref[h] = val[:, h * HEAD_DIM:(h + 1) * HEAD_DIM]
    return g, bonus


def _bmm(a, b, ca, cb, passes):
    dn = (((ca,), (cb,)), ((0,), (0,)))
    d = lambda x, y: lax.dot_general(x, y, dn, preferred_element_type=F32)
    if passes == 1:
        return d(a.astype(BF16), b.astype(BF16))
    ah, al = _split2(a)
    bh, bl = _split2(b)
    return d(ah, bh) + d(ah, bl) + d(al, bh)


def _rwkv_chunks(head_refs, s0_ref, o_ref, st_ref, s_scr, passes):
    rt_ref, at_ref, bt_ref, kt_ref, v_ref, bh_ref, kh_ref, gt_ref = head_refs
    c = pl.program_id(1)

    @pl.when(c == 0)
    def _():
        s_scr[...] = s0_ref[0]

    n = RWKV_CHUNK
    nc = rt_ref.shape[1] // n
    shp = (1, n, n)
    ti = lax.broadcasted_iota(jnp.int32, shp, 1)
    tj = lax.broadcasted_iota(jnp.int32, shp, 2)
    strict, incl = tj < ti, tj <= ti
    same = (ti // INV_BLOCK) == (tj // INV_BLOCK)
    nn = lambda a, b, p: _bmm(a, b, 2, 1, p)
    nt = lambda a, b, p: _bmm(a, b, 2, 2, p)
    tn = lambda a, b, p: _bmm(a, b, 1, 1, p)
    p_gen, p_chain = passes

    per_chunk = lambda ref: ref[...].reshape(N_HEADS * nc, n, HEAD_DIM)
    rt, at, bt, kt, v, bh, kh = (per_chunk(r) for r in (rt_ref, at_ref, bt_ref, kt_ref, v_ref, bh_ref, kh_ref))
    g_end = per_chunk(gt_ref)[:, 0:1, :]
    big = nt(jnp.concatenate([at, rt], axis=1), jnp.concatenate([bt, kt], axis=1), p_gen)
    a_ab = jnp.where(strict, big[:, :n, :n], 0.0)
    a_ak = jnp.where(strict, big[:, :n, n:], 0.0)
    a_rb = jnp.where(incl, big[:, n:, :n], 0.0)
    a_rk = jnp.where(incl, big[:, n:, n:], 0.0)
    av = nn(jnp.concatenate([a_ak, a_rk], axis=1), v, p_gen)
    nd = jnp.where(same, a_ab, 0.0)
    no = jnp.where(same, 0.0, a_ab)
    nd2 = nn(nd, nd, p_gen)
    nd4 = nn(nd2, nd2, p_gen)
    nd8 = nn(nd4, nd4, p_gen)
    y = jnp.concatenate([no, at, av[:, :n]], axis=2)
    for pw in (nd8, nd4, nd2, nd):
        y = y + nn(pw, y, p_gen)
    m = y[:, :, :n]
    z = y[:, :, n:]
    m2 = nn(m, m, p_gen)
    z = z + nn(m2, z, p_gen)
    z = z + nn(m, z, p_gen)
    qo = jnp.concatenate([rt, av[:, n:]], axis=2) + nn(a_rb, z, p_gen)
    pg = tn(z, bh, p_gen)
    gm = pg[:, n:] + tn(v, kh, p_gen)

    by_chunk = lambda x: x.reshape((N_HEADS, nc) + x.shape[1:])
    qo, pmat, gm, g_end = by_chunk(qo), by_chunk(pg[:, :n]), by_chunk(gm), by_chunk(g_end)
    s = s_scr[...]
    for i in range(nc):
        o_ref[:, i * n:(i + 1) * n, :] = nt(qo[:, i, :, :HEAD_DIM], s, p_chain) + qo[:, i, :, HEAD_DIM:]
        s = s * g_end[:, i] + nn(s, pmat[:, i], p_chain) + gm[:, i]
    s_scr[...] = s

    @pl.when(c == pl.num_programs(1) - 1)
    def _():
        st_ref[0] = s


def _rwkv_prompt_kernel(pr_ref, pk_ref, pv_ref, pl_ref, sp_ref, vec_ref, wup_ref, e_ref, s0_ref,
                        ya_ref, st_ref, carry, s_scr, o_scr, *head_scr, passes):
    g, bonus = _rwkv_prep(pr_ref, pk_ref, pv_ref, pl_ref, sp_ref, vec_ref, wup_ref, e_ref, head_scr, carry)
    _rwkv_chunks(head_scr, s0_ref, o_scr, st_ref, s_scr, passes)
    o = jnp.concatenate([o_scr[h] for h in range(N_HEADS)], axis=-1)
    e = e_ref[...]
    vec = vec_ref[...]
    mu = _dot_exact_rhs(o, e) * (1.0 / HEAD_DIM)
    d = o - mu
    var = _dot_exact_rhs(d * d, e) * (1.0 / HEAD_DIM)
    on = d * lax.rsqrt(var + GN_EPS) * vec[V_LNG:V_LNG + 1, :] + vec[V_LNB:V_LNB + 1, :]
    ya_ref[0] = ((on + bonus) * g).astype(ya_ref.dtype)


def _rwkv_prompt(proj, nb, t, shift_prev, s0, vec, wup, e, passes):
    tt = RWKV_CHUNKS_PER_STEP * RWKV_CHUNK
    nt_ = t // tt
    gw, nh, hd = GROUP_WIDTH, N_HEADS, HEAD_DIM
    pspec = lambda cb: pl.BlockSpec((tt, gw), lambda b, i: (b * nt_ + i, cb))
    const = lambda shape: pl.BlockSpec(shape, lambda b, i: (0,) * len(shape))
    sspec = pl.BlockSpec((1, nh, hd, hd), lambda b, i: (b, 0, 0, 0))
    head_major = pltpu.VMEM((nh, tt, hd), F32)
    return pl.pallas_call(
        functools.partial(_rwkv_prompt_kernel, passes=passes),
        grid=(nb, nt_),
        in_specs=[pspec(CB_R), pspec(CB_RK), pspec(CB_RV),
                  pl.BlockSpec((tt, LANES), lambda b, i: (b * nt_ + i, CB_L * (gw // LANES))),
                  pl.BlockSpec((1, 1, 3 * gw + LANES), lambda b, i: (b, 0, 0)),
                  const(vec.shape), const(wup.shape), const(e.shape), sspec],
        out_specs=[pl.BlockSpec((1, tt, gw), lambda b, i: (b, i, 0)), sspec],
        out_shape=[jax.ShapeDtypeStruct((nb, t, gw), BF16), jax.ShapeDtypeStruct((nb, nh, hd, hd), F32)],
        scratch_shapes=[pltpu.VMEM((1, 3 * gw + LANES), F32), pltpu.VMEM((nh, hd, hd), F32), head_major]
                       + [head_major] * 8,
        compiler_params=_cp("arbitrary", "arbitrary"),
        name="rwkv_prompt",
    )(proj, proj, proj, proj, shift_prev, vec, wup, e, s0)


def _alibi_slope(h):
    return 2.0 ** (-8.0 * (h + 1) / N_HEADS)


LOG2E = 1.4426950408889634
MASKED = -1e30


def _eye(n):
    return (lax.broadcasted_iota(jnp.int32, (n, n), 0) == lax.broadcasted_iota(jnp.int32, (n, n), 1)).astype(BF16)


def _moba_prompt_kernel(q_ref, k_ref, v_ref, selm_ref, slope_ref, o_ref, kh_scr, vt_scr, km_scr, selt_scr, m_scr,
                        acc_scr, *, n_blocks):
    qi = pl.program_id(1)
    blk, nh, hd = MOBA_BLOCK, N_HEADS, HEAD_DIM
    nt_dims = (((1,), (1,)), ((), ()))

    @pl.when(qi == 0)
    def _():
        k = k_ref[...]
        v = v_ref[...]
        one_row = (lax.broadcasted_iota(jnp.int32, (hd, blk), 0) == 0).astype(BF16)
        eye_hd = _eye(hd)
        for h in range(nh):
            kh_scr[h] = k[:, h * hd:(h + 1) * hd].astype(BF16)
            vt = lax.dot_general(eye_hd, v[:, h * hd:(h + 1) * hd].astype(BF16), nt_dims,
                                 preferred_element_type=F32).astype(BF16)
            for n in range(n_blocks):
                vt_scr[n, h] = jnp.concatenate([vt[:, n * blk:(n + 1) * blk], one_row], axis=0)
        rows = [jnp.mean(k[n * blk:(n + 1) * blk, :], axis=0, keepdims=True) for n in range(n_blocks)]
        rows += [jnp.zeros((1, k.shape[1]), F32)] * (selm_ref.shape[0] - n_blocks)
        kmean = jnp.concatenate(rows, axis=0)
        tn_dims = (((0,), (0,)), ((), ()))
        spread = sum(lax.dot_general(p, selm_ref[...], tn_dims, preferred_element_type=F32)
                     for p in _split3(kmean))
        rh = lax.broadcasted_iota(jnp.int32, spread.shape, 0) // hd
        ch = lax.broadcasted_iota(jnp.int32, spread.shape, 1) % nh
        for i, p in enumerate(_split3(jnp.where(rh == ch, spread, 0.0))):
            km_scr[i] = p

    own = qi
    q = q_ref[...]
    q1, q2, q3 = _split3(q * (hd ** -0.5))
    k1, k2, k3 = km_scr[0], km_scr[1], km_scr[2]
    gate = (_dot(q1, k1) + _dot(q1, k2) + _dot(q2, k1)) + (_dot(q1, k3) + _dot(q2, k2) + _dot(q3, k1))
    lane_blk = lax.broadcasted_iota(jnp.int32, gate.shape, 1) // nh
    valid = lane_blk < own
    gate = jnp.where(valid, gate, -jnp.inf)
    rank = jnp.zeros_like(gate)
    for d in range(1, n_blocks):
        earlier = pltpu.roll(gate, nh * d, axis=1)
        later = pltpu.roll(gate, LANES - nh * d, axis=1)
        rank = rank + jnp.where(earlier >= gate, 1.0, 0.0) + jnp.where(later > gate, 1.0, 0.0)
    sel = jnp.where(valid & (rank < min(MOBA_TOPK, n_blocks - 1)), 1.0, 0.0)
    selt_scr[...] = lax.dot_general(_eye(LANES), sel.astype(BF16), nt_dims, preferred_element_type=F32)

    qa = q * ((hd ** -0.5) * LOG2E)
    qh = jnp.stack([qa[:, h * hd:(h + 1) * hd].astype(BF16) for h in range(nh)])
    ahead_i = (lax.broadcasted_iota(jnp.int32, (blk, blk), 1) - lax.broadcasted_iota(jnp.int32, (blk, blk), 0))
    causal = ahead_i >= 0
    ahead = ahead_i.astype(F32)
    slope2 = slope_ref[...] * LOG2E
    bnt = (((2,), (2,)), ((0,), (0,)))
    bnn = (((2,), (1,)), ((0,), (0,)))

    def scores(first_key):
        k_blk = kh_scr[:, pl.ds(first_key, blk), :]
        s = lax.dot_general(k_blk, qh, bnt, preferred_element_type=F32)
        return [s[h] - (_alibi_slope(h) * LOG2E) * ahead for h in range(nh)]

    tm = jnp.stack([jnp.where(causal, sh, MASKED) for sh in scores(pl.multiple_of(own * blk, blk))])
    m = jnp.max(tm, axis=1, keepdims=True)
    p = jnp.exp2(tm - m)
    m_scr[...] = m
    acc_scr[...] = lax.dot_general(vt_scr[own], p.astype(BF16), bnn, preferred_element_type=F32)

    def past_block(n, carry):
        shift = slope2 * ((own - n) * blk).astype(F32)
        sh = scores(pl.multiple_of(n * blk, blk))
        tm = jnp.stack([jnp.where(selt_scr[pl.ds(n * nh + h, 1), :] > 0.5, sh[h], MASKED) for h in range(nh)])
        m_old = m_scr[...]
        m_new = jnp.maximum(m_old, jnp.max(tm, axis=1, keepdims=True) - shift)
        p = jnp.exp2(tm - (m_new + shift))
        m_scr[...] = m_new
        acc_scr[...] = (jnp.exp2(m_old - m_new) * acc_scr[...]
                        + lax.dot_general(vt_scr[n], p.astype(BF16), bnn, preferred_element_type=F32))
        return carry

    lax.fori_loop(0, own, past_block, 0)
    acc = acc_scr[...]
    out_t = (acc[:, 0:hd, :] / acc[:, hd:hd + 1, :]).astype(BF16)
    eyes = jnp.broadcast_to(_eye(hd)[None], (nh, hd, hd))
    out = lax.dot_general(out_t, eyes, (((1,), (1,)), ((0,), (0,))), preferred_element_type=F32)
    o_ref[...] = jnp.concatenate([out[h] for h in range(nh)], axis=-1).astype(o_ref.dtype)


def _moba_prompt(proj, nb, t):
    blk, gw, nh, hd = MOBA_BLOCK, GROUP_WIDTH, N_HEADS, HEAD_DIM
    n_blocks = t // blk
    assert n_blocks * nh <= LANES // 2
    selm = (jnp.arange(8)[:, None] == jnp.arange(LANES)[None, :] // nh).astype(BF16)
    slopes = jnp.broadcast_to(jnp.array([_alibi_slope(h) for h in range(nh)], F32)[:, None, None], (nh, 1, blk))
    return pl.pallas_call(
        functools.partial(_moba_prompt_kernel, n_blocks=n_blocks),
        grid=(nb, n_blocks),
        in_specs=[pl.BlockSpec((blk, gw), lambda b, i: (b * n_blocks + i, CB_Q)),
                  pl.BlockSpec((t, gw), lambda b, i: (b, CB_K)),
                  pl.BlockSpec((t, gw), lambda b, i: (b, CB_V)),
                  pl.BlockSpec((8, LANES), lambda b, i: (0, 0)),
                  pl.BlockSpec((nh, 1, blk), lambda b, i: (0, 0, 0))],
        out_specs=pl.BlockSpec((blk, gw), lambda b, i: (b * n_blocks + i, 0)),
        out_shape=jax.ShapeDtypeStruct((nb * t, gw), BF16),
        scratch_shapes=[pltpu.VMEM((nh, t, hd), BF16), pltpu.VMEM((n_blocks, nh, 2 * hd, blk), BF16),
                        pltpu.VMEM((3, gw, LANES), BF16), pltpu.VMEM((LANES, blk), F32),
                        pltpu.VMEM((nh, 1, blk), F32), pltpu.VMEM((nh, 2 * hd, blk), F32)],
        compiler_params=_cp("arbitrary", "arbitrary"),
        name="moba_prompt",
    )(proj, proj, proj, selm, slopes)


def _sgu_prompt_kernel(pu_ref, pv_ref, w_ref, bias_ref, ln_ref, o_ref):
    u = jax.nn.gelu(pu_ref[...])
    v = _layer_norm(jax.nn.gelu(pv_ref[...]), ln_ref[0:1, :], ln_ref[1:2, :]).astype(BF16)
    cl, hd = SG_CHUNK, HEAD_DIM
    ri = lax.broadcasted_iota(jnp.int32, (cl, cl), 0)
    ci = lax.broadcasted_iota(jnp.int32, (cl, cl), 1)
    ws = [jnp.where(ci <= ri, w_ref[h], 0.0).astype(BF16) for h in range(N_HEADS)]
    bias = bias_ref[...]
    for c in range(u.shape[0] // cl):
        vc = v[c * cl:(c + 1) * cl, :]
        mixed = jnp.concatenate([_dot(ws[h], vc[:, h * hd:(h + 1) * hd]) for h in range(N_HEADS)], axis=-1)
        o_ref[c * cl:(c + 1) * cl, :] = (u[c * cl:(c + 1) * cl, :] * (mixed + bias)).astype(o_ref.dtype)


def _sgu_prompt(proj, nrows, w, bias_exp, ln):
    tt, gw = 256, GROUP_WIDTH
    return pl.pallas_call(
        _sgu_prompt_kernel,
        grid=(nrows // tt,),
        in_specs=[pl.BlockSpec((tt, gw), lambda i: (i, CB_SU)),
                  pl.BlockSpec((tt, gw), lambda i: (i, CB_SV)),
                  pl.BlockSpec(w.shape, lambda i: (0, 0, 0)),
                  pl.BlockSpec(bias_exp.shape, lambda i: (0, 0)),
                  pl.BlockSpec(ln.shape, lambda i: (0, 0))],
        out_specs=pl.BlockSpec((tt, gw), lambda i: (i, 0)),
        out_shape=jax.ShapeDtypeStruct((nrows, gw), BF16),
        compiler_params=_cp("arbitrary"),
        name="sgu_prompt",
    )(proj, proj, w, bias_exp, ln)


def _conv_prompt_kernel(pa_ref, pg_ref, prev_ref, w_ref, vec_ref, o_ref, hist_ref, zf):
    ti = pl.program_id(1)
    tt = pa_ref.shape[0]
    halo = CONV_HALO

    @pl.when(ti == 0)
    def _():
        zf[0:halo, :] = prev_ref[0]

    z = pa_ref[...] * jax.nn.sigmoid(pg_ref[...])
    zf[halo:halo + tt, :] = z
    lead = halo - (CONV_WIDTH - 1)
    zall = zf[...]
    n_rows = zall.shape[0]
    sub = 8
    rotated = [zall if r == 0 else pltpu.roll(zall, n_rows - r, axis=0) for r in range(sub)]
    acc = jnp.zeros_like(z)
    for i in range(CONV_WIDTH):
        a, r = divmod(lead + i, sub)
        acc = acc + w_ref[i:i + 1, :] * rotated[r][a * sub:a * sub + tt, :]
    y = _layer_norm(acc + vec_ref[0:1, :], vec_ref[1:2, :], vec_ref[2:3, :])
    o_ref[...] = (y * jax.nn.sigmoid(y)).astype(o_ref.dtype)
    tail = zf[pl.ds(tt, halo), :]
    hist_ref[0] = tail
    zf[0:halo, :] = tail


def _conv_prompt(proj, nb, t, prev, w, vec):
    tt, gw = 256, GROUP_WIDTH
    nt_ = t // tt
    return pl.pallas_call(
        _conv_prompt_kernel,
        grid=(nb, nt_),
        in_specs=[pl.BlockSpec((tt, gw), lambda b, i: (b * nt_ + i, CB_CA)),
                  pl.BlockSpec((tt, gw), lambda b, i: (b * nt_ + i, CB_CG)),
                  pl.BlockSpec((1, CONV_HALO, gw), lambda b, i: (b, 0, 0)),
                  pl.BlockSpec(w.shape, lambda b, i: (0, 0)),
                  pl.BlockSpec(vec.shape, lambda b, i: (0, 0))],
        out_specs=[pl.BlockSpec((tt, gw), lambda b, i: (b * nt_ + i, 0)),
                   pl.BlockSpec((1, CONV_HALO, gw), lambda b, i: (b, 0, 0))],
        out_shape=[jax.ShapeDtypeStruct((nb * t, gw), BF16),
                   jax.ShapeDtypeStruct((nb, CONV_HALO, gw), F32)],
        scratch_shapes=[pltpu.VMEM((CONV_HALO + tt, gw), F32)],
        compiler_params=_cp("arbitrary", "arbitrary"),
        name="conv_prompt",
    )(proj, proj, prev, w, vec)


def _sample_elem_kernel(pr_ref, pk_ref, pv_ref, pl_ref, sp_ref, vec_ref, wup_ref, e_ref,
                        su_ref, sv_ref, sgp_ref, ca_ref, cg_ref, cprev_ref, cw_ref, cvec_ref,
                        r_ref, w_ref, k_ref, v_ref, kk_ref, b_ref, g_ref, bonus_ref,
                        yc_ref, sgv_ref, yd_ref, cnew_ref):
    gw = GROUP_WIDTH
    sp = sp_ref[...]
    r, lw, kmod, v, kk, b, g, bonus = _rwkv_elem(
        pr_ref[...], pk_ref[...], pv_ref[...], pl_ref[...],
        sp[:, 0:gw], sp[:, gw:2 * gw], sp[:, 2 * gw:3 * gw], sp[:, 3 * gw:3 * gw + LANES],
        vec_ref[...], wup_ref, e_ref[...], precise=True)
    for ref, val in ((r_ref, r), (w_ref, jnp.exp(lw)), (k_ref, kmod), (v_ref, v), (kk_ref, kk),
                     (b_ref, b), (g_ref, g), (bonus_ref, bonus)):
        ref[...] = val
    u = jax.nn.gelu(su_ref[...])
    sv = _layer_norm(jax.nn.gelu(sv_ref[...]), sgp_ref[0:1, :], sgp_ref[1:2, :])
    sgv_ref[...] = sv
    yc_ref[...] = (u * (sgp_ref[2:3, :] * sv + sgp_ref[3:4, :])).astype(yc_ref.dtype)
    z = ca_ref[...] * jax.nn.sigmoid(cg_ref[...])
    nprev = CONV_WIDTH - 1
    acc = cw_ref[nprev:nprev + 1, :] * z
    for i in range(nprev):
        acc = acc + cw_ref[i:i + 1, :] * cprev_ref[i]
        if i > 0:
            cnew_ref[i - 1] = cprev_ref[i]
    cnew_ref[nprev - 1] = z
    y = _layer_norm(acc + cvec_ref[0:1, :], cvec_ref[1:2, :], cvec_ref[2:3, :])
    yd_ref[...] = (y * jax.nn.sigmoid(y)).astype(yd_ref.dtype)


def _sample_wkv_kernel(r_ref, w_ref, k_ref, v_ref, kk_ref, b_ref, g_ref, bonus_ref, ln_ref, s_ref,
                       ya_ref, st_ref):
    hd = HEAD_DIM
    r, w, k, v, kk, b, g, bonus = (x[0] for x in (r_ref, w_ref, k_ref, v_ref, kk_ref, b_ref, g_ref, bonus_ref))
    s = s_ref[0]
    eye = (lax.broadcasted_iota(jnp.int32, (hd, hd), 0) == lax.broadcasted_iota(jnp.int32, (hd, hd), 1))
    eye = eye.astype(F32)[None]
    v_col = jnp.sum(eye * v, axis=2, keepdims=True)
    sa = jnp.sum(s * (-kk), axis=2, keepdims=True)
    s_new = s * w + sa * b + v_col * k
    st_ref[0] = s_new
    o_col = jnp.sum(s_new * r, axis=2, keepdims=True)
    o = jnp.sum(o_col * eye, axis=1, keepdims=True)
    mu = jnp.mean(o, axis=-1, keepdims=True)
    d = o - mu
    var = jnp.mean(d * d, axis=-1, keepdims=True)
    on = d * lax.rsqrt(var + GN_EPS) * ln_ref[0] + ln_ref[1]
    ya_ref[0] = (on + bonus) * g


def _sample_mixers(proj, shift_prev, s0, conv_prev_t, vec, wup, e, sgp, cw, cvec):
    nb = proj.shape[0]
    gw, nh, hd = GROUP_WIDTH, N_HEADS, HEAD_DIM
    col = lambda cb: pl.BlockSpec((nb, gw), lambda i: (0, cb))
    full = lambda x: pl.BlockSpec(x.shape, lambda i: (0,) * x.ndim)
    row_f32 = jax.ShapeDtypeStruct((nb, gw), F32)
    outs = pl.pallas_call(
        _sample_elem_kernel,
        grid=(1,),
        in_specs=[col(CB_R), col(CB_RK), col(CB_RV),
                  pl.BlockSpec((nb, LANES), lambda i: (0, CB_L * (gw // LANES))),
                  full(shift_prev), full(vec), full(wup), full(e),
                  col(CB_SU), col(CB_SV), full(sgp), col(CB_CA), col(CB_CG),
                  full(conv_prev_t), full(cw), full(cvec)],
        out_specs=[pl.BlockSpec((nb, gw), lambda i: (0, 0))] * 11 + [full(conv_prev_t)],
        out_shape=[row_f32] * 11 + [jax.ShapeDtypeStruct(conv_prev_t.shape, F32)],
        compiler_params=_cp("arbitrary"),
        name="sample_elem",
    )(proj, proj, proj, proj, shift_prev, vec, wup, e, proj, proj, sgp, proj, proj, conv_prev_t, cw, cvec)
    heads = [x.reshape(nb, nh, 1, hd) for x in outs[:8]]
    yc, sgv, yd, conv_new_t = outs[8:]
    ln = jnp.stack([vec[V_LNG].reshape(nh, 1, hd), vec[V_LNB].reshape(nh, 1, hd)])
    hspec = pl.BlockSpec((1, nh, 1, hd), lambda b: (b, 0, 0, 0))
    sspec = pl.BlockSpec((1, nh, hd, hd), lambda b: (b, 0, 0, 0))
    ya, s_new = pl.pallas_call(
        _sample_wkv_kernel,
        grid=(nb,),
        in_specs=[hspec] * 8 + [pl.BlockSpec(ln.shape, lambda b: (0, 0, 0, 0)), sspec],
        out_specs=[hspec, sspec],
        out_shape=[jax.ShapeDtypeStruct((nb, nh, 1, hd), F32), jax.ShapeDtypeStruct(s0.shape, F32)],
        compiler_params=_cp("arbitrary"),
        name="sample_wkv",
    )(*heads, ln, s0)
    return ya.reshape(nb, gw), s_new, yc, sgv, yd, conv_new_t


PAGES_PER_STEP = 32


def _moba_scores_kernel(pt_ref, q_ref, *refs):
    pages, o_ref = refs[:-1], refs[-1]
    ps = pages[0].shape[4]
    qb = jnp.broadcast_to(q_ref[0] * (HEAD_DIM ** -0.5), pages[0].shape[2:])
    for i, p_ref in enumerate(pages):
        rows = [jnp.sum(p_ref[0, 0, h] * qb[h], axis=0, keepdims=True) for h in range(N_HEADS)]
        o_ref[0, :, i * ps:(i + 1) * ps] = jnp.concatenate(rows, axis=0)


def _moba_select_kernel(s_ref, q_ref, kn_ref, bi_ref, slope_ref, p_ref, p0_ref, *, n_past_blocks):
    s = s_ref[...]
    length = s.shape[1]
    bi = bi_ref[...]
    s0 = jnp.sum(q_ref[...] * (HEAD_DIM ** -0.5) * kn_ref[...], axis=1, keepdims=True)
    gate = sum(_dot(p, bi) for p in _split3(s)) * (1.0 / MOBA_BLOCK)
    lane = lax.broadcasted_iota(jnp.int32, gate.shape, 1)
    gate = jnp.where(lane < n_past_blocks, gate, -jnp.inf)
    sel = jnp.zeros(gate.shape, jnp.bool_)
    for _ in range(min(MOBA_TOPK, n_past_blocks)):
        m = jnp.max(gate, axis=1, keepdims=True)
        idx = jnp.min(jnp.where(gate == m, lane, LANES), axis=1, keepdims=True)
        pick = lane == idx
        sel = sel | pick
        gate = jnp.where(pick, -jnp.inf, gate)
    picked = lax.dot_general(sel.astype(BF16), bi, (((1,), (1,)), ((), ())),
                             preferred_element_type=F32) > 0.5
    pos = lax.broadcasted_iota(jnp.int32, s.shape, 1)
    dist = (length - pos).astype(F32)
    logit = jnp.where(picked, s - slope_ref[:, 0:1] * dist, -jnp.inf)
    m = jnp.maximum(jnp.max(logit, axis=1, keepdims=True), s0)
    p = jnp.exp(logit - m)
    p0 = jnp.exp(s0 - m)
    inv = 1.0 / (jnp.sum(p, axis=1, keepdims=True) + p0)
    p_ref[...] = p * inv
    p0_ref[...] = jnp.broadcast_to(p0 * inv, p0_ref.shape)


def _moba_pv_kernel(pt_ref, p_ref, p0_ref, vn_ref, *refs, n_steps):
    pages, o_ref, acc = refs[:-2], refs[-2], refs[-1]
    j = pl.program_id(1)
    ps = pages[0].shape[4]

    @pl.when(j == 0)
    def _():
        acc[...] = jnp.zeros_like(acc)

    for h in range(N_HEADS):
        tot = acc[h]
        for i, v_ref in enumerate(pages):
            tot = tot + v_ref[0, 0, h] * p_ref[0, h:h + 1, i * ps:(i + 1) * ps]
        acc[h] = tot

    @pl.when(j == n_steps - 1)
    def _():
        for h in range(N_HEADS):
            o_ref[0, h] = (jnp.sum(acc[h], axis=1, keepdims=True)
                           + p0_ref[0, h:h + 1, 0:1] * vn_ref[0, h])


def _moba_sample(proj, cache_k, cache_v, layer, page_table):
    nb, n_pages = page_table.shape
    ps, gw, nh, hd = cache_k.shape[2], GROUP_WIDTH, N_HEADS, HEAD_DIM
    length = n_pages * ps
    assert length % MOBA_BLOCK == 0
    n_past_blocks = length // MOBA_BLOCK
    pp = min(PAGES_PER_STEP, n_pages)
    n_steps = n_pages // pp
    q, kn, vn = (proj[:, cb * gw:(cb + 1) * gw] for cb in (CB_Q, CB_K, CB_V))
    cache_kt = cache_k.transpose(0, 1, 3, 4, 2)
    cache_vt = cache_v.transpose(0, 1, 3, 4, 2)
    col4 = lambda x: x.reshape(nb, nh, hd, 1)
    col_spec = pl.BlockSpec((1, nh, hd, 1), lambda b, j, pt: (b, 0, 0, 0))

    def page_spec(i):
        return pl.BlockSpec((1, 1, nh, hd, ps), lambda b, j, pt: (layer, pt[b, j * pp + i], 0, 0, 0))

    scores = pl.pallas_call(
        _moba_scores_kernel,
        grid_spec=pltpu.PrefetchScalarGridSpec(
            num_scalar_prefetch=1,
            grid=(nb, n_steps),
            in_specs=[col_spec] + [page_spec(i) for i in range(pp)],
            out_specs=pl.BlockSpec((1, nh, pp * ps), lambda b, j, pt: (b, 0, j)),
        ),
        out_shape=jax.ShapeDtypeStruct((nb, nh, length), F32),
        compiler_params=_cp("arbitrary", "arbitrary"),
        name="moba_scores",
    )(page_table, col4(q), *([cache_kt] * pp))

    bi = (jnp.arange(length)[:, None] // MOBA_BLOCK == jnp.arange(LANES)[None, :]).astype(BF16)
    rows = nb * nh
    tr = min(rows, 128)
    slopes = jnp.broadcast_to(
        jnp.tile(jnp.array([_alibi_slope(h) for h in range(nh)], F32), nb)[:, None], (rows, LANES))
    p, p0 = pl.pallas_call(
        functools.partial(_moba_select_kernel, n_past_blocks=n_past_blocks),
        grid=(rows // tr,),
        in_specs=[pl.BlockSpec((tr, length), lambda i: (i, 0)),
                  pl.BlockSpec((tr, hd), lambda i: (i, 0)),
                  pl.BlockSpec((tr, hd), lambda i: (i, 0)),
                  pl.BlockSpec((length, LANES), lambda i: (0, 0)),
                  pl.BlockSpec((tr, LANES), lambda i: (i, 0))],
        out_specs=[pl.BlockSpec((tr, length), lambda i: (i, 0)),
                   pl.BlockSpec((tr, LANES), lambda i: (i, 0))],
        out_shape=[jax.ShapeDtypeStruct((rows, length), F32), jax.ShapeDtypeStruct((rows, LANES), F32)],
        compiler_params=_cp("arbitrary"),
        name="moba_select",
    )(scores.reshape(rows, length), q.reshape(rows, hd), kn.reshape(rows, hd), bi, slopes)

    out = pl.pallas_call(
        functools.partial(_moba_pv_kernel, n_steps=n_steps),
        grid_spec=pltpu.PrefetchScalarGridSpec(
            num_scalar_prefetch=1,
            grid=(nb, n_steps),
            in_specs=[pl.BlockSpec((1, nh, pp * ps), lambda b, j, pt: (b, 0, j)),
                      pl.BlockSpec((1, nh, LANES), lambda b, j, pt: (b, 0, 0)),
                      col_spec]
                     + [page_spec(i) for i in range(pp)],
            out_specs=col_spec,
            scratch_shapes=[pltpu.VMEM((nh, hd, ps), F32)],
        ),
        out_shape=jax.ShapeDtypeStruct((nb, nh, hd, 1), F32),
        compiler_params=_cp("arbitrary", "arbitrary"),
        name="moba_pv",
    )(page_table, p.reshape(nb, nh, length), p0.reshape(nb, nh, LANES), col4(vn), *([cache_vt] * pp))
    return out.reshape(nb, gw)


RWKV_PASSES = (1, 1)


def _shift_layout(shift):
    n = shift.shape[0]
    pad = jnp.zeros((n, LANES - 3 * TM_LORA), F32)
    return jnp.concatenate([shift, pad], axis=1)


def _layer_params(l, w_in, w_out, w_ffn_in, w_ffn_out, tm_mu, tm_w0, tm_w_up, tm_a0, tm_a_up, tm_g_up,
                  tm_k_k, tm_k_a, tm_r_k, tm_ln_g, tm_ln_b, sg_ws, sg_bs, sg_ln_g, sg_ln_b,
                  cv_w, cv_b, cv_ln_g, cv_ln_b):
    gw, lr, hd = GROUP_WIDTH, TM_LORA, HEAD_DIM
    wt = w_in[l].T
    w_in_t = jnp.concatenate([wt[A_PROJ:], wt[:A_PROJ], jnp.zeros((IN_PAD - IN_WIDTH, wt.shape[1]), F32)], axis=0)
    mu = tm_mu[l]
    mu_l = jnp.concatenate([mu[3 * gw:], jnp.zeros((gw - 3 * lr,), F32)])
    vec = jnp.stack([mu[0:gw], mu[gw:2 * gw], mu[2 * gw:3 * gw], tm_w0[l], tm_a0[l], tm_k_k[l], tm_k_a[l],
                     tm_r_k[l].reshape(gw), tm_ln_g[l], tm_ln_b[l], mu_l]
                    + [jnp.zeros((gw,), F32)] * 5)
    wup = jnp.zeros((3, LANES, gw), F32)
    wup = wup.at[0, 0:lr].set(tm_w_up[l]).at[1, lr:2 * lr].set(tm_a_up[l]).at[2, 2 * lr:3 * lr].set(tm_g_up[l])
    sg_bias = jnp.repeat(sg_bs[l].T, hd, axis=1)
    sg_ln = jnp.stack([sg_ln_g[l], sg_ln_b[l]] + [jnp.zeros((gw,), F32)] * 6)
    sgp = jnp.stack([sg_ln_g[l], sg_ln_b[l], jnp.repeat(sg_ws[l][:, 0, 0], hd), jnp.repeat(sg_bs[l][:, 0], hd)]
                    + [jnp.zeros((gw,), F32)] * 4)
    cw = jnp.concatenate([cv_w[l], jnp.zeros((1, gw), F32)], axis=0)
    cvec = jnp.stack([cv_b[l], cv_ln_g[l], cv_ln_b[l]] + [jnp.zeros((gw,), F32)] * 5)
    return dict(vec=vec, wup=wup, sg_w=sg_ws[l], sg_bias=sg_bias, sg_ln=sg_ln, sgp=sgp, cw=cw, cvec=cvec,
                w_in_t=w_in_t[None], dense_f32=(w_out, w_ffn_in, w_ffn_out))


def _prompt_dense_tail(x, ys, mod, p, g_ffn, layer):
    ng, t, d = x.shape
    m = ng * t
    w_out, w_ffn_in, w_ffn_out = p['dense_f32']
    x1, h2 = _mix_out(ys, w_out[layer].astype(BF16), x.reshape(m, d), mod, g_ffn, 512, t)
    hid = _mm_swiglu(h2, w_ffn_in, layer, 2048, 512)
    x2 = _mm_rows([hid], w_ffn_out[layer].astype(BF16), 1024, 512, res=x1, gate=mod, gate_col=5, rows_per_group=t)
    return x2.reshape(ng, t, d)


def _prompt_layer(x, mod, p, g_mix, g_ffn, e, layer):
    nb, t, d = x.shape
    m = nb * t
    gw, nh, hd = GROUP_WIDTH, N_HEADS, HEAD_DIM
    h = _norm_mod(x, g_mix, mod, 0, 1, 512).reshape(m, d)
    proj = _mm([h], p['w_in_t'], 0, 2048, 512, w_t=True)
    shift0 = jnp.zeros((nb, 1, 3 * gw + LANES), F32)
    s0 = jnp.zeros((nb, nh, hd, hd), F32)
    ya, wkv = _rwkv_prompt(proj, nb, t, shift0, s0, p['vec'], p['wup'], e, RWKV_PASSES)
    yb = _moba_prompt(proj, nb, t)
    yc = _sgu_prompt(proj, m, p['sg_w'], p['sg_bias'], p['sg_ln'])
    yd, hist = _conv_prompt(proj, nb, t, jnp.zeros((nb, CONV_HALO, gw), F32), p['cw'], p['cvec'])
    x2 = _prompt_dense_tail(x, [ya.reshape(m, gw), yb, yc, yd], mod, p, g_ffn, layer)
    proj3 = proj.reshape(nb, t, IN_PAD)
    last = proj3[:, t - 1]
    shift = jnp.concatenate([last[:, CB_R * gw:CB_L * gw], last[:, CB_L * gw:CB_L * gw + 3 * TM_LORA]], axis=1)
    new_k = proj3[:, :, CB_K * gw:(CB_K + 1) * gw].reshape(nb, t, nh, hd)
    new_v = proj3[:, :, CB_V * gw:(CB_V + 1) * gw].reshape(nb, t, nh, hd)
    return x2, (shift, wkv, new_k, new_v, hist[:, CONV_HALO - (CONV_WIDTH - 1):])


def _sample_layer(x, mod, p, g_mix, g_ffn, e, layer, shift_prev, wkv_prev, conv_prev, cache_k, cache_v,
                  page_table):
    nb, t, d = x.shape
    gw, nh, hd = GROUP_WIDTH, N_HEADS, HEAD_DIM
    xg = x.reshape(1, nb, d)
    h = _norm_mod(xg, g_mix, mod, 0, 1, nb, dtype=F32).reshape(nb, d)
    proj = _mm_precise([h], p['w_in_t'], 0, 512, mode="plain_t")
    ya, wkv, yc, sgv, yd, conv_new_t = _sample_mixers(
        proj, _shift_layout(shift_prev), wkv_prev, conv_prev.transpose(1, 0, 2),
        p['vec'], p['wup'], e, p['sgp'], p['cw'], p['cvec'])
    yb = _moba_sample(proj, cache_k, cache_v, layer, page_table)
    w_out, w_ffn_in, w_ffn_out = p['dense_f32']
    x1 = _mm_precise([ya, yb, yc, yd], w_out, layer, 512, mode="gated", res=x.reshape(nb, d), gate=mod, gate_col=2)
    h2 = _norm_mod(x1.reshape(1, nb, d), g_ffn, mod, 3, 4, nb, dtype=F32).reshape(nb, d)
    hid = _mm_precise([h2], w_ffn_in, layer, 512, mode="swiglu")
    x2 = _mm_precise([hid], w_ffn_out, layer, 256, mode="gated", res=x1, gate=mod, gate_col=5)
    shift = jnp.concatenate([proj[:, CB_R * gw:CB_L * gw], proj[:, CB_L * gw:CB_L * gw + 3 * TM_LORA]], axis=1)
    new_k = proj[:, CB_K * gw:(CB_K + 1) * gw].reshape(nb, 1, nh, hd)
    new_v = proj[:, CB_V * gw:(CB_V + 1) * gw].reshape(nb, 1, nh, hd)
    return x2.reshape(nb, t, d), (shift, wkv, new_k, new_v, sgv.reshape(nb, 1, gw), conv_new_t.transpose(1, 0, 2))


def kernel(x_prompt, x_sample, c_prompt, c_sample, cache_k_b, cache_v_b, state_a_wkv, state_a_shift, state_d_conv, page_table, w_ada, b_ada, g_norm_mix, w_in, w_out, tm_mu, tm_w0, tm_w_up, tm_a0, tm_a_up, tm_g_up, tm_k_k, tm_k_a, tm_r_k, tm_ln_g, tm_ln_b, sg_ws, sg_bs, sg_ln_g, sg_ln_b, cv_w, cv_b, cv_ln_g, cv_ln_b, g_norm_ffn, w_ffn_in, w_ffn_out, g_norm_final):
    depth = w_in.shape[0]
    bp, ns = x_prompt.shape[0], x_sample.shape[0]
    gw = GROUP_WIDTH
    rows = bp + ns
    rows_pad = -(-rows // 8) * 8
    c_all = jnp.concatenate([c_prompt, c_sample, jnp.zeros((rows_pad - rows, c_prompt.shape[1]), F32)], axis=0)
    mod = _ada(c_all, w_ada, b_ada)
    e = _head_ones()
    xp, xs = x_prompt, x_sample
    st_p, st_s = [], []
    for l in range(depth):
        p = _layer_params(l, w_in, w_out, w_ffn_in, w_ffn_out, tm_mu, tm_w0, tm_w_up, tm_a0, tm_a_up, tm_g_up,
                          tm_k_k, tm_k_a, tm_r_k, tm_ln_g, tm_ln_b, sg_ws, sg_bs, sg_ln_g, sg_ln_b,
                          cv_w, cv_b, cv_ln_g, cv_ln_b)
        mod_p = mod[l, :bp][:, None, :]
        mod_s = mod[l, bp:rows][None]
        xp, sp = _prompt_layer(xp, mod_p, p, g_norm_mix[l], g_norm_ffn[l], e, l)
        xs, ss = _sample_layer(xs, mod_s, p, g_norm_mix[l], g_norm_ffn[l], e, l, state_a_shift[l],
                               state_a_wkv[l], state_d_conv[l], cache_k_b, cache_v_b, page_table)
        st_p.append(sp)
        st_s.append(ss)
    y_prompt = _norm(xp, g_norm_final, 512)
    y_sample = _norm(xs.reshape(1, ns, -1), g_norm_final, ns).reshape(xs.shape)
    stack = lambda sts, i: jnp.stack([s[i] for s in sts])
    return (y_prompt, y_sample, stack(st_p, 2), stack(st_p, 3), stack(st_s, 2), stack(st_s, 3),
            stack(st_p, 1), stack(st_s, 1), stack(st_p, 0), stack(st_s, 0), stack(st_p, 4), stack(st_s, 5),
            stack(st_s, 4))
```
